```python
import math
import jax, jax.numpy as jnp
from jax import lax
import numpy as np

D_MODEL = 1024
BATCH = 32
SEQ = 2048
DEPTH = 4

GRID_W = 64
CTX_LEN = 256
HEAD_DIM = 64
N_MIX_HEADS = D_MODEL // HEAD_DIM
MLSTM_HEADS = N_MIX_HEADS // 4
MLSTM_DH = HEAD_DIM
GLA_HEADS = N_MIX_HEADS // 4
GLA_DV = HEAD_DIM
GLA_DK = HEAD_DIM // 2
GLA_RANK = 16
GLA_TAU = 16.0
NAT_HEADS = N_MIX_HEADS // 2
NAT_DH = HEAD_DIM
WIN_ROWS = 8
WIN_COLS = 16
CONV_K = 3
CHUNK = 64
ROPE_BASE = 10000.0
N_EXPERTS = 16
EXPERT_FF = 2 * D_MODEL
CAPACITY_FACTOR = 2
LN_EPS = 1e-5
DEEPNORM_ALPHA = (2 * DEPTH) ** 0.25
DEEPNORM_BETA = (8 * DEPTH) ** -0.25

MLSTM_W = MLSTM_HEADS * MLSTM_DH
GLA_W = GLA_HEADS * GLA_DV
GLA_KW = GLA_HEADS * GLA_DK
NAT_W = NAT_HEADS * NAT_DH
MIX_W = MLSTM_W + GLA_W + NAT_W

SPLIT_NAMES = ("m_q", "m_k", "m_v", "m_o", "m_i_f", "m_f_f", "m_i_b", "m_f_b",
               "g_q", "g_k", "g_v", "g_r", "g_lr_f", "g_lr_b",
               "n_q", "n_k", "n_v")
SPLIT_SIZES = (MLSTM_W, MLSTM_W, MLSTM_W, MLSTM_W, MLSTM_HEADS, MLSTM_HEADS, MLSTM_HEADS, MLSTM_HEADS,
               GLA_KW, GLA_KW, GLA_W, GLA_W, GLA_RANK, GLA_RANK,
               NAT_W, NAT_W, NAT_W)
PROJ_W = sum(SPLIT_SIZES)

kernel_name = "hybrid_mlstm_gla_natten_ec_moe_dit"


def layer_norm(x, g, b):
    xf = x.astype(jnp.float32)
    mu = xf.mean(-1, keepdims=True)
    var = jnp.mean(jnp.square(xf - mu), -1, keepdims=True)
    return ((xf - mu) * lax.rsqrt(var + LN_EPS) * g + b).astype(x.dtype)


def split_heads(x, h):
    B, T, F = x.shape
    return x.reshape(B, T, h, F // h).transpose(0, 2, 1, 3)


def merge_heads(x):
    B, H, T, d = x.shape
    return x.transpose(0, 2, 1, 3).reshape(B, T, H * d)


def head_norm(h, g):
    mu = h.mean(-1, keepdims=True)
    var = jnp.mean(jnp.square(h - mu), -1, keepdims=True)
    return merge_heads((h - mu) * lax.rsqrt(var + LN_EPS)) * g


def centred_conv(x, w):
    K = w.shape[0]
    T = x.shape[1]
    pad = K // 2
    xp = jnp.pad(x, ((0, 0), (pad, pad), (0, 0)))
    out = xp[:, 0:T] * w[0]
    for j in range(1, K):
        out = out + xp[:, j:j + T] * w[j]
    return out


def rope_1d(x, pos):
    d2 = x.shape[-1] // 2
    inv = ROPE_BASE ** (-jnp.arange(d2, dtype=jnp.float32) / d2)
    ang = pos[:, None] * inv[None, :]
    cos, sin = jnp.cos(ang), jnp.sin(ang)
    x1, x2 = x[..., :d2], x[..., d2:]
    return jnp.concatenate([x1 * cos - x2 * sin, x1 * sin + x2 * cos], -1)


def axial_rope(x, rows, cols):
    h = x.shape[-1] // 2
    return jnp.concatenate([rope_1d(x[..., :h], rows), rope_1d(x[..., h:], cols)], -1)


def to_chunks(a):
    B, H, T = a.shape[:3]
    return jnp.moveaxis(a.reshape(B, H, T // CHUNK, CHUNK, *a.shape[3:]), 2, 0)


def from_chunks(a):
    a = jnp.moveaxis(a, 0, 2)
    return a.reshape(a.shape[0], a.shape[1], -1, *a.shape[4:])


def mlstm_scan(q, k, v, li, lf, state):
    tri = jnp.tril(jnp.ones((CHUNK, CHUNK), dtype=bool))

    def step(carry, blk):
        C, n, m = carry
        qc, kc, vc, ic, fc = blk
        b = jnp.cumsum(fc, axis=-1)
        logw = jnp.where(tri, b[..., :, None] - b[..., None, :] + ic[..., None, :], -jnp.inf)
        inter = b + m[..., None]
        m_t = jnp.maximum(inter, logw.max(-1))
        s = jnp.einsum('bhtd,bhsd->bhts', qc, kc) * jnp.exp(logw - m_t[..., None])
        e = jnp.exp(inter - m_t)
        num = jnp.einsum('bhts,bhsv->bhtv', s, vc) + e[..., None] * jnp.einsum('bhvd,bhtd->bhtv', C, qc)
        den = s.sum(-1) + e * jnp.einsum('bhd,bhtd->bht', n, qc)
        h = num / jnp.maximum(jnp.abs(den), jnp.exp(-m_t))[..., None]
        bl = b[..., -1]
        lw_end = bl[..., None] - b + ic
        m_new = jnp.maximum(bl + m, lw_end.max(-1))
        w_end = jnp.exp(lw_end - m_new[..., None])
        decay = jnp.exp(bl + m - m_new)
        C_new = decay[..., None, None] * C + jnp.einsum('bhs,bhsv,bhsd->bhvd', w_end, vc, kc)
        n_new = decay[..., None] * n + jnp.einsum('bhs,bhsd->bhd', w_end, kc)
        return (C_new, n_new, m_new), h

    state, h = lax.scan(step, state, tuple(to_chunks(a) for a in (q, k, v, li, lf)))
    return from_chunks(h), state


def gla_scan(q, k, v, a, state):
    tri = jnp.tril(jnp.ones((CHUNK, CHUNK), dtype=bool))[:, :, None]

    def step(S, blk):
        qc, kc, vc, ac = blk
        b = jnp.cumsum(ac, axis=2)
        expo = jnp.where(tri, b[:, :, :, None, :] - b[:, :, None, :, :], -jnp.inf)
        A = jnp.einsum('bhtd,bhsd,bhtsd->bhts', qc, kc, jnp.exp(expo))
        o = jnp.einsum('bhts,bhsv->bhtv', A, vc) + jnp.einsum('bhtd,bhdv->bhtv', qc * jnp.exp(b), S)
        bl = b[:, :, -1:, :]
        S_new = jnp.exp(bl[:, :, 0])[..., None] * S + jnp.einsum('bhsd,bhsv->bhdv', kc * jnp.exp(bl - b), vc)
        return S_new, o

    state, o = lax.scan(step, state, tuple(to_chunks(t) for t in (q, k, v, a)))
    return from_chunks(o), state


def bidir(scan, ctx_f, lat_f, ctx_b, lat_b, init):
    flip = lambda seq: tuple(jnp.flip(t, axis=2) for t in seq)
    oc_f, st_f = scan(*ctx_f, init)
    ol_f, _ = scan(*lat_f, st_f)
    oc_b, st_b = scan(*flip(ctx_b), init)
    ol_b, _ = scan(*flip(lat_b), st_b)
    return oc_f + jnp.flip(oc_b, 2), ol_f + jnp.flip(ol_b, 2)


def natten_latent(q, k, v, k_ctx, v_ctx, rpb):
    B, H, T, d = q.shape
    R = T // GRID_W
    wh = min(WIN_ROWS, R)
    scale = d ** -0.5
    qg = q.reshape(B, H, R, GRID_W, d)
    kg = k.reshape(B, H, R, GRID_W, d)
    vg = v.reshape(B, H, R, GRID_W, d)
    col = jnp.arange(GRID_W)
    cstart = jnp.clip(col - WIN_COLS // 2, 0, GRID_W - WIN_COLS)
    col_ok = (col[None, :] >= cstart[:, None]) & (col[None, :] < cstart[:, None] + WIN_COLS)
    dc_idx = jnp.clip(col[None, :] - col[:, None] + WIN_COLS - 1, 0, 2 * WIN_COLS - 2)
    n_loc = wh * GRID_W

    def row_block(r):
        rs = jnp.clip(r - wh // 2, 0, R - wh)
        kr = lax.dynamic_slice_in_dim(kg, rs, wh, axis=2)
        vr = lax.dynamic_slice_in_dim(vg, rs, wh, axis=2)
        qr = lax.dynamic_index_in_dim(qg, r, axis=2, keepdims=False)
        s_loc = jnp.einsum('bhqd,bhrkd->bhqrk', qr, kr) * scale
        dr_idx = rs + jnp.arange(wh) - r + WIN_ROWS - 1
        bias = rpb[:, dr_idx][:, :, dc_idx].transpose(0, 2, 1, 3)
        s_loc = jnp.where(col_ok[:, None, :], s_loc + bias, -jnp.inf).reshape(B, H, GRID_W, n_loc)
        s_ctx = jnp.einsum('bhqd,bhcd->bhqc', qr, k_ctx) * scale
        p = jax.nn.softmax(jnp.concatenate([s_loc, s_ctx], -1).astype(jnp.float32), axis=-1)
        return (jnp.einsum('bhqk,bhkd->bhqd', p[..., :n_loc], vr.reshape(B, H, n_loc, d))
                + jnp.einsum('bhqc,bhcd->bhqd', p[..., n_loc:], v_ctx))

    out = lax.map(row_block, jnp.arange(R))
    return jnp.moveaxis(out, 0, 2).reshape(B, H, T, d)


def context_attention(q, k, v):
    s = jnp.einsum('bhqd,bhkd->bhqk', q, k) * q.shape[-1] ** -0.5
    return jnp.einsum('bhqk,bhkd->bhqd', jax.nn.softmax(s.astype(jnp.float32), -1), v)


def project_stream(u, w_in, b_in, conv_w, gla_w2, gla_b2, pos):
    p = (u @ w_in + b_in).astype(jnp.float32)
    cuts = np.cumsum(SPLIT_SIZES)[:-1].tolist()
    parts = dict(zip(SPLIT_NAMES, jnp.split(p, cuts, axis=-1)))
    qk = jax.nn.silu(centred_conv(jnp.concatenate([parts["m_q"], parts["m_k"]], -1),
                                  conv_w.astype(jnp.float32)))
    mq = split_heads(qk[..., :MLSTM_W], MLSTM_HEADS)
    mk = split_heads(qk[..., MLSTM_W:], MLSTM_HEADS) * MLSTM_DH ** -0.5
    mv = split_heads(parts["m_v"], MLSTM_HEADS)
    gq = split_heads(parts["g_q"], GLA_HEADS) * GLA_DK ** -0.5
    gk = split_heads(parts["g_k"], GLA_HEADS)
    gv = split_heads(parts["g_v"], GLA_HEADS)
    if pos is not None:
        rows, cols = pos
        mq, mk = axial_rope(mq, rows, cols), axial_rope(mk, rows, cols)
        gq, gk = axial_rope(gq, rows, cols), axial_rope(gk, rows, cols)
    w2 = gla_w2.astype(jnp.float32)
    b2 = gla_b2.astype(jnp.float32)
    gla_decay = lambda lr, d: split_heads(jax.nn.log_sigmoid(lr @ w2[d] + b2[d]) / GLA_TAU, GLA_HEADS)
    tr = lambda g: jnp.swapaxes(g, 1, 2)
    return {
        "m_q": mq, "m_k": mk, "m_v": mv, "m_o": parts["m_o"],
        "m_i_f": tr(parts["m_i_f"]), "m_f_f": tr(jax.nn.log_sigmoid(parts["m_f_f"])),
        "m_i_b": tr(parts["m_i_b"]), "m_f_b": tr(jax.nn.log_sigmoid(parts["m_f_b"])),
        "g_q": gq, "g_k": gk, "g_v": gv, "g_r": parts["g_r"],
        "g_a_f": gla_decay(parts["g_lr_f"], 0), "g_a_b": gla_decay(parts["g_lr_b"], 1),
        "n_q": split_heads(parts["n_q"], NAT_HEADS), "n_k": split_heads(parts["n_k"], NAT_HEADS),
        "n_v": split_heads(parts["n_v"], NAT_HEADS),
    }


def hybrid_mixer(u_lat, u_ctx, rows, cols, w_in, b_in, conv_w, gla_w2, gla_b2,
                 mlstm_norm_g, gla_norm_g, rpb, w_out, need_ctx):
    L = project_stream(u_lat, w_in, b_in, conv_w, gla_w2, gla_b2, (rows, cols))
    C = project_stream(u_ctx, w_in, b_in, conv_w, gla_w2, gla_b2, None)
    B = u_lat.shape[0]
    f32 = jnp.float32
    m_init = (jnp.zeros((B, MLSTM_HEADS, MLSTM_DH, MLSTM_DH), f32),
              jnp.zeros((B, MLSTM_HEADS, MLSTM_DH), f32), jnp.zeros((B, MLSTM_HEADS), f32))
    mf = lambda S: (S["m_q"], S["m_k"], S["m_v"], S["m_i_f"], S["m_f_f"])
    mb = lambda S: (S["m_q"], S["m_k"], S["m_v"], S["m_i_b"], S["m_f_b"])
    m_ctx, m_lat = bidir(mlstm_scan, mf(C), mf(L), mb(C), mb(L), m_init)
    g_init = jnp.zeros((B, GLA_HEADS, GLA_DK, GLA_DV), f32)
    gf = lambda S: (S["g_q"], S["g_k"], S["g_v"], S["g_a_f"])
    gb = lambda S: (S["g_q"], S["g_k"], S["g_v"], S["g_a_b"])
    g_ctx, g_lat = bidir(gla_scan, gf(C), gf(L), gb(C), gb(L), g_init)
    rpb = rpb.astype(f32)
    n_lat = natten_latent(L["n_q"], L["n_k"], L["n_v"], C["n_k"], C["n_v"], rpb)

    def merge(S, m, g, n, dtype):
        y = jnp.concatenate([head_norm(m, mlstm_norm_g) * jax.nn.sigmoid(S["m_o"]),
                             head_norm(g, gla_norm_g) * jax.nn.silu(S["g_r"]),
                             merge_heads(n)], -1)
        return y.astype(dtype) @ w_out

    y_lat = merge(L, m_lat, g_lat, n_lat, u_lat.dtype)
    y_ctx = None
    if need_ctx:
        n_ctx = context_attention(C["n_q"], C["n_k"], C["n_v"])
        y_ctx = merge(C, m_ctx, g_ctx, n_ctx, u_ctx.dtype)
    return y_lat, y_ctx


def ec_ffn(u, w_router, w_gate, w_up, w_down):
    B, T, D = u.shape
    cap = CAPACITY_FACTOR * T // N_EXPERTS
    aff = jax.nn.softmax((u @ w_router).astype(jnp.float32), -1)
    gate, idx = lax.top_k(jnp.swapaxes(aff, 1, 2), cap)
    bidx = jnp.arange(B)[:, None, None]
    xe = u[bidx, idx]
    h = jax.nn.silu(jnp.einsum('becd,edf->becf', xe, w_gate)) * jnp.einsum('becd,edf->becf', xe, w_up)
    y = (jnp.einsum('becf,efd->becd', h, w_down) * gate[..., None]).astype(u.dtype)
    return jnp.zeros_like(u).at[bidx, idx].add(y)


def setup_inputs(seed: int = 0) -> dict:
    key = jax.random.key(seed)
    ks = jax.random.split(key, 23)
    D = D_MODEL
    nrm = lambda k, shape, s: jax.random.normal(k, shape, jnp.float32) * s
    offs = np.cumsum((0,) + SPLIT_SIZES)
    start = dict(zip(SPLIT_NAMES, offs[:-1].tolist()))
    fbias = np.zeros((PROJ_W,), np.float32)
    for name in ("m_f_f", "m_f_b"):
        fbias[start[name]:start[name] + MLSTM_HEADS] = np.linspace(3.0, 6.0, MLSTM_HEADS)
    return {
        "x": nrm(ks[0], (BATCH, SEQ, D), 1.0),
        "c": nrm(ks[1], (BATCH, D), 1.0),
        "ctx": nrm(ks[2], (BATCH, CTX_LEN, D), 1.0),
        "c_ctx": nrm(ks[3], (D,), 1.0),
        "w_mod": nrm(ks[4], (DEPTH, D, 6 * D), 0.5 * D ** -0.5),
        "b_mod": nrm(ks[5], (DEPTH, 6 * D), 0.02),
        "w_in": nrm(ks[6], (DEPTH, D, PROJ_W), D ** -0.5),
        "b_in": nrm(ks[7], (DEPTH, PROJ_W), 0.02) + jnp.asarray(fbias),
        "conv_w": nrm(ks[8], (DEPTH, CONV_K, 2 * MLSTM_W), CONV_K ** -0.5),
        "gla_w2": nrm(ks[9], (DEPTH, 2, GLA_RANK, GLA_KW), GLA_RANK ** -0.5),
        "gla_b2": nrm(ks[10], (DEPTH, 2, GLA_KW), 0.1),
        "mlstm_norm_g": 1.0 + nrm(ks[11], (DEPTH, MLSTM_W), 0.02),
        "gla_norm_g": 1.0 + nrm(ks[12], (DEPTH, GLA_W), 0.02),
        "rpb": nrm(ks[13], (DEPTH, NAT_HEADS, 2 * WIN_ROWS - 1, 2 * WIN_COLS - 1), 0.05),
        "w_out": nrm(ks[14], (DEPTH, MIX_W, D), MIX_W ** -0.5 * DEEPNORM_BETA),
        "ln1_g": 1.0 + nrm(ks[15], (DEPTH, D), 0.02),
        "ln1_b": nrm(ks[16], (DEPTH, D), 0.02),
        "w_router": nrm(ks[17], (DEPTH, D, N_EXPERTS), D ** -0.5),
        "w_gate": nrm(ks[18], (DEPTH, N_EXPERTS, D, EXPERT_FF), D ** -0.5),
        "w_up": nrm(ks[19], (DEPTH, N_EXPERTS, D, EXPERT_FF), D ** -0.5),
        "w_down": nrm(ks[20], (DEPTH, N_EXPERTS, EXPERT_FF, D), EXPERT_FF ** -0.5 * DEEPNORM_BETA),
        "ln2_g": 1.0 + nrm(ks[21], (DEPTH, D), 0.02),
        "ln2_b": nrm(ks[22], (DEPTH, D), 0.02),
    }


def reference(x, c, ctx, c_ctx, w_mod, b_mod, w_in, b_in, conv_w, gla_w2, gla_b2,
              mlstm_norm_g, gla_norm_g, rpb, w_out, ln1_g, ln1_b, w_router, w_gate, w_up,
              w_down, ln2_g, ln2_b):
    T = x.shape[1]
    t = jnp.arange(T)
    rows = (t // GRID_W).astype(jnp.float32)
    cols = (t % GRID_W).astype(jnp.float32)
    silu_c = jax.nn.silu(c)
    silu_cc = jax.nn.silu(c_ctx)
    for l in range(DEPTH):
        need_ctx = l < DEPTH - 1
        mod = (silu_c @ w_mod[l] + b_mod[l])[:, None, :]
        modc = (silu_cc @ w_mod[l] + b_mod[l])[None, None, :]
        sh1, sc1, g1, sh2, sc2, g2 = jnp.split(mod, 6, axis=-1)
        sh1c, sc1c, g1c, sh2c, sc2c, g2c = jnp.split(modc, 6, axis=-1)
        y_lat, y_ctx = hybrid_mixer(x * (1 + sc1) + sh1, ctx * (1 + sc1c) + sh1c, rows, cols,
                                    w_in[l], b_in[l], conv_w[l], gla_w2[l], gla_b2[l],
                                    mlstm_norm_g[l], gla_norm_g[l], rpb[l], w_out[l], need_ctx)
        x = layer_norm(DEEPNORM_ALPHA * x + g1 * y_lat, ln1_g[l], ln1_b[l])
        f_lat = ec_ffn(x * (1 + sc2) + sh2, w_router[l], w_gate[l], w_up[l], w_down[l])
        x = layer_norm(DEEPNORM_ALPHA * x + g2 * f_lat, ln2_g[l], ln2_b[l])
        if need_ctx:
            ctx = layer_norm(DEEPNORM_ALPHA * ctx + g1c * y_ctx, ln1_g[l], ln1_b[l])
            f_ctx = ec_ffn(ctx * (1 + sc2c) + sh2c, w_router[l], w_gate[l], w_up[l], w_down[l])
            ctx = layer_norm(DEEPNORM_ALPHA * ctx + g2c * f_ctx, ln2_g[l], ln2_b[l])
    return x
```

```python
import functools

import numpy as np
import jax
import jax.numpy as jnp
from jax import lax
from jax.experimental import pallas as pl
from jax.experimental.pallas import tpu as pltpu

F32 = jnp.float32
BF16 = jnp.bfloat16

D = 1024
DEPTH = 4
GRID_W = 64
HEAD_DIM = 64
WIN_ROWS = 8
WIN_COLS = 16
CONV_K = 3
ROPE_BASE = 10000.0
N_EXPERTS = 16
EXPERT_FF = 2 * D
CAPACITY_FACTOR = 2
LN_EPS = 1e-5
GLA_TAU = 16.0
GLA_RANK = 16
DEEPNORM_ALPHA = (2 * DEPTH) ** 0.25
NEG = -1e30

VMEM_LIMIT = 56 * 1024 * 1024
LANE = 128
CHUNK = 256

C_MQK, C_NQ, C_NK, C_NV = 0, 512, 1024, 1536
C_MV, C_MO, C_GV, C_GR = 2048, 2304, 2560, 2816
C_GQ, C_GK, C_SM = 3072, 3200, 3328
PW = 3456
_ORIG = dict(m_q=0, m_k=256, m_v=512, m_o=768, m_g=1024, g_q=1040, g_k=1168, g_v=1296, g_r=1552,
             g_lr=1808, n_q=1840, n_k=2352, n_v=2864)
_PERM = np.concatenate([
    np.arange(0, 512), np.arange(1840, 2352), np.arange(2352, 2864), np.arange(2864, 3376),
    np.arange(512, 768), np.arange(768, 1024), np.arange(1296, 1552), np.arange(1552, 1808),
    np.arange(1040, 1168), np.arange(1168, 1296), np.arange(1024, 1040), np.arange(1808, 1840)])
_NPAD = PW - _PERM.size


def _cparams(sem):
    return pltpu.CompilerParams(dimension_semantics=sem, vmem_limit_bytes=VMEM_LIMIT)


def _dot(a, b):
    return jnp.dot(a, b, preferred_element_type=F32)


def _dot_nt(a, b):
    return lax.dot_general(a, b, (((1,), (1,)), ((), ())), preferred_element_type=F32)


def _split3(x):
    hi = x.astype(BF16)
    r1 = x - hi.astype(F32)
    mid = r1.astype(BF16)
    lo = (r1 - mid.astype(F32)).astype(BF16)
    return hi, mid, lo


def _dot_exact_l(m01, x):
    hi, mid, lo = _split3(x)
    return _dot(m01, hi) + _dot(m01, mid) + _dot(m01, lo)


def _dot_exact_r(x, m01):
    hi, mid, lo = _split3(x)
    return _dot(hi, m01) + _dot(mid, m01) + _dot(lo, m01)


def _mod_kernel(c_ref, w_ref, b_ref, o_ref):
    s = jax.nn.silu(c_ref[...]).astype(BF16)
    o_ref[0] = _dot(s, w_ref[0].astype(BF16)) + b_ref[0]


def _mod_call(c_all, w_mod, b_mod):
    rows = c_all.shape[0]
    tn = 1536
    return pl.pallas_call(
        _mod_kernel,
        grid=(DEPTH, 6 * D // tn),
        in_specs=[pl.BlockSpec((rows, D), lambda l, j: (0, 0)),
                  pl.BlockSpec((1, D, tn), lambda l, j: (l, 0, j)),
                  pl.BlockSpec((1, 1, tn), lambda l, j: (l, 0, j))],
        out_specs=pl.BlockSpec((1, rows, tn), lambda l, j: (l, 0, j)),
        out_shape=jax.ShapeDtypeStruct((DEPTH, rows, 6 * D), F32),
        compiler_params=_cparams(("arbitrary", "arbitrary")),
        name="mod",
    )(c_all, w_mod, b_mod.reshape(DEPTH, 1, 6 * D))


def _inproj_kernel(x_ref, mod_ref, w_ref, b_ref, o_ref):
    m = mod_ref[0]
    u = x_ref[0] * (1.0 + m[:, D:2 * D]) + m[:, 0:D]
    o_ref[0] = _dot(u.astype(BF16), w_ref[...]) + b_ref[...]


def _inproj_call(x, mod3, w_p, b_p, shared_row):
    B, T, _ = x.shape
    tm = min(T, 512)
    if shared_row is None:
        mod_map = lambda b, i: (b, 0, 0)
    else:
        mod_map = lambda b, i: (shared_row, 0, 0)
    return pl.pallas_call(
        _inproj_kernel,
        grid=(B, T // tm),
        in_specs=[pl.BlockSpec((1, tm, D), lambda b, i: (b, i, 0)),
                  pl.BlockSpec((1, 1, 6 * D), mod_map),
                  pl.BlockSpec((D, PW), lambda b, i: (0, 0)),
                  pl.BlockSpec((1, PW), lambda b, i: (0, 0))],
        out_specs=pl.BlockSpec((1, tm, PW), lambda b, i: (b, i, 0)),
        out_shape=jax.ShapeDtypeStruct((B, T, PW), F32),
        compiler_params=_cparams(("arbitrary", "arbitrary")),
        name="inproj",
    )(x, mod3, w_p, b_p)


def _softmax_av(s_list, v_list):
    m = s_list[0].max(-1, keepdims=True)
    for s in s_list[1:]:
        m = jnp.maximum(m, s.max(-1, keepdims=True))
    acc = None
    l = None
    for s, v in zip(s_list, v_list):
        p = jnp.exp(s - m)
        ls = p.sum(-1, keepdims=True)
        o = _dot(p.astype(BF16), v)
        acc = o if acc is None else acc + o
        l = ls if l is None else l + ls
    return acc / l


def _natten_kernel(q_ref, k_ref, v_ref, qc_ref, kc_ref, vc_ref, bias_ref, o_ref, oc_ref, ks, vs, *, n_rows):
    ks[...] = k_ref[0].astype(BF16)
    vs[...] = v_ref[0].astype(BF16)
    kcb = kc_ref[0].astype(BF16)
    vcb = vc_ref[0].astype(BF16)
    lane = lax.broadcasted_iota(jnp.int32, (1, LANE), 1)
    head_mask = (lane < HEAD_DIM, lane >= HEAD_DIM)
    scale = HEAD_DIM ** -0.5

    def row_body(r, carry):
        rs = jnp.clip(r - WIN_ROWS // 2, 0, n_rows - WIN_ROWS)
        d = r - rs
        q = q_ref[0, pl.ds(pl.multiple_of(r * GRID_W, GRID_W), GRID_W), :] * scale
        koff = pl.multiple_of(rs * GRID_W, GRID_W)
        kl = ks[pl.ds(koff, WIN_ROWS * GRID_W), :]
        vl = vs[pl.ds(koff, WIN_ROWS * GRID_W), :]
        outs = []
        for h in range(2):
            qm = jnp.where(head_mask[h], q, 0.0).astype(BF16)
            s_loc = _dot_nt(qm, kl) + bias_ref[d, h]
            s_ctx = _dot_nt(qm, kcb)
            outs.append(_softmax_av([s_loc, s_ctx], [vl, vcb]))
        o_ref[0, pl.ds(pl.multiple_of(r * GRID_W, GRID_W), GRID_W), :] = jnp.where(head_mask[0], outs[0], outs[1])
        return carry

    lax.fori_loop(0, n_rows, row_body, 0)

    qc = qc_ref[0] * scale
    outs = []
    for h in range(2):
        qm = jnp.where(head_mask[h], qc, 0.0).astype(BF16)
        outs.append(_softmax_av([_dot_nt(qm, kcb)], [vcb]))
    oc_ref[0] = jnp.where(head_mask[0], outs[0], outs[1])


def _natten_call(p_lat, p_ctx, bias_l):
    B, T, _ = p_lat.shape
    Tc = p_ctx.shape[1]
    n_rows = T // GRID_W
    assert n_rows >= WIN_ROWS
    cb = lambda base: (lambda b, p: (b, 0, base // LANE + p))
    return pl.pallas_call(
        functools.partial(_natten_kernel, n_rows=n_rows),
        grid=(B, 4),
        in_specs=[pl.BlockSpec((1, T, LANE), cb(C_NQ)),
                  pl.BlockSpec((1, T, LANE), cb(C_NK)),
                  pl.BlockSpec((1, T, LANE), cb(C_NV)),
                  pl.BlockSpec((1, Tc, LANE), cb(C_NQ)),
                  pl.BlockSpec((1, Tc, LANE), cb(C_NK)),
                  pl.BlockSpec((1, Tc, LANE), cb(C_NV)),
                  pl.BlockSpec((WIN_ROWS, 2, GRID_W, WIN_ROWS * GRID_W), lambda b, p: (0, p, 0, 0))],
        out_specs=[pl.BlockSpec((1, T, LANE), lambda b, p: (b, 0, p)),
                   pl.BlockSpec((1, Tc, LANE), lambda b, p: (b, 0, p))],
        out_shape=[jax.ShapeDtypeStruct((B, T, 4 * LANE), F32),
                   jax.ShapeDtypeStruct((B, Tc, 4 * LANE), F32)],
        scratch_shapes=[pltpu.VMEM((T, LANE), BF16), pltpu.VMEM((T, LANE), BF16)],
        compiler_params=_cparams(("arbitrary", "arbitrary")),
        name="natten",
    )(p_lat, p_lat, p_lat, p_ctx, p_ctx, p_ctx, bias_l)


def _natten_bias_tables(rpb):
    col = np.arange(GRID_W)
    cstart = np.clip(col - WIN_COLS // 2, 0, GRID_W - WIN_COLS)
    col_ok = (col[None, :] >= cstart[:, None]) & (col[None, :] < cstart[:, None] + WIN_COLS)
    dc_idx = np.clip(col[None, :] - col[:, None] + WIN_COLS - 1, 0, 2 * WIN_COLS - 2)
    dr_idx = np.arange(WIN_ROWS)[None, :] - np.arange(WIN_ROWS)[:, None] + WIN_ROWS - 1
    t = rpb.astype(F32)[:, :, dr_idx]
    t = t[..., dc_idx]
    t = jnp.where(col_ok[None, None, None, None], t, NEG)
    t = t.transpose(0, 2, 1, 4, 3, 5)
    return t.reshape(DEPTH, WIN_ROWS, 8, GRID_W, WIN_ROWS * GRID_W)


def _tri_masks():
    r = lax.broadcasted_iota(jnp.int32, (CHUNK, CHUNK), 0)
    c = lax.broadcasted_iota(jnp.int32, (CHUNK, CHUNK), 1)
    return r >= c, r <= c


def _rope(x, cs, sn, first, dist):
    w = x.shape[-1]
    partner = jnp.where(first, pltpu.roll(x, w - dist, 1), pltpu.roll(x, dist, 1))
    return x * cs + partner * sn


def _rope_tables(T, n_heads, half):
    t = jnp.arange(T)
    rows = (t // GRID_W).astype(F32)
    cols = (t % GRID_W).astype(F32)
    inv = ROPE_BASE ** (-jnp.arange(half, dtype=F32) / half)
    ar = rows[:, None] * inv[None, :]
    ac = cols[:, None] * inv[None, :]
    cos = jnp.concatenate([jnp.cos(ar), jnp.cos(ar), jnp.cos(ac), jnp.cos(ac)], -1)
    sin = jnp.concatenate([-jnp.sin(ar), jnp.sin(ar), -jnp.sin(ac), jnp.sin(ac)], -1)
    return jnp.tile(cos, (1, n_heads)), jnp.tile(sin, (1, n_heads))


def _bwd_chunk(c, nc, n):
    return jnp.where(c < nc, nc - 1 - c, n - 1 - c + nc)


def _mlstm_kernel(qkl_ref, vl_ref, sml_ref, qkc_ref, vc_ref, smc_ref, cos_ref, sin_ref, cw_ref,
                  ol_ref, oc_ref,
                  q_s, ktb_s, ktf_s, va_s, bcol_s, brow_s, o_s, ck_s, *, nc, n):
    low, upp = _tri_masks()
    tri = jnp.where(low, 1.0, 0.0).astype(BF16)
    triu = jnp.where(upp, 1.0, 0.0).astype(BF16)
    lane128 = lax.broadcasted_iota(jnp.int32, (1, LANE), 1)
    lane256 = lax.broadcasted_iota(jnp.int32, (1, 2 * LANE), 1)
    lane512 = lax.broadcasted_iota(jnp.int32, (1, 4 * LANE), 1)
    rowi = lax.broadcasted_iota(jnp.int32, (CHUNK, 1), 0)
    row16 = lax.broadcasted_iota(jnp.int32, (16, 1), 0)
    first = (lane256 % 32) < 16
    hmask = [(lane256 >= HEAD_DIM * h) & (lane256 < HEAD_DIM * (h + 1)) for h in range(4)]
    cw = cw_ref[...]

    def prep(qk_ref, v_ref, sm_ref, ci, n_str, dst, use_rope):
        r0 = ci * CHUNK
        xc = qk_ref[0, r0:r0 + CHUNK, :]
        prev = qk_ref[0, r0 - 1:r0, :] if ci > 0 else jnp.zeros((1, 4 * LANE), F32)
        nxt = qk_ref[0, r0 + CHUNK:r0 + CHUNK + 1, :] if ci < n_str - 1 else jnp.zeros((1, 4 * LANE), F32)
        xp = jnp.where(rowi == 0, prev, pltpu.roll(xc, 1, 0))
        xn = jnp.where(rowi == CHUNK - 1, nxt, pltpu.roll(xc, CHUNK - 1, 0))
        y = jax.nn.silu(xp * cw[0:1] + xc * cw[1:2] + xn * cw[2:3])
        q = y[:, :2 * LANE]
        k = y[:, 2 * LANE:] * HEAD_DIM ** -0.5
        if use_rope:
            cs = cos_ref[r0:r0 + CHUNK, :]
            sn = sin_ref[r0:r0 + CHUNK, :]
            q = _rope(q, cs, sn, first, 16)
            k = _rope(k, cs, sn, first, 16)
        q_s[dst] = q.astype(BF16)
        kt = k.T
        ktf_s[dst] = kt
        ktb_s[dst] = kt.astype(BF16)
        vv = v_ref[0, r0:r0 + CHUNK, :]
        for h in range(4):
            vp = vv[:, LANE * (h // 2):LANE * (h // 2 + 1)]
            own = (lane128 < HEAD_DIM) if h % 2 == 0 else (lane128 >= HEAD_DIM)
            one_lane = HEAD_DIM if h % 2 == 0 else 0
            va_s[h, dst] = jnp.where(own, vp, jnp.where(lane128 == one_lane, 1.0, 0.0)).astype(BF16)
        g = sm_ref[0, r0:r0 + CHUNK, :]
        lf = jax.nn.log_sigmoid(g)
        pf = _dot_exact_l(tri, lf)
        sf = _dot_exact_l(triu, lf)
        bcol_s[dst] = jnp.where((lane128 >= 4) & (lane128 < 8), pf,
                                jnp.where((lane128 >= 12) & (lane128 < 16), sf, g))
        gt = g.T[0:16]
        lft = jax.nn.log_sigmoid(gt)
        pfr = _dot_exact_r(lft, triu)
        sfr = _dot_exact_r(lft, tri)
        brow_s[dst] = jnp.where((row16 >= 4) & (row16 < 8), pfr,
                                jnp.where((row16 >= 12) & (row16 < 16), sfr, gt))

    for ci in range(nc):
        prep(qkc_ref, vc_ref, smc_ref, ci, nc, ci, False)
    for ci in range(n - nc):
        prep(qkl_ref, vl_ref, sml_ref, ci, n - nc, nc + ci, True)

    def scan_dir(bwd):
        ck_s[...] = jnp.zeros_like(ck_s)
        gi = 8 if bwd else 0
        causal = upp if bwd else low

        def chunk_body(c, ms):
            ch = _bwd_chunk(c, nc, n) if bwd else c
            qc = q_s[ch]
            ktb = ktb_s[ch]
            bcol_all = bcol_s[ch]
            brow_all = brow_s[ch]
            ck_b = ck_s[...].astype(BF16)
            new_ms, houts = [], []
            for h in range(4):
                qm = jnp.where(hmask[h], qc, jnp.zeros_like(qc))
                s0 = _dot(qm, ktb)
                bcol = bcol_all[:, gi + 4 + h:gi + 5 + h]
                brow = brow_all[gi + 4 + h:gi + 5 + h, :]
                irow = brow_all[gi + h:gi + h + 1, :]
                logw = jnp.where(causal, bcol - brow + irow, NEG)
                m = ms[h]
                inter = bcol + m
                m_t = jnp.maximum(inter, logw.max(-1, keepdims=True))
                sw = (s0 * jnp.exp(logw - m_t)).astype(BF16)
                e = jnp.exp(inter - m_t)
                va = va_s[h, ch]
                nd = _dot(sw, va) + e * _dot(qm, ck_b)
                dl = HEAD_DIM if h % 2 == 0 else 0
                den = nd[:, dl:dl + 1]
                houts.append(nd / jnp.maximum(jnp.abs(den), jnp.exp(-m_t)))
                bl = brow[:, 0:1] if bwd else brow[:, CHUNK - 1:CHUNK]
                lw_end = bl - brow + irow
                m_new = jnp.maximum(bl + m, lw_end.max(-1, keepdims=True))
                w_end = jnp.exp(lw_end - m_new)
                decay = jnp.exp(bl + m - m_new)
                ktf = ktf_s[ch, HEAD_DIM * h:HEAD_DIM * (h + 1), :]
                upd = _dot((ktf * w_end).astype(BF16), va)
                rows = slice(HEAD_DIM * h, HEAD_DIM * (h + 1))
                ck_s[rows, :] = decay * ck_s[rows, :] + upd
                new_ms.append(m_new)
            for p in range(2):
                o_pair = jnp.where(lane128 < HEAD_DIM, houts[2 * p], houts[2 * p + 1])
                cols = slice(LANE * p, LANE * (p + 1))
                if bwd:
                    o_s[ch, :, cols] = o_s[ch, :, cols] + o_pair
                else:
                    o_s[ch, :, cols] = o_pair
            return tuple(new_ms)

        lax.fori_loop(0, n, chunk_body, tuple(jnp.zeros((1, 1), F32) for _ in range(4)))

    scan_dir(False)
    scan_dir(True)
    for ci in range(nc):
        oc_ref[0, ci * CHUNK:(ci + 1) * CHUNK, :] = o_s[ci]
    for ci in range(n - nc):
        ol_ref[0, ci * CHUNK:(ci + 1) * CHUNK, :] = o_s[nc + ci]


def _mlstm_call(p_lat, p_ctx, cos, sin, conv_w_l):
    B, T, _ = p_lat.shape
    Tc = p_ctx.shape[1]
    nc, n = Tc // CHUNK, (Tc + T) // CHUNK
    cb = lambda base, w: (lambda b: (b, 0, base // w))
    return pl.pallas_call(
        functools.partial(_mlstm_kernel, nc=nc, n=n),
        grid=(B,),
        in_specs=[pl.BlockSpec((1, T, 512), cb(C_MQK, 512)),
                  pl.BlockSpec((1, T, 256), cb(C_MV, 256)),
                  pl.BlockSpec((1, T, LANE), cb(C_SM, LANE)),
                  pl.BlockSpec((1, Tc, 512), cb(C_MQK, 512)),
                  pl.BlockSpec((1, Tc, 256), cb(C_MV, 256)),
                  pl.BlockSpec((1, Tc, LANE), cb(C_SM, LANE)),
                  pl.BlockSpec((T, 256), lambda b: (0, 0)),
                  pl.BlockSpec((T, 256), lambda b: (0, 0)),
                  pl.BlockSpec((CONV_K, 512), lambda b: (0, 0))],
        out_specs=[pl.BlockSpec((1, T, 256), lambda b: (b, 0, 0)),
                   pl.BlockSpec((1, Tc, 256), lambda b: (b, 0, 0))],
        out_shape=[jax.ShapeDtypeStruct((B, T, 256), F32),
                   jax.ShapeDtypeStruct((B, Tc, 256), F32)],
        scratch_shapes=[pltpu.VMEM((n, CHUNK, 256), BF16),
                        pltpu.VMEM((n, 256, CHUNK), BF16),
                        pltpu.VMEM((n, 256, CHUNK), F32),
                        pltpu.VMEM((4, n, CHUNK, LANE), BF16),
                        pltpu.VMEM((n, CHUNK, LANE), F32),
                        pltpu.VMEM((n, 16, CHUNK), F32),
                        pltpu.VMEM((n, CHUNK, 256), F32),
                        pltpu.VMEM((256, LANE), F32)],
        compiler_params=_cparams(("arbitrary",)),
        name="mlstm",
    )(p_lat, p_lat, p_lat, p_ctx, p_ctx, p_ctx, cos, sin, conv_w_l)


GLA_BLK = 64
GLA_CLAMP = 80.0


def _gla_kernel(ql_ref, kl_ref, vl_ref, sml_ref, qc_ref, kc_ref, vc_ref, smc_ref, cos_ref, sin_ref, w2_ref, b2_ref,
                ol_ref, oc_ref,
                q_s, k_s, v_s, a_s, b_s, o_s, st_s, *, nc, n):
    low, upp = _tri_masks()
    tri = jnp.where(low, 1.0, 0.0).astype(BF16)
    triu = jnp.where(upp, 1.0, 0.0).astype(BF16)
    lane128 = lax.broadcasted_iota(jnp.int32, (1, LANE), 1)
    lane256 = lax.broadcasted_iota(jnp.int32, (1, 2 * LANE), 1)
    first = (lane128 % 16) < 8
    hm128 = [(lane128 >= 32 * h) & (lane128 < 32 * (h + 1)) for h in range(4)]
    hm256 = [(lane256 >= 64 * h) & (lane256 < 64 * (h + 1)) for h in range(4)]
    nb = CHUNK // GLA_BLK
    r_st = lax.broadcasted_iota(jnp.int32, (nb * GLA_BLK, CHUNK), 0)
    c_st = lax.broadcasted_iota(jnp.int32, (nb * GLA_BLK, CHUNK), 1)
    bd_r = lax.broadcasted_iota(jnp.int32, (LANE, 2 * LANE), 0)
    bd_c = lax.broadcasted_iota(jnp.int32, (LANE, 2 * LANE), 1)
    blockdiag = (bd_r // 32) == (bd_c // 64)

    def prep(q_ref, k_ref, v_ref, sm_ref, ci, dst, use_rope):
        r0 = ci * CHUNK
        q = q_ref[0, r0:r0 + CHUNK, :] * 32 ** -0.5
        k = k_ref[0, r0:r0 + CHUNK, :]
        if use_rope:
            cs = cos_ref[r0:r0 + CHUNK, :]
            sn = sin_ref[r0:r0 + CHUNK, :]
            q = _rope(q, cs, sn, first, 8)
            k = _rope(k, cs, sn, first, 8)
        q_s[dst] = q
        k_s[dst] = k
        v_s[dst] = v_ref[0, r0:r0 + CHUNK, :].astype(BF16)
        lr = sm_ref[0, r0:r0 + CHUNK, :].astype(BF16)
        for d in range(2):
            a = jax.nn.log_sigmoid(_dot(lr, w2_ref[d]) + b2_ref[d]) / GLA_TAU
            a_s[d, dst] = a
            b_s[d, dst] = _dot_exact_l(triu if d else tri, a)

    for ci in range(nc):
        prep(qc_ref, kc_ref, vc_ref, smc_ref, ci, ci, False)
    for ci in range(n - nc):
        prep(ql_ref, kl_ref, vl_ref, sml_ref, ci, nc + ci, True)

    def scan_dir(bwd):
        d = 1 if bwd else 0
        st_s[...] = jnp.zeros_like(st_s)

        def chunk_body(c, carry):
            ch = _bwd_chunk(c, nc, n) if bwd else c
            q = q_s[ch]
            k = k_s[ch]
            vb = v_s[ch]
            a = a_s[d, ch]
            b = b_s[d, ch]
            st_b = st_s[...].astype(BF16)
            o_inter = _dot((q * jnp.exp(b)).astype(BF16), st_b)
            for i in range(nb):
                rows = slice(i * GLA_BLK, (i + 1) * GLA_BLK)
                e = (i + 1) * GLA_BLK - 1 if bwd else i * GLA_BLK
                ref = b[e:e + 1, :] - a[e:e + 1, :]
                qs = q[rows] * jnp.exp(b[rows] - ref)
                ks = (k * jnp.exp(jnp.minimum(ref - b, GLA_CLAMP))).astype(BF16)
                lhs = jnp.concatenate([jnp.where(hm128[h], qs, 0.0) for h in range(4)], 0).astype(BF16)
                att = _dot_nt(lhs, ks)
                t_idx = (r_st % GLA_BLK) + i * GLA_BLK
                ok = (c_st >= t_idx) if bwd else (c_st <= t_idx)
                att = jnp.where(ok, att, 0.0).astype(BF16)
                oh = _dot(att, vb)
                o_blk = o_inter[rows]
                for h in range(4):
                    o_blk = o_blk + jnp.where(hm256[h], oh[h * GLA_BLK:(h + 1) * GLA_BLK], 0.0)
                if bwd:
                    o_s[ch, rows, :] = o_s[ch, rows, :] + o_blk
                else:
                    o_s[ch, rows, :] = o_blk
            bt = b.T
            tot = bt[:, 0:1] if bwd else bt[:, CHUNK - 1:CHUNK]
            kt = (k.T * jnp.exp(tot - bt)).astype(BF16)
            upd = jnp.where(blockdiag, _dot(kt, vb), 0.0)
            st_s[...] = jnp.exp(tot) * st_s[...] + upd
            return carry

        lax.fori_loop(0, n, chunk_body, 0)

    scan_dir(False)
    scan_dir(True)
    for ci in range(nc):
        oc_ref[0, ci * CHUNK:(ci + 1) * CHUNK, :] = o_s[ci]
    for ci in range(n - nc):
        ol_ref[0, ci * CHUNK:(ci + 1) * CHUNK, :] = o_s[nc + ci]


def _gla_call(p_lat, p_ctx, cos, sin, w2p, b2p):
    B, T, _ = p_lat.shape
    Tc = p_ctx.shape[1]
    nc, n = Tc // CHUNK, (Tc + T) // CHUNK
    cb = lambda base, w: (lambda b: (b, 0, base // w))
    return pl.pallas_call(
        functools.partial(_gla_kernel, nc=nc, n=n),
        grid=(B,),
        in_specs=[pl.BlockSpec((1, T, LANE), cb(C_GQ, LANE)),
                  pl.BlockSpec((1, T, LANE), cb(C_GK, LANE)),
                  pl.BlockSpec((1, T, 256), cb(C_GV, 256)),
                  pl.BlockSpec((1, T, LANE), cb(C_SM, LANE)),
                  pl.BlockSpec((1, Tc, LANE), cb(C_GQ, LANE)),
                  pl.BlockSpec((1, Tc, LANE), cb(C_GK, LANE)),
                  pl.BlockSpec((1, Tc, 256), cb(C_GV, 256)),
                  pl.BlockSpec((1, Tc, LANE), cb(C_SM, LANE)),
                  pl.BlockSpec((T, LANE), lambda b: (0, 0)),
                  pl.BlockSpec((T, LANE), lambda b: (0, 0)),
                  pl.BlockSpec((2, LANE, LANE), lambda b: (0, 0, 0)),
                  pl.BlockSpec((2, 1, LANE), lambda b: (0, 0, 0))],
        out_specs=[pl.BlockSpec((1, T, 256), lambda b: (b, 0, 0)),
                   pl.BlockSpec((1, Tc, 256), lambda b: (b, 0, 0))],
        out_shape=[jax.ShapeDtypeStruct((B, T, 256), F32),
                   jax.ShapeDtypeStruct((B, Tc, 256), F32)],
        scratch_shapes=[pltpu.VMEM((n, CHUNK, LANE), F32),
                        pltpu.VMEM((n, CHUNK, LANE), F32),
                        pltpu.VMEM((n, CHUNK, 256), BF16),
                        pltpu.VMEM((2, n, CHUNK, LANE), F32),
                        pltpu.VMEM((2, n, CHUNK, LANE), F32),
                        pltpu.VMEM((n, CHUNK, 256), F32),
                        pltpu.VMEM((LANE, 256), F32)],
        compiler_params=_cparams(("arbitrary",)),
        name="gla",
    )(p_lat, p_lat, p_lat, p_lat, p_ctx, p_ctx, p_ctx, p_ctx, cos, sin, w2p, b2p)


def _gla_gate_weights(gla_w2, gla_b2):
    w = jnp.zeros((DEPTH, 2, LANE, LANE), F32)
    w = w.at[:, 0, 16:32].set(gla_w2[:, 0]).at[:, 1, 32:48].set(gla_w2[:, 1])
    return w.astype(BF16), gla_b2.astype(F32).reshape(DEPTH, 2, 1, LANE)


def _layer_norm(z, g, b):
    mu = z.mean(-1, keepdims=True)
    zc = z - mu
    var = jnp.mean(jnp.square(zc), -1, keepdims=True)
    return zc * lax.rsqrt(var + LN_EPS) * g + b


def _merge_kernel(hm_ref, hg_ref, hn_ref, mo_ref, gr_ref, x_ref, mod_ref, wout_ref, ng_ref, ln_ref, x1_ref, u2_ref):
    r = lax.broadcasted_iota(jnp.int32, (256, 256), 0)
    c = lax.broadcasted_iota(jnp.int32, (256, 256), 1)
    avg = jnp.where((r // HEAD_DIM) == (c // HEAD_DIM), 1.0 / HEAD_DIM, 0.0).astype(BF16)

    def head_norm(h):
        d = h - _dot_exact_r(h, avg)
        var = _dot_exact_r(d * d, avg)
        return d * lax.rsqrt(var + LN_EPS)

    ym = head_norm(hm_ref[0]) * ng_ref[0:1, :] * jax.nn.sigmoid(mo_ref[0])
    yg = head_norm(hg_ref[0]) * ng_ref[1:2, :] * jax.nn.silu(gr_ref[0])
    y = (_dot(ym.astype(BF16), wout_ref[0:256, :]) + _dot(yg.astype(BF16), wout_ref[256:512, :])
         + _dot(hn_ref[0].astype(BF16), wout_ref[512:1024, :]))
    m = mod_ref[0]
    x1 = _layer_norm(DEEPNORM_ALPHA * x_ref[0] + m[:, 2 * D:3 * D] * y, ln_ref[0:1, :], ln_ref[1:2, :])
    x1_ref[0] = x1
    u2_ref[0] = x1 * (1.0 + m[:, 4 * D:5 * D]) + m[:, 3 * D:4 * D]


def _merge_call(hm, hg, hn, p, x, mod3, wout_b, ng, ln, shared_row):
    B, T, _ = x.shape
    tm = 256
    mod_map = (lambda b, i: (b, 0, 0)) if shared_row is None else (lambda b, i: (shared_row, 0, 0))
    tok = lambda w: pl.BlockSpec((1, tm, w), lambda b, i: (b, i, 0))
    return pl.pallas_call(
        _merge_kernel,
        grid=(B, T // tm),
        in_specs=[tok(256), tok(256), tok(512),
                  pl.BlockSpec((1, tm, 256), lambda b, i: (b, i, C_MO // 256)),
                  pl.BlockSpec((1, tm, 256), lambda b, i: (b, i, C_GR // 256)),
                  tok(D),
                  pl.BlockSpec((1, 1, 6 * D), mod_map),
                  pl.BlockSpec((D, D), lambda b, i: (0, 0)),
                  pl.BlockSpec((2, 256), lambda b, i: (0, 0)),
                  pl.BlockSpec((2, D), lambda b, i: (0, 0))],
        out_specs=[tok(D), tok(D)],
        out_shape=[jax.ShapeDtypeStruct((B, T, D), F32), jax.ShapeDtypeStruct((B, T, D), F32)],
        compiler_params=_cparams(("arbitrary", "arbitrary")),
        name="merge",
    )(hm, hg, hn, p, p, x, mod3, wout_b, ng, ln)


def _router_kernel(u_ref, wr_ref, o_ref, aff_s, sp_s, *, T, cap):
    J = T // LANE
    E = N_EXPERTS
    wrt = wr_ref[...]
    keys = []
    for j in range(J):
        lg = _dot_nt(wrt, u_ref[0, j * LANE:(j + 1) * LANE, :].astype(BF16))
        ex = jnp.exp(lg - lg.max(0, keepdims=True))
        aff = ex / ex.sum(0, keepdims=True)
        aff_s[j * E:(j + 1) * E, :] = aff
        keys.append(pltpu.bitcast(aff, jnp.int32))

    thr = jnp.zeros((E, 1), jnp.int32)
    for bit in range(30, -1, -1):
        cand = thr | (1 << bit)
        cnt = None
        for j in range(J):
            cj = jnp.where(keys[j] >= cand, 1.0, 0.0)
            cnt = cj if cnt is None else cnt + cj
        thr = jnp.where(cnt.sum(-1, keepdims=True) >= cap, cand, thr)

    cnt = None
    for j in range(J):
        cj = jnp.where(keys[j] > thr, 1.0, 0.0)
        cnt = cj if cnt is None else cnt + cj
    need = cap - cnt.sum(-1, keepdims=True)

    r = lax.broadcasted_iota(jnp.int32, (LANE, LANE), 0)
    c = lax.broadcasted_iota(jnp.int32, (LANE, LANE), 1)
    upper = jnp.where(r <= c, 1.0, 0.0).astype(BF16)
    ones = jnp.ones((LANE, LANE), BF16)
    rr = lax.broadcasted_iota(jnp.int32, (J * E, J * E), 0)
    cc = lax.broadcasted_iota(jnp.int32, (J * E, J * E), 1)
    earlier = jnp.where(((rr % E) == (cc % E)) & ((cc // E) < (rr // E)), 1.0, 0.0).astype(BF16)

    def prefix(x01):
        xb = x01.astype(BF16)
        return _dot(xb, upper) + _dot(earlier, _dot(xb, ones).astype(BF16))

    gt = jnp.concatenate([jnp.where(k > thr, 1.0, 0.0) for k in keys], 0)
    eq = jnp.concatenate([jnp.where(k == thr, 1.0, 0.0) for k in keys], 0)
    need_t = jnp.concatenate([need] * J, 0)
    sel = jnp.maximum(gt, jnp.where(prefix(eq) <= need_t, eq, 0.0))
    sp_s[...] = jnp.where(sel > 0.0, prefix(sel) - 1.0, -1.0)

    sb = min(cap, LANE)
    lane = lax.broadcasted_iota(jnp.int32, (1, LANE), 1)
    lane_f = lane.astype(F32)
    o_ref[...] = jnp.zeros_like(o_ref)
    for e in range(E):
        for half in range(cap // sb):
            slot = (lax.broadcasted_iota(jnp.int32, (sb, LANE), 0) + half * sb).astype(F32)

            def jbody(j, acc):
                acc_i, acc_g = acc
                sp = sp_s[pl.ds(j * E + e, 1), :]
                af = aff_s[pl.ds(j * E + e, 1), :]
                hit = sp == slot
                tid = lane_f + (j * LANE).astype(F32)
                return jnp.where(hit, tid, acc_i), jnp.where(hit, af, acc_g)

            acc_i, acc_g = lax.fori_loop(0, J, jbody, (jnp.zeros((sb, LANE), F32), jnp.zeros((sb, LANE), F32)))
            icol = acc_i.sum(-1, keepdims=True)
            gcol = acc_g.sum(-1, keepdims=True)
            rows = slice(half * sb, (half + 1) * sb)
            o_ref[0, rows, :] = jnp.where(lane == e, icol, jnp.where(lane == E + e, gcol, o_ref[0, rows, :]))


def _router_call(u2, wrt_b):
    B, T, _ = u2.shape
    cap = CAPACITY_FACTOR * T // N_EXPERTS
    J = T // LANE
    out = pl.pallas_call(
        functools.partial(_router_kernel, T=T, cap=cap),
        grid=(B,),
        in_specs=[pl.BlockSpec((1, T, D), lambda b: (b, 0, 0)),
                  pl.BlockSpec((N_EXPERTS, D), lambda b: (0, 0))],
        out_specs=pl.BlockSpec((1, cap, LANE), lambda b: (b, 0, 0)),
        out_shape=jax.ShapeDtypeStruct((B, cap, LANE), F32),
        scratch_shapes=[pltpu.VMEM((J * N_EXPERTS, LANE), F32), pltpu.VMEM((J * N_EXPERTS, LANE), F32)],
        compiler_params=_cparams(("arbitrary",)),
        name="router",
    )(u2, wrt_b)
    idx = out[:, :, :N_EXPERTS].astype(jnp.int32).transpose(0, 2, 1).reshape(B, 1, N_EXPERTS * cap)
    gate = out[:, :, N_EXPERTS:2 * N_EXPERTS].transpose(0, 2, 1).reshape(B, 1, N_EXPERTS * cap)
    return idx, gate


def _gather_kernel(idx_ref, u_ref, o_ref, *, cap, eg):
    g = pl.program_id(1)
    for e in range(eg):
        def body(s, carry):
            r = idx_ref[0, 0, (g * eg + e) * cap + s]
            o_ref[e, 0, pl.ds(s, 1), :] = u_ref[0, pl.ds(r, 1), :]
            return carry
        lax.fori_loop(0, cap, body, 0, unroll=8)


def _gather_call(idx, u2):
    B, T, _ = u2.shape
    cap = CAPACITY_FACTOR * T // N_EXPERTS
    eg = 4
    smem = lambda: pl.BlockSpec((1, 1, N_EXPERTS * cap), lambda b, g: (b, 0, 0), memory_space=pltpu.SMEM)
    return pl.pallas_call(
        functools.partial(_gather_kernel, cap=cap, eg=eg),
        grid=(B, N_EXPERTS // eg),
        in_specs=[smem(), pl.BlockSpec((1, T, D), lambda b, g: (b, 0, 0))],
        out_specs=pl.BlockSpec((eg, 1, cap, D), lambda b, g: (g, b, 0, 0)),
        out_shape=jax.ShapeDtypeStruct((N_EXPERTS, B, cap, D), F32),
        compiler_params=_cparams(("arbitrary", "arbitrary")),
        name="gather",
    )(idx, u2)


FF_CHUNK = 512


def _ffn_kernel(x_ref, wg_ref, wu_ref, wd_ref, o_ref):
    xb = x_ref[0].astype(BF16)
    acc = None
    for c in range(EXPERT_FF // FF_CHUNK):
        cols = slice(c * FF_CHUNK, (c + 1) * FF_CHUNK)
        h = (jax.nn.silu(_dot(xb, wg_ref[0, :, cols])) * _dot(xb, wu_ref[0, :, cols])).astype(BF16)
        t = _dot(h, wd_ref[0, cols, :])
        acc = t if acc is None else acc + t
    o_ref[0] = acc


def _ffn_call(xe, wg_b, wu_b, wd_b):
    E, M, _ = xe.shape
    tm = min(M, 512)
    return pl.pallas_call(
        _ffn_kernel,
        grid=(E, M // tm),
        in_specs=[pl.BlockSpec((1, tm, D), lambda e, i: (e, i, 0)),
                  pl.BlockSpec((1, D, EXPERT_FF), lambda e, i: (e, 0, 0)),
                  pl.BlockSpec((1, D, EXPERT_FF), lambda e, i: (e, 0, 0)),
                  pl.BlockSpec((1, EXPERT_FF, D), lambda e, i: (e, 0, 0))],
        out_specs=pl.BlockSpec((1, tm, D), lambda e, i: (e, i, 0)),
        out_shape=jax.ShapeDtypeStruct((E, M, D), F32),
        compiler_params=_cparams(("arbitrary", "arbitrary")),
        name="ffn",
    )(xe, wg_b, wu_b, wd_b)


def _scatter_kernel(idx_ref, gate_ref, y_ref, x1_ref, mod_ref, ln_ref, o_ref, *, cap, eg, n_g):
    g = pl.program_id(1)

    @pl.when(g == 0)
    def _():
        o_ref[...] = jnp.zeros_like(o_ref)

    for e in range(eg):
        def body(s, carry):
            p = (g * eg + e) * cap + s
            r = idx_ref[0, 0, p]
            o_ref[0, pl.ds(r, 1), :] = o_ref[0, pl.ds(r, 1), :] + y_ref[e, 0, pl.ds(s, 1), :] * gate_ref[0, 0, p]
            return carry
        lax.fori_loop(0, cap, body, 0, unroll=4)

    @pl.when(g == n_g - 1)
    def _():
        z = DEEPNORM_ALPHA * x1_ref[0] + mod_ref[0][:, 5 * D:6 * D] * o_ref[0]
        o_ref[0] = _layer_norm(z, ln_ref[0:1, :], ln_ref[1:2, :])


def _scatter_call(idx, gate, y4, x1, mod3, ln, shared_row):
    B, T, _ = x1.shape
    cap = CAPACITY_FACTOR * T // N_EXPERTS
    eg = 2
    n_g = N_EXPERTS // eg
    mod_map = (lambda b, g: (b, 0, 0)) if shared_row is None else (lambda b, g: (shared_row, 0, 0))
    smem = lambda: pl.BlockSpec((1, 1, N_EXPERTS * cap), lambda b, g: (b, 0, 0), memory_space=pltpu.SMEM)
    return pl.pallas_call(
        functools.partial(_scatter_kernel, cap=cap, eg=eg, n_g=n_g),
        grid=(B, n_g),
        in_specs=[smem(), smem(),
                  pl.BlockSpec((eg, 1, cap, D), lambda b, g: (g, b, 0, 0)),
                  pl.BlockSpec((1, T, D), lambda b, g: (b, 0, 0)),
                  pl.BlockSpec((1, 1, 6 * D), mod_map),
                  pl.BlockSpec((2, D), lambda b, g: (0, 0))],
        out_specs=pl.BlockSpec((1, T, D), lambda b, g: (b, 0, 0)),
        out_shape=jax.ShapeDtypeStruct((B, T, D), F32),
        compiler_params=_cparams(("arbitrary", "arbitrary")),
        name="scatter",
    )(idx, gate, y4, x1, mod3, ln)


def _moe(x1, u2, mod3, wrt_b, wg_b, wu_b, wd_b, ln2, shared_row):
    B, T, _ = x1.shape
    cap = CAPACITY_FACTOR * T // N_EXPERTS
    idx, gate = _router_call(u2, wrt_b)
    xe = _gather_call(idx, u2)
    y = _ffn_call(xe.reshape(N_EXPERTS, B * cap, D), wg_b, wu_b, wd_b)
    return _scatter_call(idx, gate, y.reshape(N_EXPERTS, B, cap, D), x1, mod3, ln2, shared_row)


def kernel(x, c, ctx, c_ctx, w_mod, b_mod, w_in, b_in, conv_w, gla_w2, gla_b2, mlstm_norm_g, gla_norm_g, rpb, w_out, ln1_g, ln1_b, w_router, w_gate, w_up, w_down, ln2_g, ln2_b):
    B, T, _ = x.shape
    n_mod = -(-(B + 1) // 8) * 8
    c_all = jnp.concatenate([c, c_ctx[None], jnp.zeros((n_mod - B - 1, D), F32)], 0)
    mods = _mod_call(c_all, w_mod, b_mod)
    w_p = jnp.concatenate([w_in[..., _PERM], jnp.zeros((DEPTH, D, _NPAD), F32)], -1).astype(BF16)
    b_p = jnp.concatenate([b_in[..., _PERM], jnp.zeros((DEPTH, _NPAD), F32)], -1)
    bias = _natten_bias_tables(rpb)
    mcos, msin = _rope_tables(T, 4, 16)
    gcos, gsin = _rope_tables(T, 4, 8)
    w2p, b2p = _gla_gate_weights(gla_w2, gla_b2)
    wout_b = w_out.astype(BF16)
    wrt_b = jnp.swapaxes(w_router, 1, 2).astype(BF16)
    wg_b, wu_b, wd_b = w_gate.astype(BF16), w_up.astype(BF16), w_down.astype(BF16)
    ng = jnp.stack([mlstm_norm_g, gla_norm_g], 1)
    ln1 = jnp.stack([ln1_g, ln1_b], 1)
    ln2 = jnp.stack([ln2_g, ln2_b], 1)
    for l in range(DEPTH):
        mod3 = mods[l].reshape(n_mod, 1, 6 * D)
        p_lat = _inproj_call(x, mod3, w_p[l], b_p[l][None], None)
        p_ctx = _inproj_call(ctx, mod3, w_p[l], b_p[l][None], B)
        n_lat, n_ctx = _natten_call(p_lat, p_ctx, bias[l])
        m_lat, m_ctx = _mlstm_call(p_lat, p_ctx, mcos, msin, conv_w[l])
        g_lat, g_ctx = _gla_call(p_lat, p_ctx, gcos, gsin, w2p[l], b2p[l])
        x1, u2 = _merge_call(m_lat, g_lat, n_lat, p_lat, x, mod3, wout_b[l], ng[l], ln1[l], None)
        x = _moe(x1, u2, mod3, wrt_b[l], wg_b[l], wu_b[l], wd_b[l], ln2[l], None)
        if l < DEPTH - 1:
            c1, u2c = _merge_call(m_ctx, g_ctx, n_ctx, p_ctx, ctx, mod3, wout_b[l], ng[l], ln1[l], B)
            ctx = _moe(c1, u2c, mod3, wrt_b[l], wg_b[l], wu_b[l], wd_b[l], ln2[l], B)
    return x
```

```python
import functools

import numpy as np
import jax
import jax.numpy as jnp
from jax import lax
from jax.experimental import pallas as pl
from jax.experimental.pallas import tpu as pltpu

F32 = jnp.float32
BF16 = jnp.bfloat16

D = 1024
DEPTH = 4
GRID_W = 64
HEAD_DIM = 64
WIN_ROWS = 8
WIN_COLS = 16
CONV_K = 3
ROPE_BASE = 10000.0
N_EXPERTS = 16
EXPERT_FF = 2 * D
CAPACITY_FACTOR = 2
LN_EPS = 1e-5
GLA_TAU = 16.0
GLA_RANK = 16
DEEPNORM_ALPHA = (2 * DEPTH) ** 0.25
NEG = -1e30

VMEM_LIMIT = 56 * 1024 * 1024
LANE = 128
CHUNK = 256

C_MQK, C_NQ, C_NK, C_NV = 0, 512, 1024, 1536
C_MV, C_MO, C_GV, C_GR = 2048, 2304, 2560, 2816
C_GQ, C_GK, C_SM = 3072, 3200, 3328
PW = 3456
_ORIG = dict(m_q=0, m_k=256, m_v=512, m_o=768, m_g=1024, g_q=1040, g_k=1168, g_v=1296, g_r=1552,
             g_lr=1808, n_q=1840, n_k=2352, n_v=2864)
_PERM = np.concatenate([
    np.arange(0, 512), np.arange(1840, 2352), np.arange(2352, 2864), np.arange(2864, 3376),
    np.arange(512, 768), np.arange(768, 1024), np.arange(1296, 1552), np.arange(1552, 1808),
    np.arange(1040, 1168), np.arange(1168, 1296), np.arange(1024, 1040), np.arange(1808, 1840)])
_NPAD = PW - _PERM.size


def _cparams(sem):
    return pltpu.CompilerParams(dimension_semantics=sem, vmem_limit_bytes=VMEM_LIMIT)


def _dot(a, b):
    return jnp.dot(a, b, preferred_element_type=F32)


def _dot_nt(a, b):
    return lax.dot_general(a, b, (((1,), (1,)), ((), ())), preferred_element_type=F32)


def _split3(x):
    hi = x.astype(BF16)
    r1 = x - hi.astype(F32)
    mid = r1.astype(BF16)
    lo = (r1 - mid.astype(F32)).astype(BF16)
    return hi, mid, lo


def _dot_exact_l(m01, x):
    hi, mid, lo = _split3(x)
    return _dot(m01, hi) + _dot(m01, mid) + _dot(m01, lo)


def _dot_exact_r(x, m01):
    hi, mid, lo = _split3(x)
    return _dot(hi, m01) + _dot(mid, m01) + _dot(lo, m01)


def _mod_kernel(c_ref, w_ref, b_ref, o_ref):
    s = jax.nn.silu(c_ref[...]).astype(BF16)
    o_ref[0] = _dot(s, w_ref[0].astype(BF16)) + b_ref[0]


def _mod_call(c_all, w_mod, b_mod):
    rows = c_all.shape[0]
    tn = 1536
    return pl.pallas_call(
        _mod_kernel,
        grid=(DEPTH, 6 * D // tn),
        in_specs=[pl.BlockSpec((rows, D), lambda l, j: (0, 0)),
                  pl.BlockSpec((1, D, tn), lambda l, j: (l, 0, j)),
                  pl.BlockSpec((1, 1, tn), lambda l, j: (l, 0, j))],
        out_specs=pl.BlockSpec((1, rows, tn), lambda l, j: (l, 0, j)),
        out_shape=jax.ShapeDtypeStruct((DEPTH, rows, 6 * D), F32),
        compiler_params=_cparams(("arbitrary", "arbitrary")),
        name="mod",
    )(c_all, w_mod, b_mod.reshape(DEPTH, 1, 6 * D))


def _inproj_kernel(x_ref, mod_ref, w_ref, b_ref, o_ref):
    m = mod_ref[0]
    u = x_ref[0] * (1.0 + m[:, D:2 * D]) + m[:, 0:D]
    o_ref[0] = _dot(u.astype(BF16), w_ref[...]) + b_ref[...]


def _inproj_call(x, mod3, w_p, b_p, shared_row):
    B, T, _ = x.shape
    tm = min(T, 512)
    if shared_row is None:
        mod_map = lambda b, i: (b, 0, 0)
    else:
        mod_map = lambda b, i: (shared_row, 0, 0)
    return pl.pallas_call(
        _inproj_kernel,
        grid=(B, T // tm),
        in_specs=[pl.BlockSpec((1, tm, D), lambda b, i: (b, i, 0)),
                  pl.BlockSpec((1, 1, 6 * D), mod_map),
                  pl.BlockSpec((D, PW), lambda b, i: (0, 0)),
                  pl.BlockSpec((1, PW), lambda b, i: (0, 0))],
        out_specs=pl.BlockSpec((1, tm, PW), lambda b, i: (b, i, 0)),
        out_shape=jax.ShapeDtypeStruct((B, T, PW), F32),
        compiler_params=_cparams(("arbitrary", "arbitrary")),
        name="inproj",
    )(x, mod3, w_p, b_p)


def _softmax_av(s_list, v_list):
    m = s_list[0].max(-1, keepdims=True)
    for s in s_list[1:]:
        m = jnp.maximum(m, s.max(-1, keepdims=True))
    acc = None
    l = None
    for s, v in zip(s_list, v_list):
        p = jnp.exp(s - m)
        ls = p.sum(-1, keepdims=True)
        o = _dot(p.astype(BF16), v)
        acc = o if acc is None else acc + o
        l = ls if l is None else l + ls
    return acc / l


NAT_GROUP = 4


def _natten_kernel(q_ref, k_ref, v_ref, qc_ref, kc_ref, vc_ref, bias_ref, o_ref, oc_ref, ks, vs, *, n_rows):
    ks[...] = k_ref[0].astype(BF16)
    vs[...] = v_ref[0].astype(BF16)
    kcb = kc_ref[0].astype(BF16)
    vcb = vc_ref[0].astype(BF16)
    lane = lax.broadcasted_iota(jnp.int32, (1, LANE), 1)
    head0 = lane < HEAD_DIM
    scale = HEAD_DIM ** -0.5

    def stack_heads(q):
        return jnp.concatenate([jnp.where(head0, q, 0.0), jnp.where(head0, 0.0, q)], 0).astype(BF16)

    def unstack(o, n):
        return jnp.where(head0, o[:n], o[n:])

    def rows_body(g, carry):
        koffs, scores = [], []
        for i in range(NAT_GROUP):
            r = g * NAT_GROUP + i
            rs = jnp.clip(r - WIN_ROWS // 2, 0, n_rows - WIN_ROWS)
            qs = stack_heads(q_ref[0, pl.ds(pl.multiple_of(r * GRID_W, GRID_W), GRID_W), :] * scale)
            koff = pl.multiple_of(rs * GRID_W, GRID_W)
            kl = ks[pl.ds(koff, WIN_ROWS * GRID_W), :]
            koffs.append(koff)
            scores.append([_dot_nt(qs, kl) + bias_ref[r - rs, 0], _dot_nt(qs, kcb)])
        for i in range(NAT_GROUP):
            r = g * NAT_GROUP + i
            vl = vs[pl.ds(koffs[i], WIN_ROWS * GRID_W), :]
            o = _softmax_av(scores[i], [vl, vcb])
            o_ref[0, pl.ds(pl.multiple_of(r * GRID_W, GRID_W), GRID_W), :] = unstack(o, GRID_W)
        return carry

    lax.fori_loop(0, n_rows // NAT_GROUP, rows_body, 0)

    tc = qc_ref.shape[1]
    oc = _softmax_av([_dot_nt(stack_heads(qc_ref[0] * scale), kcb)], [vcb])
    oc_ref[0] = unstack(oc, tc)


def _natten_call(p_lat, p_ctx, bias_l):
    B, T, _ = p_lat.shape
    Tc = p_ctx.shape[1]
    n_rows = T // GRID_W
    assert n_rows >= WIN_ROWS
    cb = lambda base: (lambda b, p: (b, 0, base // LANE + p))
    return pl.pallas_call(
        functools.partial(_natten_kernel, n_rows=n_rows),
        grid=(B, 4),
        in_specs=[pl.BlockSpec((1, T, LANE), cb(C_NQ)),
                  pl.BlockSpec((1, T, LANE), cb(C_NK)),
                  pl.BlockSpec((1, T, LANE), cb(C_NV)),
                  pl.BlockSpec((1, Tc, LANE), cb(C_NQ)),
                  pl.BlockSpec((1, Tc, LANE), cb(C_NK)),
                  pl.BlockSpec((1, Tc, LANE), cb(C_NV)),
                  pl.BlockSpec((WIN_ROWS, 1, 2 * GRID_W, WIN_ROWS * GRID_W), lambda b, p: (0, p, 0, 0))],
        out_specs=[pl.BlockSpec((1, T, LANE), lambda b, p: (b, 0, p)),
                   pl.BlockSpec((1, Tc, LANE), lambda b, p: (b, 0, p))],
        out_shape=[jax.ShapeDtypeStruct((B, T, 4 * LANE), F32),
                   jax.ShapeDtypeStruct((B, Tc, 4 * LANE), F32)],
        scratch_shapes=[pltpu.VMEM((T, LANE), BF16), pltpu.VMEM((T, LANE), BF16)],
        compiler_params=_cparams(("arbitrary", "arbitrary")),
        name="natten",
    )(p_lat, p_lat, p_lat, p_ctx, p_ctx, p_ctx, bias_l)


def _natten_bias_tables(rpb):
    col = np.arange(GRID_W)
    cstart = np.clip(col - WIN_COLS // 2, 0, GRID_W - WIN_COLS)
    col_ok = (col[None, :] >= cstart[:, None]) & (col[None, :] < cstart[:, None] + WIN_COLS)
    dc_idx = np.clip(col[None, :] - col[:, None] + WIN_COLS - 1, 0, 2 * WIN_COLS - 2)
    dr_idx = np.arange(WIN_ROWS)[None, :] - np.arange(WIN_ROWS)[:, None] + WIN_ROWS - 1
    t = rpb.astype(F32)[:, :, dr_idx]
    t = t[..., dc_idx]
    t = jnp.where(col_ok[None, None, None, None], t, NEG)
    t = t.transpose(0, 2, 1, 4, 3, 5)
    return t.reshape(DEPTH, WIN_ROWS, 4, 2 * GRID_W, WIN_ROWS * GRID_W)


def _tri_masks():
    r = lax.broadcasted_iota(jnp.int32, (CHUNK, CHUNK), 0)
    c = lax.broadcasted_iota(jnp.int32, (CHUNK, CHUNK), 1)
    return r >= c, r <= c


def _rope(x, cs, sn, first, dist):
    w = x.shape[-1]
    partner = jnp.where(first, pltpu.roll(x, w - dist, 1), pltpu.roll(x, dist, 1))
    return x * cs + partner * sn


def _rope_tables(T, n_heads, half):
    t = jnp.arange(T)
    rows = (t // GRID_W).astype(F32)
    cols = (t % GRID_W).astype(F32)
    inv = ROPE_BASE ** (-jnp.arange(half, dtype=F32) / half)
    ar = rows[:, None] * inv[None, :]
    ac = cols[:, None] * inv[None, :]
    cos = jnp.concatenate([jnp.cos(ar), jnp.cos(ar), jnp.cos(ac), jnp.cos(ac)], -1)
    sin = jnp.concatenate([-jnp.sin(ar), jnp.sin(ar), -jnp.sin(ac), jnp.sin(ac)], -1)
    return jnp.tile(cos, (1, n_heads)), jnp.tile(sin, (1, n_heads))


def _bwd_chunk(c, nc, n):
    return jnp.where(c < nc, nc - 1 - c, n - 1 - c + nc)


def _mlstm_kernel(qkl_ref, vl_ref, sml_ref, qkc_ref, vc_ref, smc_ref, cos_ref, sin_ref, cw_ref,
                  ol_ref, oc_ref,
                  qt_s, k_s, vat_s, gcol_s, grow_s, brow_s, cm_s, ot_s, ck_s, *, nc, n):
    low, upp = _tri_masks()
    tri = jnp.where(low, 1.0, 0.0).astype(BF16)
    triu = jnp.where(upp, 1.0, 0.0).astype(BF16)
    lane128 = lax.broadcasted_iota(jnp.int32, (1, LANE), 1)
    lane256 = lax.broadcasted_iota(jnp.int32, (1, 2 * LANE), 1)
    rowi = lax.broadcasted_iota(jnp.int32, (CHUNK, 1), 0)
    row16 = lax.broadcasted_iota(jnp.int32, (16, 1), 0)
    first = (lane256 % 32) < 16
    hmask = [(lane256 >= HEAD_DIM * h) & (lane256 < HEAD_DIM * (h + 1)) for h in range(4)]
    hrows = [(rowi >= HEAD_DIM * h) & (rowi < HEAD_DIM * (h + 1)) for h in range(4)]
    ones_blk = jnp.where(lax.broadcasted_iota(jnp.int32, (HEAD_DIM, CHUNK), 0) == 0, 1.0, 0.0)
    fwd_rows = row16 < 8
    cw = cw_ref[...]

    def cummax_lanes(x, suffix):
        sh = 1
        while sh < CHUNK:
            if suffix:
                moved = jnp.where(lane256 < CHUNK - sh, pltpu.roll(x, CHUNK - sh, 1), NEG)
            else:
                moved = jnp.where(lane256 >= sh, pltpu.roll(x, sh, 1), NEG)
            x = jnp.maximum(x, moved)
            sh *= 2
        return x

    def prep(qk_ref, v_ref, sm_ref, ci, n_str, dst, use_rope):
        r0 = ci * CHUNK
        xc = qk_ref[0, r0:r0 + CHUNK, :]
        prev = qk_ref[0, r0 - 1:r0, :] if ci > 0 else jnp.zeros((1, 4 * LANE), F32)
        nxt = qk_ref[0, r0 + CHUNK:r0 + CHUNK + 1, :] if ci < n_str - 1 else jnp.zeros((1, 4 * LANE), F32)
        xp = jnp.where(rowi == 0, prev, pltpu.roll(xc, 1, 0))
        xn = jnp.where(rowi == CHUNK - 1, nxt, pltpu.roll(xc, CHUNK - 1, 0))
        y = jax.nn.silu(xp * cw[0:1] + xc * cw[1:2] + xn * cw[2:3])
        q = y[:, :2 * LANE]
        k = y[:, 2 * LANE:] * HEAD_DIM ** -0.5
        if use_rope:
            cs = cos_ref[r0:r0 + CHUNK, :]
            sn = sin_ref[r0:r0 + CHUNK, :]
            q = _rope(q, cs, sn, first, 16)
            k = _rope(k, cs, sn, first, 16)
        qt_s[dst] = q.T.astype(BF16)
        k_s[dst] = k.astype(BF16)
        vt = v_ref[0, r0:r0 + CHUNK, :].T
        for h in range(4):
            vat_s[h, dst] = jnp.concatenate([vt[HEAD_DIM * h:HEAD_DIM * (h + 1)], ones_blk], 0).astype(BF16)
        g = sm_ref[0, r0:r0 + CHUNK, :]
        lf = pltpu.roll(jax.nn.log_sigmoid(g), LANE - 4, 1)
        gcol_s[dst] = g - jnp.where(lane128 < 8, _dot_exact_l(tri, lf), _dot_exact_l(triu, lf))
        gt = g.T[0:16]
        lft = pltpu.roll(jax.nn.log_sigmoid(gt), 12, 0)
        brow = jnp.where(fwd_rows, _dot_exact_r(lft, triu), _dot_exact_r(lft, tri))
        brow_s[dst] = brow
        grow_s[dst] = gt - brow

    for ci in range(nc):
        prep(qkc_ref, vc_ref, smc_ref, ci, nc, ci, False)
    for ci in range(n - nc):
        prep(qkl_ref, vl_ref, sml_ref, ci, n - nc, nc + ci, True)
    g_all = grow_s[...].reshape(n * 16, CHUNK)
    fwd_all = (lax.broadcasted_iota(jnp.int32, (n * 16, 1), 0) % 16) < 8
    cm_s[...] = jnp.where(fwd_all, cummax_lanes(g_all, False), cummax_lanes(g_all, True)).reshape(n, 16, CHUNK)

    def scan_dir(bwd):
        ck_s[...] = jnp.zeros_like(ck_s)
        gi = 8 if bwd else 0
        causal = low if bwd else upp

        def chunk_body(c, ms):
            ch = _bwd_chunk(c, nc, n) if bwd else c
            qt = qt_s[ch]
            kb = k_s[ch]
            gcol_all = gcol_s[ch]
            ck_b = ck_s[...].astype(BF16)
            new_ms = []
            qtms = [jnp.where(hrows[h], qt, jnp.zeros_like(qt)) for h in range(4)]
            scores = [_dot(kb, qtms[h]) for h in range(4)]
            inters = [_dot(ck_b, qtms[h]) for h in range(4)]
            for h in range(4):
                m = ms[h]
                g_row = grow_s[ch, gi + h:gi + h + 1, :]
                b_row = brow_s[ch, gi + h:gi + h + 1, :]
                a_row = jnp.maximum(m, cm_s[ch, gi + h:gi + h + 1, :])
                w = jnp.exp(jnp.where(causal, gcol_all[:, gi + h:gi + h + 1] - a_row, NEG))
                pt = (scores[h] * w).astype(BF16)
                vat = vat_s[h, ch]
                nd = _dot(vat, pt) + jnp.exp(m - a_row) * inters[h]
                den = nd[HEAD_DIM:HEAD_DIM + 1, :]
                ht = nd[0:HEAD_DIM] / jnp.maximum(jnp.abs(den), jnp.exp(-(b_row + a_row)))
                rows = slice(HEAD_DIM * h, HEAD_DIM * (h + 1))
                if bwd:
                    ot_s[ch, rows, :] = ot_s[ch, rows, :] + ht
                else:
                    ot_s[ch, rows, :] = ht
                bl = b_row[:, 0:1] if bwd else b_row[:, CHUNK - 1:CHUNK]
                lw_end = bl + g_row
                m_new = jnp.maximum(bl + m, lw_end.max(-1, keepdims=True))
                upd = _dot((vat * jnp.exp(lw_end - m_new)).astype(BF16), kb)
                ck_s[...] = jnp.where(hmask[h], jnp.exp(bl + m - m_new) * ck_s[...] + upd, ck_s[...])
                new_ms.append(m_new)
            return tuple(new_ms)

        lax.fori_loop(0, n, chunk_body, tuple(jnp.zeros((1, 1), F32) for _ in range(4)))

    scan_dir(False)
    scan_dir(True)
    for ci in range(nc):
        oc_ref[0, ci * CHUNK:(ci + 1) * CHUNK, :] = ot_s[ci].T
    for ci in range(n - nc):
        ol_ref[0, ci * CHUNK:(ci + 1) * CHUNK, :] = ot_s[nc + ci].T


def _mlstm_call(p_lat, p_ctx, cos, sin, conv_w_l):
    B, T, _ = p_lat.shape
    Tc = p_ctx.shape[1]
    nc, n = Tc // CHUNK, (Tc + T) // CHUNK
    cb = lambda base, w: (lambda b: (b, 0, base // w))
    return pl.pallas_call(
        functools.partial(_mlstm_kernel, nc=nc, n=n),
        grid=(B,),
        in_specs=[pl.BlockSpec((1, T, 512), cb(C_MQK, 512)),
                  pl.BlockSpec((1, T, 256), cb(C_MV, 256)),
                  pl.BlockSpec((1, T, LANE), cb(C_SM, LANE)),
                  pl.BlockSpec((1, Tc, 512), cb(C_MQK, 512)),
                  pl.BlockSpec((1, Tc, 256), cb(C_MV, 256)),
                  pl.BlockSpec((1, Tc, LANE), cb(C_SM, LANE)),
                  pl.BlockSpec((T, 256), lambda b: (0, 0)),
                  pl.BlockSpec((T, 256), lambda b: (0, 0)),
                  pl.BlockSpec((CONV_K, 512), lambda b: (0, 0))],
        out_specs=[pl.BlockSpec((1, T, 256), lambda b: (b, 0, 0)),
                   pl.BlockSpec((1, Tc, 256), lambda b: (b, 0, 0))],
        out_shape=[jax.ShapeDtypeStruct((B, T, 256), F32),
                   jax.ShapeDtypeStruct((B, Tc, 256), F32)],
        scratch_shapes=[pltpu.VMEM((n, 256, CHUNK), BF16),
                        pltpu.VMEM((n, CHUNK, 256), BF16),
                        pltpu.VMEM((4, n, LANE, CHUNK), BF16),
                        pltpu.VMEM((n, CHUNK, LANE), F32),
                        pltpu.VMEM((n, 16, CHUNK), F32),
                        pltpu.VMEM((n, 16, CHUNK), F32),
                        pltpu.VMEM((n, 16, CHUNK), F32),
                        pltpu.VMEM((n, 256, CHUNK), F32),
                        pltpu.VMEM((LANE, 256), F32)],
        compiler_params=_cparams(("arbitrary",)),
        name="mlstm",
    )(p_lat, p_lat, p_lat, p_ctx, p_ctx, p_ctx, cos, sin, conv_w_l)


GLA_BLK = 64
GLA_CLAMP = 80.0


def _gla_kernel(ql_ref, kl_ref, vl_ref, sml_ref, qc_ref, kc_ref, vc_ref, smc_ref, cos_ref, sin_ref, w2_ref, b2_ref,
                ol_ref, oc_ref,
                q_s, k_s, v_s, a_s, b_s, o_s, st_s, *, nc, n):
    low, upp = _tri_masks()
    tri = jnp.where(low, 1.0, 0.0).astype(BF16)
    triu = jnp.where(upp, 1.0, 0.0).astype(BF16)
    lane128 = lax.broadcasted_iota(jnp.int32, (1, LANE), 1)
    lane256 = lax.broadcasted_iota(jnp.int32, (1, 2 * LANE), 1)
    first = (lane128 % 16) < 8
    hm128 = [(lane128 >= 32 * h) & (lane128 < 32 * (h + 1)) for h in range(4)]
    hm256 = [(lane256 >= 64 * h) & (lane256 < 64 * (h + 1)) for h in range(4)]
    nb = CHUNK // GLA_BLK
    r_st = lax.broadcasted_iota(jnp.int32, (nb * GLA_BLK, CHUNK), 0)
    c_st = lax.broadcasted_iota(jnp.int32, (nb * GLA_BLK, CHUNK), 1)
    bd_r = lax.broadcasted_iota(jnp.int32, (LANE, 2 * LANE), 0)
    bd_c = lax.broadcasted_iota(jnp.int32, (LANE, 2 * LANE), 1)
    blockdiag = (bd_r // 32) == (bd_c // 64)

    def prep(q_ref, k_ref, v_ref, sm_ref, ci, dst, use_rope):
        r0 = ci * CHUNK
        q = q_ref[0, r0:r0 + CHUNK, :] * 32 ** -0.5
        k = k_ref[0, r0:r0 + CHUNK, :]
        if use_rope:
            cs = cos_ref[r0:r0 + CHUNK, :]
            sn = sin_ref[r0:r0 + CHUNK, :]
            q = _rope(q, cs, sn, first, 8)
            k = _rope(k, cs, sn, first, 8)
        q_s[dst] = q
        k_s[dst] = k
        v_s[dst] = v_ref[0, r0:r0 + CHUNK, :].astype(BF16)
        lr = sm_ref[0, r0:r0 + CHUNK, :].astype(BF16)
        for d in range(2):
            a = jax.nn.log_sigmoid(_dot(lr, w2_ref[d]) + b2_ref[d]) / GLA_TAU
            a_s[d, dst] = a
            b_s[d, dst] = _dot_exact_l(triu if d else tri, a)

    for ci in range(nc):
        prep(qc_ref, kc_ref, vc_ref, smc_ref, ci, ci, False)
    for ci in range(n - nc):
        prep(ql_ref, kl_ref, vl_ref, sml_ref, ci, nc + ci, True)

    def scan_dir(bwd):
        d = 1 if bwd else 0
        st_s[...] = jnp.zeros_like(st_s)

        def chunk_body(c, carry):
            ch = _bwd_chunk(c, nc, n) if bwd else c
            q = q_s[ch]
            k = k_s[ch]
            vb = v_s[ch]
            a = a_s[d, ch]
            b = b_s[d, ch]
            st_b = st_s[...].astype(BF16)
            o_inter = _dot((q * jnp.exp(b)).astype(BF16), st_b)
            for i in range(nb):
                rows = slice(i * GLA_BLK, (i + 1) * GLA_BLK)
                e = (i + 1) * GLA_BLK - 1 if bwd else i * GLA_BLK
                ref = b[e:e + 1, :] - a[e:e + 1, :]
                qs = q[rows] * jnp.exp(b[rows] - ref)
                ks = (k * jnp.exp(jnp.minimum(ref - b, GLA_CLAMP))).astype(BF16)
                lhs = jnp.concatenate([jnp.where(hm128[h], qs, 0.0) for h in range(4)], 0).astype(BF16)
                att = _dot_nt(lhs, ks)
                t_idx = (r_st % GLA_BLK) + i * GLA_BLK
                ok = (c_st >= t_idx) if bwd else (c_st <= t_idx)
                att = jnp.where(ok, att, 0.0).astype(BF16)
                oh = _dot(att, vb)
                o_blk = o_inter[rows]
                for h in range(4):
                    o_blk = o_blk + jnp.where(hm256[h], oh[h * GLA_BLK:(h + 1) * GLA_BLK], 0.0)
                if bwd:
                    o_s[ch, rows, :] = o_s[ch, rows, :] + o_blk
                else:
                    o_s[ch, rows, :] = o_blk
            bt = b.T
            tot = bt[:, 0:1] if bwd else bt[:, CHUNK - 1:CHUNK]
            kt = (k.T * jnp.exp(tot - bt)).astype(BF16)
            upd = jnp.where(blockdiag, _dot(kt, vb), 0.0)
            st_s[...] = jnp.exp(tot) * st_s[...] + upd
            return carry

        lax.fori_loop(0, n, chunk_body, 0)

    scan_dir(False)
    scan_dir(True)
    for ci in range(nc):
        oc_ref[0, ci * CHUNK:(ci + 1) * CHUNK, :] = o_s[ci]
    for ci in range(n - nc):
        ol_ref[0, ci * CHUNK:(ci + 1) * CHUNK, :] = o_s[nc + ci]


def _gla_call(p_lat, p_ctx, cos, sin, w2p, b2p):
    B, T, _ = p_lat.shape
    Tc = p_ctx.shape[1]
    nc, n = Tc // CHUNK, (Tc + T) // CHUNK
    cb = lambda base, w: (lambda b: (b, 0, base // w))
    return pl.pallas_call(
        functools.partial(_gla_kernel, nc=nc, n=n),
        grid=(B,),
        in_specs=[pl.BlockSpec((1, T, LANE), cb(C_GQ, LANE)),
                  pl.BlockSpec((1, T, LANE), cb(C_GK, LANE)),
                  pl.BlockSpec((1, T, 256), cb(C_GV, 256)),
                  pl.BlockSpec((1, T, LANE), cb(C_SM, LANE)),
                  pl.BlockSpec((1, Tc, LANE), cb(C_GQ, LANE)),
                  pl.BlockSpec((1, Tc, LANE), cb(C_GK, LANE)),
                  pl.BlockSpec((1, Tc, 256), cb(C_GV, 256)),
                  pl.BlockSpec((1, Tc, LANE), cb(C_SM, LANE)),
                  pl.BlockSpec((T, LANE), lambda b: (0, 0)),
                  pl.BlockSpec((T, LANE), lambda b: (0, 0)),
                  pl.BlockSpec((2, LANE, LANE), lambda b: (0, 0, 0)),
                  pl.BlockSpec((2, 1, LANE), lambda b: (0, 0, 0))],
        out_specs=[pl.BlockSpec((1, T, 256), lambda b: (b, 0, 0)),
                   pl.BlockSpec((1, Tc, 256), lambda b: (b, 0, 0))],
        out_shape=[jax.ShapeDtypeStruct((B, T, 256), F32),
                   jax.ShapeDtypeStruct((B, Tc, 256), F32)],
        scratch_shapes=[pltpu.VMEM((n, CHUNK, LANE), F32),
                        pltpu.VMEM((n, CHUNK, LANE), F32),
                        pltpu.VMEM((n, CHUNK, 256), BF16),
                        pltpu.VMEM((2, n, CHUNK, LANE), F32),
                        pltpu.VMEM((2, n, CHUNK, LANE), F32),
                        pltpu.VMEM((n, CHUNK, 256), F32),
                        pltpu.VMEM((LANE, 256), F32)],
        compiler_params=_cparams(("arbitrary",)),
        name="gla",
    )(p_lat, p_lat, p_lat, p_lat, p_ctx, p_ctx, p_ctx, p_ctx, cos, sin, w2p, b2p)


def _gla_gate_weights(gla_w2, gla_b2):
    w = jnp.zeros((DEPTH, 2, LANE, LANE), F32)
    w = w.at[:, 0, 16:32].set(gla_w2[:, 0]).at[:, 1, 32:48].set(gla_w2[:, 1])
    return w.astype(BF16), gla_b2.astype(F32).reshape(DEPTH, 2, 1, LANE)


def _layer_norm(z, g, b):
    mu = z.mean(-1, keepdims=True)
    zc = z - mu
    var = jnp.mean(jnp.square(zc), -1, keepdims=True)
    return zc * lax.rsqrt(var + LN_EPS) * g + b


def _merge_kernel(hm_ref, hg_ref, hn_ref, mo_ref, gr_ref, x_ref, mod_ref, wout_ref, ng_ref, ln_ref, x1_ref, u2_ref):
    r = lax.broadcasted_iota(jnp.int32, (256, 256), 0)
    c = lax.broadcasted_iota(jnp.int32, (256, 256), 1)
    avg = jnp.where((r // HEAD_DIM) == (c // HEAD_DIM), 1.0 / HEAD_DIM, 0.0).astype(BF16)

    def seg_mean(x):
        hi = x.astype(BF16)
        lo = (x - hi.astype(F32)).astype(BF16)
        return _dot(hi, avg) + _dot(lo, avg)

    def head_norm(h):
        d = h - seg_mean(h)
        return d * lax.rsqrt(seg_mean(d * d) + LN_EPS)

    ym = head_norm(hm_ref[0]) * ng_ref[0:1, :] * jax.nn.sigmoid(mo_ref[0])
    yg = head_norm(hg_ref[0]) * ng_ref[1:2, :] * jax.nn.silu(gr_ref[0])
    y = (_dot(ym.astype(BF16), wout_ref[0:256, :]) + _dot(yg.astype(BF16), wout_ref[256:512, :])
         + _dot(hn_ref[0].astype(BF16), wout_ref[512:1024, :]))
    m = mod_ref[0]
    x1 = _layer_norm(DEEPNORM_ALPHA * x_ref[0] + m[:, 2 * D:3 * D] * y, ln_ref[0:1, :], ln_ref[1:2, :])
    x1_ref[0] = x1
    u2_ref[0] = x1 * (1.0 + m[:, 4 * D:5 * D]) + m[:, 3 * D:4 * D]


def _merge_call(hm, hg, hn, p, x, mod3, wout_b, ng, ln, shared_row):
    B, T, _ = x.shape
    tm = 256
    mod_map = (lambda b, i: (b, 0, 0)) if shared_row is None else (lambda b, i: (shared_row, 0, 0))
    tok = lambda w: pl.BlockSpec((1, tm, w), lambda b, i: (b, i, 0))
    return pl.pallas_call(
        _merge_kernel,
        grid=(B, T // tm),
        in_specs=[tok(256), tok(256), tok(512),
                  pl.BlockSpec((1, tm, 256), lambda b, i: (b, i, C_MO // 256)),
                  pl.BlockSpec((1, tm, 256), lambda b, i: (b, i, C_GR // 256)),
                  tok(D),
                  pl.BlockSpec((1, 1, 6 * D), mod_map),
                  pl.BlockSpec((D, D), lambda b, i: (0, 0)),
                  pl.BlockSpec((2, 256), lambda b, i: (0, 0)),
                  pl.BlockSpec((2, D), lambda b, i: (0, 0))],
        out_specs=[tok(D), tok(D)],
        out_shape=[jax.ShapeDtypeStruct((B, T, D), F32), jax.ShapeDtypeStruct((B, T, D), F32)],
        compiler_params=_cparams(("arbitrary", "arbitrary")),
        name="merge",
    )(hm, hg, hn, p, p, x, mod3, wout_b, ng, ln)


ROUTER_SAMPLES = 2


def _router_kernel(u_ref, wr_ref, o_ref, aff_s, sp_s, *, T, cap):
    J = T // LANE
    E = N_EXPERTS
    NS = ROUTER_SAMPLES
    wrt = wr_ref[...]
    keys = [[] for _ in range(NS)]
    for j in range(J):
        for si in range(NS):
            lg = _dot_nt(wrt, u_ref[si, j * LANE:(j + 1) * LANE, :].astype(BF16))
            ex = jnp.exp(lg - lg.max(0, keepdims=True))
            aff = ex / ex.sum(0, keepdims=True)
            aff_s[si, j * E:(j + 1) * E, :] = aff
            keys[si].append(pltpu.bitcast(aff, jnp.int32))

    def count(ks, pred):
        cnt = None
        for k in ks:
            cj = jnp.where(pred(k), 1.0, 0.0)
            cnt = cj if cnt is None else cnt + cj
        return cnt.sum(-1, keepdims=True)

    thrs = [jnp.zeros((E, 1), jnp.int32) for _ in range(NS)]
    for bit in range(30, -1, -1):
        for si in range(NS):
            cand = thrs[si] | (1 << bit)
            thrs[si] = jnp.where(count(keys[si], lambda k: k >= cand) >= cap, cand, thrs[si])

    r = lax.broadcasted_iota(jnp.int32, (LANE, LANE), 0)
    c = lax.broadcasted_iota(jnp.int32, (LANE, LANE), 1)
    upper = jnp.where(r <= c, 1.0, 0.0).astype(BF16)
    ones = jnp.ones((LANE, LANE), BF16)
    rr = lax.broadcasted_iota(jnp.int32, (J * E, J * E), 0)
    cc = lax.broadcasted_iota(jnp.int32, (J * E, J * E), 1)
    earlier = jnp.where(((rr % E) == (cc % E)) & ((cc // E) < (rr // E)), 1.0, 0.0).astype(BF16)

    def prefix(x01):
        xb = x01.astype(BF16)
        return _dot(xb, upper) + _dot(earlier, _dot(xb, ones).astype(BF16))

    for si in range(NS):
        thr = thrs[si]
        need = cap - count(keys[si], lambda k: k > thr)
        gt = jnp.concatenate([jnp.where(k > thr, 1.0, 0.0) for k in keys[si]], 0)
        eq = jnp.concatenate([jnp.where(k == thr, 1.0, 0.0) for k in keys[si]], 0)
        need_t = jnp.concatenate([need] * J, 0)
        sel = jnp.maximum(gt, jnp.where(prefix(eq) <= need_t, eq, 0.0))
        sp_s[si] = jnp.where(sel > 0.0, prefix(sel) - 1.0, -1.0)

    sb = min(cap, LANE)
    lane = lax.broadcasted_iota(jnp.int32, (1, LANE), 1)
    lane_f = lane.astype(F32)
    o_ref[...] = jnp.zeros_like(o_ref)
    for e in range(E):
        for half in range(cap // sb):
            slot = (lax.broadcasted_iota(jnp.int32, (sb, LANE), 0) + half * sb).astype(F32)
            rows = slice(half * sb, (half + 1) * sb)
            for si in range(NS):
                def jbody(j, acc):
                    acc_i, acc_g = acc
                    sp = sp_s[si, pl.ds(j * E + e, 1), :]
                    af = aff_s[si, pl.ds(j * E + e, 1), :]
                    hit = sp == slot
                    tid = lane_f + (j * LANE).astype(F32)
                    return jnp.where(hit, tid, acc_i), jnp.where(hit, af, acc_g)

                acc_i, acc_g = lax.fori_loop(0, J, jbody, (jnp.zeros((sb, LANE), F32), jnp.zeros((sb, LANE), F32)),
                                             unroll=2)
                icol = acc_i.sum(-1, keepdims=True)
                gcol = acc_g.sum(-1, keepdims=True)
                o_ref[si, rows, :] = jnp.where(lane == e, icol, jnp.where(lane == E + e, gcol, o_ref[si, rows, :]))


def _router_call(u2, wrt_b):
    B, T, _ = u2.shape
    cap = CAPACITY_FACTOR * T // N_EXPERTS
    J = T // LANE
    ns = ROUTER_SAMPLES
    assert B % ns == 0
    out = pl.pallas_call(
        functools.partial(_router_kernel, T=T, cap=cap),
        grid=(B // ns,),
        in_specs=[pl.BlockSpec((ns, T, D), lambda b: (b, 0, 0)),
                  pl.BlockSpec((N_EXPERTS, D), lambda b: (0, 0))],
        out_specs=pl.BlockSpec((ns, cap, LANE), lambda b: (b, 0, 0)),
        out_shape=jax.ShapeDtypeStruct((B, cap, LANE), F32),
        scratch_shapes=[pltpu.VMEM((ns, J * N_EXPERTS, LANE), F32), pltpu.VMEM((ns, J * N_EXPERTS, LANE), F32)],
        compiler_params=_cparams(("arbitrary",)),
        name="router",
    )(u2, wrt_b)
    idx = out[:, :, :N_EXPERTS].astype(jnp.int32).transpose(0, 2, 1).reshape(B, 1, N_EXPERTS * cap)
    gate = out[:, :, N_EXPERTS:2 * N_EXPERTS].transpose(0, 2, 1).reshape(B, 1, N_EXPERTS * cap)
    return idx, gate


def _gather_kernel(idx_ref, u_ref, o_ref, *, cap, eg):
    g = pl.program_id(1)
    for e in range(eg):
        def body(s, carry):
            r = idx_ref[0, 0, (g * eg + e) * cap + s]
            o_ref[e, 0, pl.ds(s, 1), :] = u_ref[0, pl.ds(r, 1), :]
            return carry
        lax.fori_loop(0, cap, body, 0, unroll=8)


def _gather_call(idx, u2):
    B, T, _ = u2.shape
    cap = CAPACITY_FACTOR * T // N_EXPERTS
    eg = 4
    smem = lambda: pl.BlockSpec((1, 1, N_EXPERTS * cap), lambda b, g: (b, 0, 0), memory_space=pltpu.SMEM)
    return pl.pallas_call(
        functools.partial(_gather_kernel, cap=cap, eg=eg),
        grid=(B, N_EXPERTS // eg),
        in_specs=[smem(), pl.BlockSpec((1, T, D), lambda b, g: (b, 0, 0))],
        out_specs=pl.BlockSpec((eg, 1, cap, D), lambda b, g: (g, b, 0, 0)),
        out_shape=jax.ShapeDtypeStruct((N_EXPERTS, B, cap, D), F32),
        compiler_params=_cparams(("arbitrary", "arbitrary")),
        name="gather",
    )(idx, u2)


FF_CHUNK = 512


def _ffn_kernel(x_ref, wg_ref, wu_ref, wd_ref, o_ref):
    xb = x_ref[0].astype(BF16)
    acc = None
    for c in range(EXPERT_FF // FF_CHUNK):
        cols = slice(c * FF_CHUNK, (c + 1) * FF_CHUNK)
        h = (jax.nn.silu(_dot(xb, wg_ref[0, :, cols])) * _dot(xb, wu_ref[0, :, cols])).astype(BF16)
        t = _dot(h, wd_ref[0, cols, :])
        acc = t if acc is None else acc + t
    o_ref[0] = acc


def _ffn_call(xe, wg_b, wu_b, wd_b):
    E, M, _ = xe.shape
    tm = min(M, 512)
    return pl.pallas_call(
        _ffn_kernel,
        grid=(E, M // tm),
        in_specs=[pl.BlockSpec((1, tm, D), lambda e, i: (e, i, 0)),
                  pl.BlockSpec((1, D, EXPERT_FF), lambda e, i: (e, 0, 0)),
                  pl.BlockSpec((1, D, EXPERT_FF), lambda e, i: (e, 0, 0)),
                  pl.BlockSpec((1, EXPERT_FF, D), lambda e, i: (e, 0, 0))],
        out_specs=pl.BlockSpec((1, tm, D), lambda e, i: (e, i, 0)),
        out_shape=jax.ShapeDtypeStruct((E, M, D), F32),
        compiler_params=_cparams(("arbitrary", "arbitrary")),
        name="ffn",
    )(xe, wg_b, wu_b, wd_b)


def _scatter_kernel(idx_ref, gate_ref, y_ref, x1_ref, mod_ref, ln_ref, o_ref, acc2, *, cap, n_g):
    g = pl.program_id(1)

    @pl.when(g == 0)
    def _():
        o_ref[...] = jnp.zeros_like(o_ref)
        acc2[...] = jnp.zeros_like(acc2)

    def body(s, carry):
        p0 = (g * 2) * cap + s
        p1 = p0 + cap
        r0 = idx_ref[0, 0, p0]
        r1 = idx_ref[0, 0, p1]
        o_ref[0, pl.ds(r0, 1), :] = o_ref[0, pl.ds(r0, 1), :] + y_ref[0, 0, pl.ds(s, 1), :] * gate_ref[0, 0, p0]
        acc2[pl.ds(r1, 1), :] = acc2[pl.ds(r1, 1), :] + y_ref[1, 0, pl.ds(s, 1), :] * gate_ref[0, 0, p1]
        return carry
    lax.fori_loop(0, cap, body, 0, unroll=4)

    @pl.when(g == n_g - 1)
    def _():
        z = DEEPNORM_ALPHA * x1_ref[0] + mod_ref[0][:, 5 * D:6 * D] * (o_ref[0] + acc2[...])
        o_ref[0] = _layer_norm(z, ln_ref[0:1, :], ln_ref[1:2, :])


def _scatter_call(idx, gate, y4, x1, mod3, ln, shared_row):
    B, T, _ = x1.shape
    cap = CAPACITY_FACTOR * T // N_EXPERTS
    eg = 2
    n_g = N_EXPERTS // eg
    mod_map = (lambda b, g: (b, 0, 0)) if shared_row is None else (lambda b, g: (shared_row, 0, 0))
    smem = lambda: pl.BlockSpec((1, 1, N_EXPERTS * cap), lambda b, g: (b, 0, 0), memory_space=pltpu.SMEM)
    return pl.pallas_call(
        functools.partial(_scatter_kernel, cap=cap, n_g=n_g),
        grid=(B, n_g),
        scratch_shapes=[pltpu.VMEM((T, D), F32)],
        in_specs=[smem(), smem(),
                  pl.BlockSpec((eg, 1, cap, D), lambda b, g: (g, b, 0, 0)),
                  pl.BlockSpec((1, T, D), lambda b, g: (b, 0, 0)),
                  pl.BlockSpec((1, 1, 6 * D), mod_map),
                  pl.BlockSpec((2, D), lambda b, g: (0, 0))],
        out_specs=pl.BlockSpec((1, T, D), lambda b, g: (b, 0, 0)),
        out_shape=jax.ShapeDtypeStruct((B, T, D), F32),
        compiler_params=_cparams(("arbitrary", "arbitrary")),
        name="scatter",
    )(idx, gate, y4, x1, mod3, ln)


def _moe(x1, u2, mod3, wrt_b, wg_b, wu_b, wd_b, ln2, shared_row):
    B, T, _ = x1.shape
    cap = CAPACITY_FACTOR * T // N_EXPERTS
    idx, gate = _router_call(u2, wrt_b)
    xe = _gather_call(idx, u2)
    y = _ffn_call(xe.reshape(N_EXPERTS, B * cap, D), wg_b, wu_b, wd_b)
    return _scatter_call(idx, gate, y.reshape(N_EXPERTS, B, cap, D), x1, mod3, ln2, shared_row)


def kernel(x, c, ctx, c_ctx, w_mod, b_mod, w_in, b_in, conv_w, gla_w2, gla_b2, mlstm_norm_g, gla_norm_g, rpb, w_out, ln1_g, ln1_b, w_router, w_gate, w_up, w_down, ln2_g, ln2_b):
    B, T, _ = x.shape
    n_mod = -(-(B + 1) // 8) * 8
    c_all = jnp.concatenate([c, c_ctx[None], jnp.zeros((n_mod - B - 1, D), F32)], 0)
    mods = _mod_call(c_all, w_mod, b_mod)
    w_p = jnp.concatenate([w_in[..., _PERM], jnp.zeros((DEPTH, D, _NPAD), F32)], -1).astype(BF16)
    b_p = jnp.concatenate([b_in[..., _PERM], jnp.zeros((DEPTH, _NPAD), F32)], -1)
    bias = _natten_bias_tables(rpb)
    mcos, msin = _rope_tables(T, 4, 16)
    gcos, gsin = _rope_tables(T, 4, 8)
    w2p, b2p = _gla_gate_weights(gla_w2, gla_b2)
    wout_b = w_out.astype(BF16)
    wrt_b = jnp.swapaxes(w_router, 1, 2).astype(BF16)
    wg_b, wu_b, wd_b = w_gate.astype(BF16), w_up.astype(BF16), w_down.astype(BF16)
    ng = jnp.stack([mlstm_norm_g, gla_norm_g], 1)
    ln1 = jnp.stack([ln1_g, ln1_b], 1)
    ln2 = jnp.stack([ln2_g, ln2_b], 1)
    for l in range(DEPTH):
        mod3 = mods[l].reshape(n_mod, 1, 6 * D)
        p_lat = _inproj_call(x, mod3, w_p[l], b_p[l][None], None)
        p_ctx = _inproj_call(ctx, mod3, w_p[l], b_p[l][None], B)
        n_lat, n_ctx = _natten_call(p_lat, p_ctx, bias[l])
        m_lat, m_ctx = _mlstm_call(p_lat, p_ctx, mcos, msin, conv_w[l])
        g_lat, g_ctx = _gla_call(p_lat, p_ctx, gcos, gsin, w2p[l], b2p[l])
        x1, u2 = _merge_call(m_lat, g_lat, n_lat, p_lat, x, mod3, wout_b[l], ng[l], ln1[l], None)
        x = _moe(x1, u2, mod3, wrt_b[l], wg_b[l], wu_b[l], wd_b[l], ln2[l], None)
        if l < DEPTH - 1:
            c1, u2c = _merge_call(m_ctx, g_ctx, n_ctx, p_ctx, ctx, mod3, wout_b[l], ng[l], ln1[l], B)
            ctx = _moe(c1, u2c, mod3, wrt_b[l], wg_b[l], wu_b[l], wd_b[l], ln2[l], B)
    return x
```

```python
import functools

import numpy as np
import jax
import jax.numpy as jnp
from jax import lax
from jax.experimental import pallas as pl
from jax.experimental.pallas import tpu as pltpu

F32 = jnp.float32
BF16 = jnp.bfloat16

D = 1024
DEPTH = 4
GRID_W = 64
HEAD_DIM = 64
WIN_ROWS = 8
WIN_COLS = 16
CONV_K = 3
ROPE_BASE = 10000.0
N_EXPERTS = 16
EXPERT_FF = 2 * D
CAPACITY_FACTOR = 2
LN_EPS = 1e-5
GLA_TAU = 16.0
GLA_RANK = 16
DEEPNORM_ALPHA = (2 * DEPTH) ** 0.25
NEG = -1e30

VMEM_LIMIT = 56 * 1024 * 1024
LANE = 128
CHUNK = 256

C_MQK, C_NQ, C_NK, C_NV = 0, 512, 1024, 1536
C_MV, C_MO, C_GV, C_GR = 2048, 2304, 2560, 2816
C_GQ, C_GK, C_SM = 3072, 3200, 3328
PW = 3456
_ORIG = dict(m_q=0, m_k=256, m_v=512, m_o=768, m_g=1024, g_q=1040, g_k=1168, g_v=1296, g_r=1552,
             g_lr=1808, n_q=1840, n_k=2352, n_v=2864)
_PERM = np.concatenate([
    np.arange(0, 512), np.arange(1840, 2352), np.arange(2352, 2864), np.arange(2864, 3376),
    np.arange(512, 768), np.arange(768, 1024), np.arange(1296, 1552), np.arange(1552, 1808),
    np.arange(1040, 1168), np.arange(1168, 1296), np.arange(1024, 1040), np.arange(1808, 1840)])
_NPAD = PW - _PERM.size
_SEGMENTS = ((0, 512), (1840, 2352), (2352, 2864), (2864, 3376), (512, 768), (768, 1024), (1296, 1552),
             (1552, 1808), (1040, 1168), (1168, 1296), (1024, 1040), (1808, 1840))


def _repack_columns(w):
    parts = [w[..., a:b] for a, b in _SEGMENTS]
    return jnp.concatenate(parts + [jnp.zeros(w.shape[:-1] + (_NPAD,), w.dtype)], -1)


def _cparams(sem):
    return pltpu.CompilerParams(dimension_semantics=sem, vmem_limit_bytes=VMEM_LIMIT)


def _dot(a, b):
    return jnp.dot(a, b, preferred_element_type=F32)


def _dot_nt(a, b):
    return lax.dot_general(a, b, (((1,), (1,)), ((), ())), preferred_element_type=F32)


def _split3(x):
    hi = x.astype(BF16)
    r1 = x - hi.astype(F32)
    mid = r1.astype(BF16)
    lo = (r1 - mid.astype(F32)).astype(BF16)
    return hi, mid, lo


def _dot_exact_l(m01, x):
    hi, mid, lo = _split3(x)
    return _dot(m01, hi) + _dot(m01, mid) + _dot(m01, lo)


def _dot_exact_r(x, m01):
    hi, mid, lo = _split3(x)
    return _dot(hi, m01) + _dot(mid, m01) + _dot(lo, m01)


def _mod_kernel(c_ref, w_ref, b_ref, o_ref):
    s = jax.nn.silu(c_ref[...]).astype(BF16)
    o_ref[0] = _dot(s, w_ref[0].astype(BF16)) + b_ref[0]


def _mod_call(c_all, w_mod, b_mod):
    rows = c_all.shape[0]
    tn = 1536
    return pl.pallas_call(
        _mod_kernel,
        grid=(DEPTH, 6 * D // tn),
        in_specs=[pl.BlockSpec((rows, D), lambda l, j: (0, 0)),
                  pl.BlockSpec((1, D, tn), lambda l, j: (l, 0, j)),
                  pl.BlockSpec((1, 1, tn), lambda l, j: (l, 0, j))],
        out_specs=pl.BlockSpec((1, rows, tn), lambda l, j: (l, 0, j)),
        out_shape=jax.ShapeDtypeStruct((DEPTH, rows, 6 * D), F32),
        compiler_params=_cparams(("arbitrary", "arbitrary")),
        name="mod",
    )(c_all, w_mod, b_mod.reshape(DEPTH, 1, 6 * D))


def _inproj_kernel(x_ref, mod_ref, w_ref, b_ref, o_ref):
    m = mod_ref[0]
    u = x_ref[0] * (1.0 + m[:, D:2 * D]) + m[:, 0:D]
    o_ref[0] = _dot(u.astype(BF16), w_ref[...]) + b_ref[...]


def _inproj_call(x, mod3, w_p, b_p, shared_row):
    B, T, _ = x.shape
    tm = min(T, 512)
    if shared_row is None:
        mod_map = lambda b, i: (b, 0, 0)
    else:
        mod_map = lambda b, i: (shared_row, 0, 0)
    return pl.pallas_call(
        _inproj_kernel,
        grid=(B, T // tm),
        in_specs=[pl.BlockSpec((1, tm, D), lambda b, i: (b, i, 0)),
                  pl.BlockSpec((1, 1, 6 * D), mod_map),
                  pl.BlockSpec((D, PW), lambda b, i: (0, 0)),
                  pl.BlockSpec((1, PW), lambda b, i: (0, 0))],
        out_specs=pl.BlockSpec((1, tm, PW), lambda b, i: (b, i, 0)),
        out_shape=jax.ShapeDtypeStruct((B, T, PW), F32),
        compiler_params=_cparams(("arbitrary", "arbitrary")),
        name="inproj",
    )(x, mod3, w_p, b_p)


def _softmax_av(s_list, v_list):
    m = s_list[0].max(-1, keepdims=True)
    for s in s_list[1:]:
        m = jnp.maximum(m, s.max(-1, keepdims=True))
    acc = None
    l = None
    for s, v in zip(s_list, v_list):
        p = jnp.exp(s - m)
        ls = p.sum(-1, keepdims=True)
        o = _dot(p.astype(BF16), v)
        acc = o if acc is None else acc + o
        l = ls if l is None else l + ls
    return acc / l


NAT_GROUP = 4


def _natten_kernel(q_ref, k_ref, v_ref, qc_ref, kc_ref, vc_ref, bias_ref, o_ref, oc_ref, ks, vs, *, n_rows):
    ks[...] = k_ref[0].astype(BF16)
    vs[...] = v_ref[0].astype(BF16)
    kcb = kc_ref[0].astype(BF16)
    vcb = vc_ref[0].astype(BF16)
    lane = lax.broadcasted_iota(jnp.int32, (1, LANE), 1)
    head0 = lane < HEAD_DIM
    scale = HEAD_DIM ** -0.5

    def stack_heads(q):
        return jnp.concatenate([jnp.where(head0, q, 0.0), jnp.where(head0, 0.0, q)], 0).astype(BF16)

    def unstack(o, n):
        return jnp.where(head0, o[:n], o[n:])

    def rows_body(g, carry):
        koffs, scores = [], []
        for i in range(NAT_GROUP):
            r = g * NAT_GROUP + i
            rs = jnp.clip(r - WIN_ROWS // 2, 0, n_rows - WIN_ROWS)
            qs = stack_heads(q_ref[0, pl.ds(pl.multiple_of(r * GRID_W, GRID_W), GRID_W), :] * scale)
            koff = pl.multiple_of(rs * GRID_W, GRID_W)
            kl = ks[pl.ds(koff, WIN_ROWS * GRID_W), :]
            koffs.append(koff)
            scores.append([_dot_nt(qs, kl) + bias_ref[r - rs, 0], _dot_nt(qs, kcb)])
        for i in range(NAT_GROUP):
            r = g * NAT_GROUP + i
            vl = vs[pl.ds(koffs[i], WIN_ROWS * GRID_W), :]
            o = _softmax_av(scores[i], [vl, vcb])
            o_ref[0, pl.ds(pl.multiple_of(r * GRID_W, GRID_W), GRID_W), :] = unstack(o, GRID_W)
        return carry

    lax.fori_loop(0, n_rows // NAT_GROUP, rows_body, 0)

    tc = qc_ref.shape[1]
    oc = _softmax_av([_dot_nt(stack_heads(qc_ref[0] * scale), kcb)], [vcb])
    oc_ref[0] = unstack(oc, tc)


def _natten_call(p_lat, p_ctx, bias_l):
    B, T, _ = p_lat.shape
    Tc = p_ctx.shape[1]
    n_rows = T // GRID_W
    assert n_rows >= WIN_ROWS
    cb = lambda base: (lambda b, p: (b, 0, base // LANE + p))
    return pl.pallas_call(
        functools.partial(_natten_kernel, n_rows=n_rows),
        grid=(B, 4),
        in_specs=[pl.BlockSpec((1, T, LANE), cb(C_NQ)),
                  pl.BlockSpec((1, T, LANE), cb(C_NK)),
                  pl.BlockSpec((1, T, LANE), cb(C_NV)),
                  pl.BlockSpec((1, Tc, LANE), cb(C_NQ)),
                  pl.BlockSpec((1, Tc, LANE), cb(C_NK)),
                  pl.BlockSpec((1, Tc, LANE), cb(C_NV)),
                  pl.BlockSpec((WIN_ROWS, 1, 2 * GRID_W, WIN_ROWS * GRID_W), lambda b, p: (0, p, 0, 0))],
        out_specs=[pl.BlockSpec((1, T, LANE), lambda b, p: (b, 0, p)),
                   pl.BlockSpec((1, Tc, LANE), lambda b, p: (b, 0, p))],
        out_shape=[jax.ShapeDtypeStruct((B, T, 4 * LANE), F32),
                   jax.ShapeDtypeStruct((B, Tc, 4 * LANE), F32)],
        scratch_shapes=[pltpu.VMEM((T, LANE), BF16), pltpu.VMEM((T, LANE), BF16)],
        compiler_params=_cparams(("arbitrary", "arbitrary")),
        name="natten",
    )(p_lat, p_lat, p_lat, p_ctx, p_ctx, p_ctx, bias_l)


def _natten_bias_tables(rpb):
    col = np.arange(GRID_W)
    cstart = np.clip(col - WIN_COLS // 2, 0, GRID_W - WIN_COLS)
    col_ok = (col[None, :] >= cstart[:, None]) & (col[None, :] < cstart[:, None] + WIN_COLS)
    dc_idx = np.clip(col[None, :] - col[:, None] + WIN_COLS - 1, 0, 2 * WIN_COLS - 2)
    dr_idx = np.arange(WIN_ROWS)[None, :] - np.arange(WIN_ROWS)[:, None] + WIN_ROWS - 1
    t = rpb.astype(F32)[:, :, dr_idx]
    t = t[..., dc_idx]
    t = jnp.where(col_ok[None, None, None, None], t, NEG)
    t = t.transpose(0, 2, 1, 4, 3, 5)
    return t.reshape(DEPTH, WIN_ROWS, 4, 2 * GRID_W, WIN_ROWS * GRID_W)


def _tri_masks():
    r = lax.broadcasted_iota(jnp.int32, (CHUNK, CHUNK), 0)
    c = lax.broadcasted_iota(jnp.int32, (CHUNK, CHUNK), 1)
    return r >= c, r <= c


def _rope(x, cs, sn, first, dist):
    w = x.shape[-1]
    partner = jnp.where(first, pltpu.roll(x, w - dist, 1), pltpu.roll(x, dist, 1))
    return x * cs + partner * sn


def _rope_tables(T, n_heads, half):
    t = jnp.arange(T)
    rows = (t // GRID_W).astype(F32)
    cols = (t % GRID_W).astype(F32)
    inv = ROPE_BASE ** (-jnp.arange(half, dtype=F32) / half)
    ar = rows[:, None] * inv[None, :]
    ac = cols[:, None] * inv[None, :]
    cos = jnp.concatenate([jnp.cos(ar), jnp.cos(ar), jnp.cos(ac), jnp.cos(ac)], -1)
    sin = jnp.concatenate([-jnp.sin(ar), jnp.sin(ar), -jnp.sin(ac), jnp.sin(ac)], -1)
    return jnp.tile(cos, (1, n_heads)), jnp.tile(sin, (1, n_heads))


def _bwd_chunk(c, nc, n):
    return jnp.where(c < nc, nc - 1 - c, n - 1 - c + nc)


def _mlstm_kernel(qkl_ref, vl_ref, sml_ref, qkc_ref, vc_ref, smc_ref, cos_ref, sin_ref, cw_ref,
                  ol_ref, oc_ref,
                  qt_s, k_s, vat_s, gcol_s, grow_s, brow_s, cm_s, ot_s, ck_s, *, nc, n):
    low, upp = _tri_masks()
    tri = jnp.where(low, 1.0, 0.0).astype(BF16)
    triu = jnp.where(upp, 1.0, 0.0).astype(BF16)
    lane128 = lax.broadcasted_iota(jnp.int32, (1, LANE), 1)
    lane256 = lax.broadcasted_iota(jnp.int32, (1, 2 * LANE), 1)
    rowi = lax.broadcasted_iota(jnp.int32, (CHUNK, 1), 0)
    row16 = lax.broadcasted_iota(jnp.int32, (16, 1), 0)
    first = (lane256 % 32) < 16
    hmask = [(lane256 >= HEAD_DIM * h) & (lane256 < HEAD_DIM * (h + 1)) for h in range(4)]
    hrows = [(rowi >= HEAD_DIM * h) & (rowi < HEAD_DIM * (h + 1)) for h in range(4)]
    ones_blk = jnp.where(lax.broadcasted_iota(jnp.int32, (HEAD_DIM, CHUNK), 0) == 0, 1.0, 0.0)
    fwd_rows = row16 < 8
    cw = cw_ref[...]

    def cummax_lanes(x, suffix):
        sh = 1
        while sh < CHUNK:
            if suffix:
                moved = jnp.where(lane256 < CHUNK - sh, pltpu.roll(x, CHUNK - sh, 1), NEG)
            else:
                moved = jnp.where(lane256 >= sh, pltpu.roll(x, sh, 1), NEG)
            x = jnp.maximum(x, moved)
            sh *= 2
        return x

    def prep(qk_ref, v_ref, sm_ref, ci, n_str, dst, use_rope):
        r0 = ci * CHUNK
        xc = qk_ref[0, r0:r0 + CHUNK, :]
        prev = qk_ref[0, r0 - 1:r0, :] if ci > 0 else jnp.zeros((1, 4 * LANE), F32)
        nxt = qk_ref[0, r0 + CHUNK:r0 + CHUNK + 1, :] if ci < n_str - 1 else jnp.zeros((1, 4 * LANE), F32)
        xp = jnp.where(rowi == 0, prev, pltpu.roll(xc, 1, 0))
        xn = jnp.where(rowi == CHUNK - 1, nxt, pltpu.roll(xc, CHUNK - 1, 0))
        y = jax.nn.silu(xp * cw[0:1] + xc * cw[1:2] + xn * cw[2:3])
        q = y[:, :2 * LANE]
        k = y[:, 2 * LANE:] * HEAD_DIM ** -0.5
        if use_rope:
            cs = cos_ref[r0:r0 + CHUNK, :]
            sn = sin_ref[r0:r0 + CHUNK, :]
            q = _rope(q, cs, sn, first, 16)
            k = _rope(k, cs, sn, first, 16)
        qt_s[dst] = q.T.astype(BF16)
        k_s[dst] = k.astype(BF16)
        vt = v_ref[0, r0:r0 + CHUNK, :].T
        for h in range(4):
            vat_s[h, dst] = jnp.concatenate([vt[HEAD_DIM * h:HEAD_DIM * (h + 1)], ones_blk], 0).astype(BF16)
        g = sm_ref[0, r0:r0 + CHUNK, :]
        lf = pltpu.roll(jax.nn.log_sigmoid(g), LANE - 4, 1)
        gcol_s[dst] = g - jnp.where(lane128 < 8, _dot_exact_l(tri, lf), _dot_exact_l(triu, lf))
        gt = g.T[0:16]
        lft = pltpu.roll(jax.nn.log_sigmoid(gt), 12, 0)
        brow = jnp.where(fwd_rows, _dot_exact_r(lft, triu), _dot_exact_r(lft, tri))
        brow_s[dst] = brow
        grow_s[dst] = gt - brow

    for ci in range(nc):
        prep(qkc_ref, vc_ref, smc_ref, ci, nc, ci, False)
    for ci in range(n - nc):
        prep(qkl_ref, vl_ref, sml_ref, ci, n - nc, nc + ci, True)
    g_all = grow_s[...].reshape(n * 16, CHUNK)
    fwd_all = (lax.broadcasted_iota(jnp.int32, (n * 16, 1), 0) % 16) < 8
    cm_s[...] = jnp.where(fwd_all, cummax_lanes(g_all, False), cummax_lanes(g_all, True)).reshape(n, 16, CHUNK)

    def scan_dir(bwd):
        ck_s[...] = jnp.zeros_like(ck_s)
        gi = 8 if bwd else 0
        causal = low if bwd else upp

        def chunk_body(c, ms):
            ch = _bwd_chunk(c, nc, n) if bwd else c
            qt = qt_s[ch]
            kb = k_s[ch]
            gcol_all = gcol_s[ch]
            ck_b = ck_s[...].astype(BF16)
            new_ms = []
            qtms = [jnp.where(hrows[h], qt, jnp.zeros_like(qt)) for h in range(4)]
            scores = [_dot(kb, qtms[h]) for h in range(4)]
            inters = [_dot(ck_b, qtms[h]) for h in range(4)]
            for h in range(4):
                m = ms[h]
                g_row = grow_s[ch, gi + h:gi + h + 1, :]
                b_row = brow_s[ch, gi + h:gi + h + 1, :]
                a_row = jnp.maximum(m, cm_s[ch, gi + h:gi + h + 1, :])
                w = jnp.exp(jnp.where(causal, gcol_all[:, gi + h:gi + h + 1] - a_row, NEG))
                pt = (scores[h] * w).astype(BF16)
                vat = vat_s[h, ch]
                nd = _dot(vat, pt) + jnp.exp(m - a_row) * inters[h]
                den = nd[HEAD_DIM:HEAD_DIM + 1, :]
                ht = nd[0:HEAD_DIM] / jnp.maximum(jnp.abs(den), jnp.exp(-(b_row + a_row)))
                rows = slice(HEAD_DIM * h, HEAD_DIM * (h + 1))
                if bwd:
                    ot_s[ch, rows, :] = ot_s[ch, rows, :] + ht
                else:
                    ot_s[ch, rows, :] = ht
                bl = b_row[:, 0:1] if bwd else b_row[:, CHUNK - 1:CHUNK]
                lw_end = bl + g_row
                m_new = jnp.maximum(bl + m, lw_end.max(-1, keepdims=True))
                upd = _dot((vat * jnp.exp(lw_end - m_new)).astype(BF16), kb)
                ck_s[...] = jnp.where(hmask[h], jnp.exp(bl + m - m_new) * ck_s[...] + upd, ck_s[...])
                new_ms.append(m_new)
            return tuple(new_ms)

        lax.fori_loop(0, n, chunk_body, tuple(jnp.zeros((1, 1), F32) for _ in range(4)))

    scan_dir(False)
    scan_dir(True)
    for ci in range(nc):
        oc_ref[0, ci * CHUNK:(ci + 1) * CHUNK, :] = ot_s[ci].T
    for ci in range(n - nc):
        ol_ref[0, ci * CHUNK:(ci + 1) * CHUNK, :] = ot_s[nc + ci].T


def _mlstm_call(p_lat, p_ctx, cos, sin, conv_w_l):
    B, T, _ = p_lat.shape
    Tc = p_ctx.shape[1]
    nc, n = Tc // CHUNK, (Tc + T) // CHUNK
    cb = lambda base, w: (lambda b: (b, 0, base // w))
    return pl.pallas_call(
        functools.partial(_mlstm_kernel, nc=nc, n=n),
        grid=(B,),
        in_specs=[pl.BlockSpec((1, T, 512), cb(C_MQK, 512)),
                  pl.BlockSpec((1, T, 256), cb(C_MV, 256)),
                  pl.BlockSpec((1, T, LANE), cb(C_SM, LANE)),
                  pl.BlockSpec((1, Tc, 512), cb(C_MQK, 512)),
                  pl.BlockSpec((1, Tc, 256), cb(C_MV, 256)),
                  pl.BlockSpec((1, Tc, LANE), cb(C_SM, LANE)),
                  pl.BlockSpec((T, 256), lambda b: (0, 0)),
                  pl.BlockSpec((T, 256), lambda b: (0, 0)),
                  pl.BlockSpec((CONV_K, 512), lambda b: (0, 0))],
        out_specs=[pl.BlockSpec((1, T, 256), lambda b: (b, 0, 0)),
                   pl.BlockSpec((1, Tc, 256), lambda b: (b, 0, 0))],
        out_shape=[jax.ShapeDtypeStruct((B, T, 256), F32),
                   jax.ShapeDtypeStruct((B, Tc, 256), F32)],
        scratch_shapes=[pltpu.VMEM((n, 256, CHUNK), BF16),
                        pltpu.VMEM((n, CHUNK, 256), BF16),
                        pltpu.VMEM((4, n, LANE, CHUNK), BF16),
                        pltpu.VMEM((n, CHUNK, LANE), F32),
                        pltpu.VMEM((n, 16, CHUNK), F32),
                        pltpu.VMEM((n, 16, CHUNK), F32),
                        pltpu.VMEM((n, 16, CHUNK), F32),
                        pltpu.VMEM((n, 256, CHUNK), F32),
                        pltpu.VMEM((LANE, 256), F32)],
        compiler_params=_cparams(("arbitrary",)),
        name="mlstm",
    )(p_lat, p_lat, p_lat, p_ctx, p_ctx, p_ctx, cos, sin, conv_w_l)


GLA_BLK = 64
GLA_CLAMP = 80.0


def _gla_kernel(ql_ref, kl_ref, vl_ref, sml_ref, qc_ref, kc_ref, vc_ref, smc_ref, cos_ref, sin_ref, w2_ref, b2_ref,
                ol_ref, oc_ref,
                q_s, k_s, v_s, a_s, b_s, o_s, st_s, *, nc, n):
    low, upp = _tri_masks()
    tri = jnp.where(low, 1.0, 0.0).astype(BF16)
    triu = jnp.where(upp, 1.0, 0.0).astype(BF16)
    lane128 = lax.broadcasted_iota(jnp.int32, (1, LANE), 1)
    lane256 = lax.broadcasted_iota(jnp.int32, (1, 2 * LANE), 1)
    first = (lane128 % 16) < 8
    hm128 = [(lane128 >= 32 * h) & (lane128 < 32 * (h + 1)) for h in range(4)]
    hm256 = [(lane256 >= 64 * h) & (lane256 < 64 * (h + 1)) for h in range(4)]
    nb = CHUNK // GLA_BLK
    r_st = lax.broadcasted_iota(jnp.int32, (nb * GLA_BLK, CHUNK), 0)
    c_st = lax.broadcasted_iota(jnp.int32, (nb * GLA_BLK, CHUNK), 1)
    bd_r = lax.broadcasted_iota(jnp.int32, (LANE, 2 * LANE), 0)
    bd_c = lax.broadcasted_iota(jnp.int32, (LANE, 2 * LANE), 1)
    blockdiag = (bd_r // 32) == (bd_c // 64)

    def prep(q_ref, k_ref, v_ref, sm_ref, ci, dst, use_rope):
        r0 = ci * CHUNK
        q = q_ref[0, r0:r0 + CHUNK, :] * 32 ** -0.5
        k = k_ref[0, r0:r0 + CHUNK, :]
        if use_rope:
            cs = cos_ref[r0:r0 + CHUNK, :]
            sn = sin_ref[r0:r0 + CHUNK, :]
            q = _rope(q, cs, sn, first, 8)
            k = _rope(k, cs, sn, first, 8)
        q_s[dst] = q
        k_s[dst] = k
        v_s[dst] = v_ref[0, r0:r0 + CHUNK, :].astype(BF16)
        lr = sm_ref[0, r0:r0 + CHUNK, :].astype(BF16)
        for d in range(2):
            a = jax.nn.log_sigmoid(_dot(lr, w2_ref[d]) + b2_ref[d]) / GLA_TAU
            a_s[d, dst] = a
            b_s[d, dst] = _dot_exact_l(triu if d else tri, a)

    for ci in range(nc):
        prep(qc_ref, kc_ref, vc_ref, smc_ref, ci, ci, False)
    for ci in range(n - nc):
        prep(ql_ref, kl_ref, vl_ref, sml_ref, ci, nc + ci, True)

    def scan_dir(bwd):
        d = 1 if bwd else 0
        st_s[...] = jnp.zeros_like(st_s)

        def chunk_body(c, carry):
            ch = _bwd_chunk(c, nc, n) if bwd else c
            q = q_s[ch]
            k = k_s[ch]
            vb = v_s[ch]
            a = a_s[d, ch]
            b = b_s[d, ch]
            st_b = st_s[...].astype(BF16)
            o_inter = _dot((q * jnp.exp(b)).astype(BF16), st_b)
            for i in range(nb):
                rows = slice(i * GLA_BLK, (i + 1) * GLA_BLK)
                e = (i + 1) * GLA_BLK - 1 if bwd else i * GLA_BLK
                ref = b[e:e + 1, :] - a[e:e + 1, :]
                qs = q[rows] * jnp.exp(b[rows] - ref)
                ks = (k * jnp.exp(jnp.minimum(ref - b, GLA_CLAMP))).astype(BF16)
                lhs = jnp.concatenate([jnp.where(hm128[h], qs, 0.0) for h in range(4)], 0).astype(BF16)
                att = _dot_nt(lhs, ks)
                t_idx = (r_st % GLA_BLK) + i * GLA_BLK
                ok = (c_st >= t_idx) if bwd else (c_st <= t_idx)
                att = jnp.where(ok, att, 0.0).astype(BF16)
                oh = _dot(att, vb)
                o_blk = o_inter[rows]
                for h in range(4):
                    o_blk = o_blk + jnp.where(hm256[h], oh[h * GLA_BLK:(h + 1) * GLA_BLK], 0.0)
                if bwd:
                    o_s[ch, rows, :] = o_s[ch, rows, :] + o_blk
                else:
                    o_s[ch, rows, :] = o_blk
            bt = b.T
            tot = bt[:, 0:1] if bwd else bt[:, CHUNK - 1:CHUNK]
            kt = (k.T * jnp.exp(tot - bt)).astype(BF16)
            upd = jnp.where(blockdiag, _dot(kt, vb), 0.0)
            st_s[...] = jnp.exp(tot) * st_s[...] + upd
            return carry

        lax.fori_loop(0, n, chunk_body, 0)

    scan_dir(False)
    scan_dir(True)
    for ci in range(nc):
        oc_ref[0, ci * CHUNK:(ci + 1) * CHUNK, :] = o_s[ci]
    for ci in range(n - nc):
        ol_ref[0, ci * CHUNK:(ci + 1) * CHUNK, :] = o_s[nc + ci]


def _gla_call(p_lat, p_ctx, cos, sin, w2p, b2p):
    B, T, _ = p_lat.shape
    Tc = p_ctx.shape[1]
    nc, n = Tc // CHUNK, (Tc + T) // CHUNK
    cb = lambda base, w: (lambda b: (b, 0, base // w))
    return pl.pallas_call(
        functools.partial(_gla_kernel, nc=nc, n=n),
        grid=(B,),
        in_specs=[pl.BlockSpec((1, T, LANE), cb(C_GQ, LANE)),
                  pl.BlockSpec((1, T, LANE), cb(C_GK, LANE)),
                  pl.BlockSpec((1, T, 256), cb(C_GV, 256)),
                  pl.BlockSpec((1, T, LANE), cb(C_SM, LANE)),
                  pl.BlockSpec((1, Tc, LANE), cb(C_GQ, LANE)),
                  pl.BlockSpec((1, Tc, LANE), cb(C_GK, LANE)),
                  pl.BlockSpec((1, Tc, 256), cb(C_GV, 256)),
                  pl.BlockSpec((1, Tc, LANE), cb(C_SM, LANE)),
                  pl.BlockSpec((T, LANE), lambda b: (0, 0)),
                  pl.BlockSpec((T, LANE), lambda b: (0, 0)),
                  pl.BlockSpec((2, LANE, LANE), lambda b: (0, 0, 0)),
                  pl.BlockSpec((2, 1, LANE), lambda b: (0, 0, 0))],
        out_specs=[pl.BlockSpec((1, T, 256), lambda b: (b, 0, 0)),
                   pl.BlockSpec((1, Tc, 256), lambda b: (b, 0, 0))],
        out_shape=[jax.ShapeDtypeStruct((B, T, 256), F32),
                   jax.ShapeDtypeStruct((B, Tc, 256), F32)],
        scratch_shapes=[pltpu.VMEM((n, CHUNK, LANE), F32),
                        pltpu.VMEM((n, CHUNK, LANE), F32),
                        pltpu.VMEM((n, CHUNK, 256), BF16),
                        pltpu.VMEM((2, n, CHUNK, LANE), F32),
                        pltpu.VMEM((2, n, CHUNK, LANE), F32),
                        pltpu.VMEM((n, CHUNK, 256), F32),
                        pltpu.VMEM((LANE, 256), F32)],
        compiler_params=_cparams(("arbitrary",)),
        name="gla",
    )(p_lat, p_lat, p_lat, p_lat, p_ctx, p_ctx, p_ctx, p_ctx, cos, sin, w2p, b2p)


def _gla_gate_weights(gla_w2, gla_b2):
    w = jnp.zeros((DEPTH, 2, LANE, LANE), F32)
    w = w.at[:, 0, 16:32].set(gla_w2[:, 0]).at[:, 1, 32:48].set(gla_w2[:, 1])
    return w.astype(BF16), gla_b2.astype(F32).reshape(DEPTH, 2, 1, LANE)


TILE_ROWS = D // LANE


def _store_token_tiles(ref, val):
    tm = val.shape[0]
    for s in range(TILE_ROWS):
        ref[0, pl.ds(s, tm, stride=TILE_ROWS), :] = val[:, s * LANE:(s + 1) * LANE]


def _load_token_tiles(ref, lead, t0, tm):
    return [ref[lead + (pl.ds(t0 * TILE_ROWS + s, tm, stride=TILE_ROWS), slice(None))] for s in range(TILE_ROWS)]


def _layer_norm(z, g, b):
    mu = z.mean(-1, keepdims=True)
    zc = z - mu
    var = jnp.mean(jnp.square(zc), -1, keepdims=True)
    return zc * lax.rsqrt(var + LN_EPS) * g + b


def _merge_kernel(hm_ref, hg_ref, hn_ref, mo_ref, gr_ref, x_ref, mod_ref, wout_ref, ng_ref, ln_ref, x1_ref, u2_ref):
    r = lax.broadcasted_iota(jnp.int32, (256, 256), 0)
    c = lax.broadcasted_iota(jnp.int32, (256, 256), 1)
    avg = jnp.where((r // HEAD_DIM) == (c // HEAD_DIM), 1.0 / HEAD_DIM, 0.0).astype(BF16)

    def seg_mean(x):
        hi = x.astype(BF16)
        lo = (x - hi.astype(F32)).astype(BF16)
        return _dot(hi, avg) + _dot(lo, avg)

    def head_norm(h):
        d = h - seg_mean(h)
        return d * lax.rsqrt(seg_mean(d * d) + LN_EPS)

    ym = head_norm(hm_ref[0]) * ng_ref[0:1, :] * jax.nn.sigmoid(mo_ref[0])
    yg = head_norm(hg_ref[0]) * ng_ref[1:2, :] * jax.nn.silu(gr_ref[0])
    y = (_dot(ym.astype(BF16), wout_ref[0:256, :]) + _dot(yg.astype(BF16), wout_ref[256:512, :])
         + _dot(hn_ref[0].astype(BF16), wout_ref[512:1024, :]))
    m = mod_ref[0]
    x1 = _layer_norm(DEEPNORM_ALPHA * x_ref[0] + m[:, 2 * D:3 * D] * y, ln_ref[0:1, :], ln_ref[1:2, :])
    x1_ref[0] = x1
    _store_token_tiles(u2_ref, x1 * (1.0 + m[:, 4 * D:5 * D]) + m[:, 3 * D:4 * D])


def _merge_call(hm, hg, hn, p, x, mod3, wout_b, ng, ln, shared_row):
    B, T, _ = x.shape
    tm = 256
    mod_map = (lambda b, i: (b, 0, 0)) if shared_row is None else (lambda b, i: (shared_row, 0, 0))
    tok = lambda w: pl.BlockSpec((1, tm, w), lambda b, i: (b, i, 0))
    return pl.pallas_call(
        _merge_kernel,
        grid=(B, T // tm),
        in_specs=[tok(256), tok(256), tok(512),
                  pl.BlockSpec((1, tm, 256), lambda b, i: (b, i, C_MO // 256)),
                  pl.BlockSpec((1, tm, 256), lambda b, i: (b, i, C_GR // 256)),
                  tok(D),
                  pl.BlockSpec((1, 1, 6 * D), mod_map),
                  pl.BlockSpec((D, D), lambda b, i: (0, 0)),
                  pl.BlockSpec((2, 256), lambda b, i: (0, 0)),
                  pl.BlockSpec((2, D), lambda b, i: (0, 0))],
        out_specs=[tok(D), pl.BlockSpec((1, tm * TILE_ROWS, LANE), lambda b, i: (b, i, 0))],
        out_shape=[jax.ShapeDtypeStruct((B, T, D), F32), jax.ShapeDtypeStruct((B, T * TILE_ROWS, LANE), F32)],
        compiler_params=_cparams(("arbitrary", "arbitrary")),
        name="merge",
    )(hm, hg, hn, p, p, x, mod3, wout_b, ng, ln)


ROUTER_SAMPLES = 2


def _router_kernel(u_ref, wr_ref, o_ref, aff_s, sp_s, *, T, cap):
    J = T // LANE
    E = N_EXPERTS
    NS = ROUTER_SAMPLES
    wrt = wr_ref[...]
    keys = [[] for _ in range(NS)]
    for j in range(J):
        for si in range(NS):
            uj = jnp.concatenate(_load_token_tiles(u_ref, (si,), j * LANE, LANE), -1).astype(BF16)
            lg = _dot_nt(wrt, uj)
            ex = jnp.exp(lg - lg.max(0, keepdims=True))
            aff = ex / ex.sum(0, keepdims=True)
            aff_s[si, j * E:(j + 1) * E, :] = aff
            keys[si].append(pltpu.bitcast(aff, jnp.int32))

    def count(ks, pred):
        cnt = None
        for k in ks:
            cj = jnp.where(pred(k), 1.0, 0.0)
            cnt = cj if cnt is None else cnt + cj
        return cnt.sum(-1, keepdims=True)

    thrs = [jnp.zeros((E, 1), jnp.int32) for _ in range(NS)]
    for bit in range(30, -1, -1):
        for si in range(NS):
            cand = thrs[si] | (1 << bit)
            thrs[si] = jnp.where(count(keys[si], lambda k: k >= cand) >= cap, cand, thrs[si])

    r = lax.broadcasted_iota(jnp.int32, (LANE, LANE), 0)
    c = lax.broadcasted_iota(jnp.int32, (LANE, LANE), 1)
    upper = jnp.where(r <= c, 1.0, 0.0).astype(BF16)
    ones = jnp.ones((LANE, LANE), BF16)
    rr = lax.broadcasted_iota(jnp.int32, (J * E, J * E), 0)
    cc = lax.broadcasted_iota(jnp.int32, (J * E, J * E), 1)
    earlier = jnp.where(((rr % E) == (cc % E)) & ((cc // E) < (rr // E)), 1.0, 0.0).astype(BF16)

    def prefix(x01):
        xb = x01.astype(BF16)
        return _dot(xb, upper) + _dot(earlier, _dot(xb, ones).astype(BF16))

    for si in range(NS):
        thr = thrs[si]
        need = cap - count(keys[si], lambda k: k > thr)
        gt = jnp.concatenate([jnp.where(k > thr, 1.0, 0.0) for k in keys[si]], 0)
        eq = jnp.concatenate([jnp.where(k == thr, 1.0, 0.0) for k in keys[si]], 0)
        need_t = jnp.concatenate([need] * J, 0)
        sel = jnp.maximum(gt, jnp.where(prefix(eq) <= need_t, eq, 0.0))
        sp_s[si] = jnp.where(sel > 0.0, prefix(sel) - 1.0, -1.0)

    sb = min(cap, LANE)
    lane = lax.broadcasted_iota(jnp.int32, (1, LANE), 1)
    o_ref[...] = jnp.zeros_like(o_ref)
    for e in range(E):
        for half in range(cap // sb):
            slot = (lax.broadcasted_iota(jnp.int32, (sb, LANE), 0) + half * sb).astype(F32)
            rows = slice(half * sb, (half + 1) * sb)
            for si in range(NS):
                def jbody(j, acc):
                    acc_i, acc_g = acc
                    sp = sp_s[si, pl.ds(j * E + e, 1), :]
                    af = aff_s[si, pl.ds(j * E + e, 1), :]
                    hit = sp == slot
                    tid = (lane + j * LANE).astype(F32)
                    return jnp.where(hit, tid, acc_i), jnp.where(hit, af, acc_g)

                acc_i, acc_g = lax.fori_loop(0, J, jbody, (jnp.zeros((sb, LANE), F32), jnp.zeros((sb, LANE), F32)),
                                             unroll=2)
                icol = acc_i.sum(-1, keepdims=True)
                gcol = acc_g.sum(-1, keepdims=True)
                o_ref[si, rows, :] = jnp.where(lane == e, icol, jnp.where(lane == E + e, gcol, o_ref[si, rows, :]))


def _router_call(u2t, wrt_b):
    B = u2t.shape[0]
    T = u2t.shape[1] // TILE_ROWS
    cap = CAPACITY_FACTOR * T // N_EXPERTS
    J = T // LANE
    ns = ROUTER_SAMPLES
    assert B % ns == 0
    out = pl.pallas_call(
        functools.partial(_router_kernel, T=T, cap=cap),
        grid=(B // ns,),
        in_specs=[pl.BlockSpec((ns, T * TILE_ROWS, LANE), lambda b: (b, 0, 0)),
                  pl.BlockSpec((N_EXPERTS, D), lambda b: (0, 0))],
        out_specs=pl.BlockSpec((ns, cap, LANE), lambda b: (b, 0, 0)),
        out_shape=jax.ShapeDtypeStruct((B, cap, LANE), F32),
        scratch_shapes=[pltpu.VMEM((ns, J * N_EXPERTS, LANE), F32), pltpu.VMEM((ns, J * N_EXPERTS, LANE), F32)],
        compiler_params=_cparams(("arbitrary",)),
        name="router",
    )(u2t, wrt_b)
    idx = out[:, :, :N_EXPERTS].astype(jnp.int32).transpose(0, 2, 1).reshape(B, 1, N_EXPERTS * cap)
    gate = out[:, :, N_EXPERTS:2 * N_EXPERTS].transpose(0, 2, 1).reshape(B, 1, N_EXPERTS * cap)
    return idx, gate


def _gather_kernel(idx_ref, u_ref, o_ref, *, cap, eg):
    g = pl.program_id(1)
    for e in range(eg):
        def body(s, carry):
            r = idx_ref[0, 0, (g * eg + e) * cap + s]
            src = pl.ds(pl.multiple_of(r * TILE_ROWS, TILE_ROWS), TILE_ROWS)
            o_ref[e, 0, pl.ds(pl.multiple_of(s * TILE_ROWS, TILE_ROWS), TILE_ROWS), :] = u_ref[0, src, :]
            return carry
        lax.fori_loop(0, cap, body, 0, unroll=8)


def _gather_call(idx, u2t):
    B = u2t.shape[0]
    T = u2t.shape[1] // TILE_ROWS
    cap = CAPACITY_FACTOR * T // N_EXPERTS
    eg = 4
    smem = lambda: pl.BlockSpec((1, 1, N_EXPERTS * cap), lambda b, g: (b, 0, 0), memory_space=pltpu.SMEM)
    return pl.pallas_call(
        functools.partial(_gather_kernel, cap=cap, eg=eg),
        grid=(B, N_EXPERTS // eg),
        in_specs=[smem(), pl.BlockSpec((1, T * TILE_ROWS, LANE), lambda b, g: (b, 0, 0))],
        out_specs=pl.BlockSpec((eg, 1, cap * TILE_ROWS, LANE), lambda b, g: (g, b, 0, 0)),
        out_shape=jax.ShapeDtypeStruct((N_EXPERTS, B, cap * TILE_ROWS, LANE), F32),
        compiler_params=_cparams(("arbitrary", "arbitrary")),
        name="gather",
    )(idx, u2t)


FF_CHUNK = 512


def _ffn_kernel(x_ref, wg_ref, wu_ref, wd_ref, o_ref, xs, *, tm):
    for s, piece in enumerate(_load_token_tiles(x_ref, (0,), 0, tm)):
        xs[:, s * LANE:(s + 1) * LANE] = piece.astype(BF16)
    xb = xs[...]
    acc = None
    for c in range(EXPERT_FF // FF_CHUNK):
        cols = slice(c * FF_CHUNK, (c + 1) * FF_CHUNK)
        h = (jax.nn.silu(_dot(xb, wg_ref[0, :, cols])) * _dot(xb, wu_ref[0, :, cols])).astype(BF16)
        t = _dot(h, wd_ref[0, cols, :])
        acc = t if acc is None else acc + t
    _store_token_tiles(o_ref, acc)


def _ffn_call(xe, wg_b, wu_b, wd_b, layer):
    E = xe.shape[0]
    M = xe.shape[1] // TILE_ROWS
    tm = min(M, 512)
    w_map = lambda e, i: (layer * E + e, 0, 0)
    return pl.pallas_call(
        functools.partial(_ffn_kernel, tm=tm),
        grid=(E, M // tm),
        in_specs=[pl.BlockSpec((1, tm * TILE_ROWS, LANE), lambda e, i: (e, i, 0)),
                  pl.BlockSpec((1, D, EXPERT_FF), w_map),
                  pl.BlockSpec((1, D, EXPERT_FF), w_map),
                  pl.BlockSpec((1, EXPERT_FF, D), w_map)],
        out_specs=pl.BlockSpec((1, tm * TILE_ROWS, LANE), lambda e, i: (e, i, 0)),
        out_shape=jax.ShapeDtypeStruct((E, M * TILE_ROWS, LANE), F32),
        scratch_shapes=[pltpu.VMEM((tm, D), BF16)],
        compiler_params=_cparams(("arbitrary", "arbitrary")),
        name="ffn",
    )(xe, wg_b, wu_b, wd_b)


SCATTER_FIN_ROWS = 256


def _scatter_kernel(idx_ref, gate_ref, y_ref, x1_ref, mod_ref, ln_ref, o_ref, acc_a, acc_b, *, cap, n_g, T):
    g = pl.program_id(1)

    @pl.when(g == 0)
    def _():
        acc_a[...] = jnp.zeros_like(acc_a)
        acc_b[...] = jnp.zeros_like(acc_b)

    def body(s, carry):
        p0 = (g * 2) * cap + s
        p1 = p0 + cap
        src = pl.ds(pl.multiple_of(s * TILE_ROWS, TILE_ROWS), TILE_ROWS)
        d0 = pl.ds(pl.multiple_of(idx_ref[0, 0, p0] * TILE_ROWS, TILE_ROWS), TILE_ROWS)
        d1 = pl.ds(pl.multiple_of(idx_ref[0, 0, p1] * TILE_ROWS, TILE_ROWS), TILE_ROWS)
        acc_a[d0, :] = acc_a[d0, :] + y_ref[0, 0, src, :] * gate_ref[0, 0, p0]
        acc_b[d1, :] = acc_b[d1, :] + y_ref[1, 0, src, :] * gate_ref[0, 0, p1]
        return carry
    lax.fori_loop(0, cap, body, 0, unroll=8)

    @pl.when(g == n_g - 1)
    def _():
        g2 = mod_ref[0][:, 5 * D:6 * D]
        tb = min(T, SCATTER_FIN_ROWS)
        for i in range(T // tb):
            rows = slice(i * tb, (i + 1) * tb)
            f = jnp.concatenate([pa + pb for pa, pb in zip(_load_token_tiles(acc_a, (), i * tb, tb),
                                                             _load_token_tiles(acc_b, (), i * tb, tb))], -1)
            z = DEEPNORM_ALPHA * x1_ref[0, rows, :] + g2 * f
            o_ref[0, rows, :] = _layer_norm(z, ln_ref[0:1, :], ln_ref[1:2, :])


def _scatter_call(idx, gate, y4, x1, mod3, ln, shared_row):
    B, T, _ = x1.shape
    cap = CAPACITY_FACTOR * T // N_EXPERTS
    eg = 2
    n_g = N_EXPERTS // eg
    mod_map = (lambda b, g: (b, 0, 0)) if shared_row is None else (lambda b, g: (shared_row, 0, 0))
    smem = lambda: pl.BlockSpec((1, 1, N_EXPERTS * cap), lambda b, g: (b, 0, 0), memory_space=pltpu.SMEM)
    return pl.pallas_call(
        functools.partial(_scatter_kernel, cap=cap, n_g=n_g, T=T),
        grid=(B, n_g),
        scratch_shapes=[pltpu.VMEM((T * TILE_ROWS, LANE), F32), pltpu.VMEM((T * TILE_ROWS, LANE), F32)],
        in_specs=[smem(), smem(),
                  pl.BlockSpec((eg, 1, cap * TILE_ROWS, LANE), lambda b, g: (g, b, 0, 0)),
                  pl.BlockSpec((1, T, D), lambda b, g: (b, 0, 0), pipeline_mode=pl.Buffered(1)),
                  pl.BlockSpec((1, 1, 6 * D), mod_map),
                  pl.BlockSpec((2, D), lambda b, g: (0, 0))],
        out_specs=pl.BlockSpec((1, T, D), lambda b, g: (b, 0, 0)),
        out_shape=jax.ShapeDtypeStruct((B, T, D), F32),
        compiler_params=_cparams(("arbitrary", "arbitrary")),
        name="scatter",
    )(idx, gate, y4, x1, mod3, ln)


def _moe(x1, u2t, mod3, wrt_b, wg_b, wu_b, wd_b, layer, ln2, shared_row):
    B, T, _ = x1.shape
    cap = CAPACITY_FACTOR * T // N_EXPERTS
    idx, gate = _router_call(u2t, wrt_b)
    xe = _gather_call(idx, u2t)
    y = _ffn_call(xe.reshape(N_EXPERTS, B * cap * TILE_ROWS, LANE), wg_b, wu_b, wd_b, layer)
    return _scatter_call(idx, gate, y.reshape(N_EXPERTS, B, cap * TILE_ROWS, LANE), x1, mod3, ln2, shared_row)


def kernel(x, c, ctx, c_ctx, w_mod, b_mod, w_in, b_in, conv_w, gla_w2, gla_b2, mlstm_norm_g, gla_norm_g, rpb, w_out, ln1_g, ln1_b, w_router, w_gate, w_up, w_down, ln2_g, ln2_b):
    B, T, _ = x.shape
    n_mod = -(-(B + 1) // 8) * 8
    c_all = jnp.concatenate([c, c_ctx[None], jnp.zeros((n_mod - B - 1, D), F32)], 0)
    mods = _mod_call(c_all, w_mod, b_mod)
    w_p = _repack_columns(w_in).astype(BF16)
    b_p = _repack_columns(b_in)
    bias = _natten_bias_tables(rpb)
    mcos, msin = _rope_tables(T, 4, 16)
    gcos, gsin = _rope_tables(T, 4, 8)
    w2p, b2p = _gla_gate_weights(gla_w2, gla_b2)
    wout_b = w_out.astype(BF16)
    wrt_b = jnp.swapaxes(w_router, 1, 2).astype(BF16)
    wg_b = w_gate.astype(BF16).reshape(DEPTH * N_EXPERTS, D, EXPERT_FF)
    wu_b = w_up.astype(BF16).reshape(DEPTH * N_EXPERTS, D, EXPERT_FF)
    wd_b = w_down.astype(BF16).reshape(DEPTH * N_EXPERTS, EXPERT_FF, D)
    ng = jnp.stack([mlstm_norm_g, gla_norm_g], 1)
    ln1 = jnp.stack([ln1_g, ln1_b], 1)
    ln2 = jnp.stack([ln2_g, ln2_b], 1)
    for l in range(DEPTH):
        mod3 = mods[l].reshape(n_mod, 1, 6 * D)
        p_lat = _inproj_call(x, mod3, w_p[l], b_p[l][None], None)
        p_ctx = _inproj_call(ctx, mod3, w_p[l], b_p[l][None], B)
        n_lat, n_ctx = _natten_call(p_lat, p_ctx, bias[l])
        m_lat, m_ctx = _mlstm_call(p_lat, p_ctx, mcos, msin, conv_w[l])
        g_lat, g_ctx = _gla_call(p_lat, p_ctx, gcos, gsin, w2p[l], b2p[l])
        x1, u2 = _merge_call(m_lat, g_lat, n_lat, p_lat, x, mod3, wout_b[l], ng[l], ln1[l], None)
        x = _moe(x1, u2, mod3, wrt_b[l], wg_b, wu_b, wd_b, l, ln2[l], None)
        if l < DEPTH - 1:
            c1, u2c = _merge_call(m_ctx, g_ctx, n_ctx, p_ctx, ctx, mod3, wout_b[l], ng[l], ln1[l], B)
            ctx = _moe(c1, u2c, mod3, wrt_b[l], wg_b, wu_b, wd_b, l, ln2[l], B)
    return x
```

```python
import functools

import numpy as np
import jax
import jax.numpy as jnp
from jax import lax
from jax.experimental import pallas as pl
from jax.experimental.pallas import tpu as pltpu

F32 = jnp.float32
BF16 = jnp.bfloat16

D = 1024
DEPTH = 4
GRID_W = 64
HEAD_DIM = 64
WIN_ROWS = 8
WIN_COLS = 16
CONV_K = 3
ROPE_BASE = 10000.0
N_EXPERTS = 16
EXPERT_FF = 2 * D
CAPACITY_FACTOR = 2
LN_EPS = 1e-5
GLA_TAU = 16.0
GLA_RANK = 16
DEEPNORM_ALPHA = (2 * DEPTH) ** 0.25
NEG = -1e30

VMEM_LIMIT = 56 * 1024 * 1024
LANE = 128
CHUNK = 256

C_MQK, C_NQ, C_NK, C_NV = 0, 512, 1024, 1536
C_MV, C_MO, C_GV, C_GR = 2048, 2304, 2560, 2816
C_GQ, C_GK, C_SM = 3072, 3200, 3328
PW = 3456
_ORIG = dict(m_q=0, m_k=256, m_v=512, m_o=768, m_g=1024, g_q=1040, g_k=1168, g_v=1296, g_r=1552,
             g_lr=1808, n_q=1840, n_k=2352, n_v=2864)
_PERM = np.concatenate([
    np.arange(0, 512), np.arange(1840, 2352), np.arange(2352, 2864), np.arange(2864, 3376),
    np.arange(512, 768), np.arange(768, 1024), np.arange(1296, 1552), np.arange(1552, 1808),
    np.arange(1040, 1168), np.arange(1168, 1296), np.arange(1024, 1040), np.arange(1808, 1840)])
_NPAD = PW - _PERM.size
_SEGMENTS = ((0, 512), (1840, 2352), (2352, 2864), (2864, 3376), (512, 768), (768, 1024), (1296, 1552),
             (1552, 1808), (1040, 1168), (1168, 1296), (1024, 1040), (1808, 1840))


def _repack_columns(w):
    parts = [w[..., a:b] for a, b in _SEGMENTS]
    return jnp.concatenate(parts + [jnp.zeros(w.shape[:-1] + (_NPAD,), w.dtype)], -1)


def _cparams(sem):
    return pltpu.CompilerParams(dimension_semantics=sem, vmem_limit_bytes=VMEM_LIMIT)


def _dot(a, b):
    return jnp.dot(a, b, preferred_element_type=F32)


def _dot_nt(a, b):
    return lax.dot_general(a, b, (((1,), (1,)), ((), ())), preferred_element_type=F32)


def _split3(x):
    hi = x.astype(BF16)
    r1 = x - hi.astype(F32)
    mid = r1.astype(BF16)
    lo = (r1 - mid.astype(F32)).astype(BF16)
    return hi, mid, lo


def _dot_exact_l(m01, x):
    hi, mid, lo = _split3(x)
    return _dot(m01, hi) + _dot(m01, mid) + _dot(m01, lo)


def _dot_exact_r(x, m01):
    hi, mid, lo = _split3(x)
    return _dot(hi, m01) + _dot(mid, m01) + _dot(lo, m01)


def _mod_kernel(c_ref, w_ref, b_ref, o_ref):
    s = jax.nn.silu(c_ref[...]).astype(BF16)
    o_ref[0] = _dot(s, w_ref[0].astype(BF16)) + b_ref[0]


def _mod_call(c_all, w_mod, b_mod):
    rows = c_all.shape[0]
    tn = 1536
    return pl.pallas_call(
        _mod_kernel,
        grid=(DEPTH, 6 * D // tn),
        in_specs=[pl.BlockSpec((rows, D), lambda l, j: (0, 0)),
                  pl.BlockSpec((1, D, tn), lambda l, j: (l, 0, j)),
                  pl.BlockSpec((1, 1, tn), lambda l, j: (l, 0, j))],
        out_specs=pl.BlockSpec((1, rows, tn), lambda l, j: (l, 0, j)),
        out_shape=jax.ShapeDtypeStruct((DEPTH, rows, 6 * D), F32),
        compiler_params=_cparams(("arbitrary", "arbitrary")),
        name="mod",
    )(c_all, w_mod, b_mod.reshape(DEPTH, 1, 6 * D))


def _inproj_kernel(x_ref, mod_ref, w_ref, b_ref, o_ref):
    m = mod_ref[0]
    u = x_ref[0] * (1.0 + m[:, D:2 * D]) + m[:, 0:D]
    o_ref[0] = _dot(u.astype(BF16), w_ref[...]) + b_ref[...]


def _inproj_call(x, mod3, w_p, b_p, shared_row):
    B, T, _ = x.shape
    tm = min(T, 512)
    if shared_row is None:
        mod_map = lambda b, i: (b, 0, 0)
    else:
        mod_map = lambda b, i: (shared_row, 0, 0)
    return pl.pallas_call(
        _inproj_kernel,
        grid=(B, T // tm),
        in_specs=[pl.BlockSpec((1, tm, D), lambda b, i: (b, i, 0)),
                  pl.BlockSpec((1, 1, 6 * D), mod_map),
                  pl.BlockSpec((D, PW), lambda b, i: (0, 0)),
                  pl.BlockSpec((1, PW), lambda b, i: (0, 0))],
        out_specs=pl.BlockSpec((1, tm, PW), lambda b, i: (b, i, 0)),
        out_shape=jax.ShapeDtypeStruct((B, T, PW), F32),
        compiler_params=_cparams(("arbitrary", "arbitrary")),
        name="inproj",
    )(x, mod3, w_p, b_p)


def _softmax_av(s_list, v_list):
    m = s_list[0].max(-1, keepdims=True)
    for s in s_list[1:]:
        m = jnp.maximum(m, s.max(-1, keepdims=True))
    acc = None
    l = None
    for s, v in zip(s_list, v_list):
        p = jnp.exp(s - m)
        ls = p.sum(-1, keepdims=True)
        o = _dot(p.astype(BF16), v)
        acc = o if acc is None else acc + o
        l = ls if l is None else l + ls
    return acc / l


NAT_GROUP = 8


def _natten_kernel(q_ref, k_ref, v_ref, qc_ref, kc_ref, vc_ref, bias_ref, o_ref, oc_ref, ks, vs, *, n_rows):
    ks[...] = k_ref[0].astype(BF16)
    vs[...] = v_ref[0].astype(BF16)
    kcb = kc_ref[0].astype(BF16)
    vcb = vc_ref[0].astype(BF16)
    lane = lax.broadcasted_iota(jnp.int32, (1, LANE), 1)
    head0 = lane < HEAD_DIM
    scale = HEAD_DIM ** -0.5

    def stack_heads(q):
        return jnp.concatenate([jnp.where(head0, q, 0.0), jnp.where(head0, 0.0, q)], 0).astype(BF16)

    def unstack(o, n):
        return jnp.where(head0, o[:n], o[n:])

    def rows_body(g, carry):
        koffs, scores = [], []
        for i in range(NAT_GROUP):
            r = g * NAT_GROUP + i
            rs = jnp.clip(r - WIN_ROWS // 2, 0, n_rows - WIN_ROWS)
            qs = stack_heads(q_ref[0, pl.ds(pl.multiple_of(r * GRID_W, GRID_W), GRID_W), :] * scale)
            koff = pl.multiple_of(rs * GRID_W, GRID_W)
            kl = ks[pl.ds(koff, WIN_ROWS * GRID_W), :]
            koffs.append(koff)
            scores.append([_dot_nt(qs, kl) + bias_ref[r - rs, 0], _dot_nt(qs, kcb)])
        for i in range(NAT_GROUP):
            r = g * NAT_GROUP + i
            vl = vs[pl.ds(koffs[i], WIN_ROWS * GRID_W), :]
            o = _softmax_av(scores[i], [vl, vcb])
            o_ref[0, pl.ds(pl.multiple_of(r * GRID_W, GRID_W), GRID_W), :] = unstack(o, GRID_W)
        return carry

    lax.fori_loop(0, n_rows // NAT_GROUP, rows_body, 0)

    tc = qc_ref.shape[1]
    oc = _softmax_av([_dot_nt(stack_heads(qc_ref[0] * scale), kcb)], [vcb])
    oc_ref[0] = unstack(oc, tc)


def _natten_call(p_lat, p_ctx, bias_l):
    B, T, _ = p_lat.shape
    Tc = p_ctx.shape[1]
    n_rows = T // GRID_W
    assert n_rows >= WIN_ROWS
    cb = lambda base: (lambda b, p: (b, 0, base // LANE + p))
    return pl.pallas_call(
        functools.partial(_natten_kernel, n_rows=n_rows),
        grid=(B, 4),
        in_specs=[pl.BlockSpec((1, T, LANE), cb(C_NQ)),
                  pl.BlockSpec((1, T, LANE), cb(C_NK)),
                  pl.BlockSpec((1, T, LANE), cb(C_NV)),
                  pl.BlockSpec((1, Tc, LANE), cb(C_NQ)),
                  pl.BlockSpec((1, Tc, LANE), cb(C_NK)),
                  pl.BlockSpec((1, Tc, LANE), cb(C_NV)),
                  pl.BlockSpec((WIN_ROWS, 1, 2 * GRID_W, WIN_ROWS * GRID_W), lambda b, p: (0, p, 0, 0))],
        out_specs=[pl.BlockSpec((1, T, LANE), lambda b, p: (b, 0, p)),
                   pl.BlockSpec((1, Tc, LANE), lambda b, p: (b, 0, p))],
        out_shape=[jax.ShapeDtypeStruct((B, T, 4 * LANE), F32),
                   jax.ShapeDtypeStruct((B, Tc, 4 * LANE), F32)],
        scratch_shapes=[pltpu.VMEM((T, LANE), BF16), pltpu.VMEM((T, LANE), BF16)],
        compiler_params=_cparams(("arbitrary", "arbitrary")),
        name="natten",
    )(p_lat, p_lat, p_lat, p_ctx, p_ctx, p_ctx, bias_l)


def _natten_bias_tables(rpb):
    col = np.arange(GRID_W)
    cstart = np.clip(col - WIN_COLS // 2, 0, GRID_W - WIN_COLS)
    col_ok = (col[None, :] >= cstart[:, None]) & (col[None, :] < cstart[:, None] + WIN_COLS)
    dc_idx = np.clip(col[None, :] - col[:, None] + WIN_COLS - 1, 0, 2 * WIN_COLS - 2)
    toe = rpb.astype(F32)[..., dc_idx]
    toe = jnp.where(col_ok[None, None, None], toe, NEG)
    t = jnp.stack([toe[:, :, WIN_ROWS - 1 - d:2 * WIN_ROWS - 1 - d] for d in range(WIN_ROWS)], 1)
    t = t.transpose(0, 1, 2, 4, 3, 5)
    return t.reshape(DEPTH, WIN_ROWS, 4, 2 * GRID_W, WIN_ROWS * GRID_W)


def _tri_masks():
    r = lax.broadcasted_iota(jnp.int32, (CHUNK, CHUNK), 0)
    c = lax.broadcasted_iota(jnp.int32, (CHUNK, CHUNK), 1)
    return r >= c, r <= c


def _rope(x, cs, sn, first, dist):
    w = x.shape[-1]
    partner = jnp.where(first, pltpu.roll(x, w - dist, 1), pltpu.roll(x, dist, 1))
    return x * cs + partner * sn


def _rope_tables(T, n_heads, half):
    t = jnp.arange(T)
    rows = (t // GRID_W).astype(F32)
    cols = (t % GRID_W).astype(F32)
    inv = ROPE_BASE ** (-jnp.arange(half, dtype=F32) / half)
    ar = rows[:, None] * inv[None, :]
    ac = cols[:, None] * inv[None, :]
    cos = jnp.concatenate([jnp.cos(ar), jnp.cos(ar), jnp.cos(ac), jnp.cos(ac)], -1)
    sin = jnp.concatenate([-jnp.sin(ar), jnp.sin(ar), -jnp.sin(ac), jnp.sin(ac)], -1)
    return jnp.tile(cos, (1, n_heads)), jnp.tile(sin, (1, n_heads))


def _bwd_chunk(c, nc, n):
    return jnp.where(c < nc, nc - 1 - c, n - 1 - c + nc)


def _mlstm_kernel(qkl_ref, vl_ref, sml_ref, qkc_ref, vc_ref, smc_ref, cos_ref, sin_ref, cw_ref,
                  ol_ref, oc_ref,
                  qt_s, k_s, vat_s, gcol_s, grow_s, brow_s, cm_s, ot_s, ck_s, *, nc, n):
    low, upp = _tri_masks()
    tri = jnp.where(low, 1.0, 0.0).astype(BF16)
    triu = jnp.where(upp, 1.0, 0.0).astype(BF16)
    lane128 = lax.broadcasted_iota(jnp.int32, (1, LANE), 1)
    lane256 = lax.broadcasted_iota(jnp.int32, (1, 2 * LANE), 1)
    rowi = lax.broadcasted_iota(jnp.int32, (CHUNK, 1), 0)
    row16 = lax.broadcasted_iota(jnp.int32, (16, 1), 0)
    first = (lane256 % 32) < 16
    hmask = [(lane256 >= HEAD_DIM * h) & (lane256 < HEAD_DIM * (h + 1)) for h in range(4)]
    hrows = [(rowi >= HEAD_DIM * h) & (rowi < HEAD_DIM * (h + 1)) for h in range(4)]
    ones_blk = jnp.where(lax.broadcasted_iota(jnp.int32, (HEAD_DIM, CHUNK), 0) == 0, 1.0, 0.0)
    fwd_rows = row16 < 8
    cw = cw_ref[...]

    def cummax_lanes(x, suffix):
        sh = 1
        while sh < CHUNK:
            if suffix:
                moved = jnp.where(lane256 < CHUNK - sh, pltpu.roll(x, CHUNK - sh, 1), NEG)
            else:
                moved = jnp.where(lane256 >= sh, pltpu.roll(x, sh, 1), NEG)
            x = jnp.maximum(x, moved)
            sh *= 2
        return x

    def prep(qk_ref, v_ref, sm_ref, ci, n_str, dst, use_rope):
        r0 = ci * CHUNK
        xc = qk_ref[0, r0:r0 + CHUNK, :]
        prev = qk_ref[0, r0 - 1:r0, :] if ci > 0 else jnp.zeros((1, 4 * LANE), F32)
        nxt = qk_ref[0, r0 + CHUNK:r0 + CHUNK + 1, :] if ci < n_str - 1 else jnp.zeros((1, 4 * LANE), F32)
        xp = jnp.where(rowi == 0, prev, pltpu.roll(xc, 1, 0))
        xn = jnp.where(rowi == CHUNK - 1, nxt, pltpu.roll(xc, CHUNK - 1, 0))
        y = jax.nn.silu(xp * cw[0:1] + xc * cw[1:2] + xn * cw[2:3])
        q = y[:, :2 * LANE]
        k = y[:, 2 * LANE:] * HEAD_DIM ** -0.5
        if use_rope:
            cs = cos_ref[r0:r0 + CHUNK, :]
            sn = sin_ref[r0:r0 + CHUNK, :]
            q = _rope(q, cs, sn, first, 16)
            k = _rope(k, cs, sn, first, 16)
        qt_s[dst] = q.T.astype(BF16)
        k_s[dst] = k.astype(BF16)
        vt = v_ref[0, r0:r0 + CHUNK, :].T
        for h in range(4):
            vat_s[h, dst] = jnp.concatenate([vt[HEAD_DIM * h:HEAD_DIM * (h + 1)], ones_blk], 0).astype(BF16)
        g = sm_ref[0, r0:r0 + CHUNK, :]
        lf = pltpu.roll(jax.nn.log_sigmoid(g), LANE - 4, 1)
        gcol_s[dst] = g - jnp.where(lane128 < 8, _dot_exact_l(tri, lf), _dot_exact_l(triu, lf))
        gt = g.T[0:16]
        lft = pltpu.roll(jax.nn.log_sigmoid(gt), 12, 0)
        brow = jnp.where(fwd_rows, _dot_exact_r(lft, triu), _dot_exact_r(lft, tri))
        brow_s[dst] = brow
        grow_s[dst] = gt - brow

    for ci in range(nc):
        prep(qkc_ref, vc_ref, smc_ref, ci, nc, ci, False)
    for ci in range(n - nc):
        prep(qkl_ref, vl_ref, sml_ref, ci, n - nc, nc + ci, True)
    g_all = grow_s[...].reshape(n * 16, CHUNK)
    fwd_all = (lax.broadcasted_iota(jnp.int32, (n * 16, 1), 0) % 16) < 8
    cm_s[...] = jnp.where(fwd_all, cummax_lanes(g_all, False), cummax_lanes(g_all, True)).reshape(n, 16, CHUNK)

    def scan_dir(bwd):
        ck_s[...] = jnp.zeros_like(ck_s)
        gi = 8 if bwd else 0
        causal = low if bwd else upp

        def chunk_body(c, ms):
            ch = _bwd_chunk(c, nc, n) if bwd else c
            qt = qt_s[ch]
            kb = k_s[ch]
            gcol_all = gcol_s[ch]
            ck_b = ck_s[...].astype(BF16)
            new_ms = []
            qtms = [jnp.where(hrows[h], qt, jnp.zeros_like(qt)) for h in range(4)]
            scores = [_dot(kb, qtms[h]) for h in range(4)]
            inters = [_dot(ck_b, qtms[h]) for h in range(4)]
            for h in range(4):
                m = ms[h]
                g_row = grow_s[ch, gi + h:gi + h + 1, :]
                b_row = brow_s[ch, gi + h:gi + h + 1, :]
                a_row = jnp.maximum(m, cm_s[ch, gi + h:gi + h + 1, :])
                w = jnp.exp(jnp.where(causal, gcol_all[:, gi + h:gi + h + 1] - a_row, NEG))
                pt = (scores[h] * w).astype(BF16)
                vat = vat_s[h, ch]
                nd = _dot(vat, pt) + jnp.exp(m - a_row) * inters[h]
                den = nd[HEAD_DIM:HEAD_DIM + 1, :]
                ht = nd[0:HEAD_DIM] / jnp.maximum(jnp.abs(den), jnp.exp(-(b_row + a_row)))
                rows = slice(HEAD_DIM * h, HEAD_DIM * (h + 1))
                if bwd:
                    ot_s[ch, rows, :] = ot_s[ch, rows, :] + ht
                else:
                    ot_s[ch, rows, :] = ht
                bl = b_row[:, 0:1] if bwd else b_row[:, CHUNK - 1:CHUNK]
                lw_end = bl + g_row
                m_new = jnp.maximum(bl + m, lw_end.max(-1, keepdims=True))
                upd = _dot((vat * jnp.exp(lw_end - m_new)).astype(BF16), kb)
                ck_s[...] = jnp.where(hmask[h], jnp.exp(bl + m - m_new) * ck_s[...] + upd, ck_s[...])
                new_ms.append(m_new)
            return tuple(new_ms)

        lax.fori_loop(0, n, chunk_body, tuple(jnp.zeros((1, 1), F32) for _ in range(4)))

    scan_dir(False)
    scan_dir(True)
    for ci in range(nc):
        oc_ref[0, ci * CHUNK:(ci + 1) * CHUNK, :] = ot_s[ci].T
    for ci in range(n - nc):
        ol_ref[0, ci * CHUNK:(ci + 1) * CHUNK, :] = ot_s[nc + ci].T


def _mlstm_call(p_lat, p_ctx, cos, sin, conv_w_l):
    B, T, _ = p_lat.shape
    Tc = p_ctx.shape[1]
    nc, n = Tc // CHUNK, (Tc + T) // CHUNK
    cb = lambda base, w: (lambda b: (b, 0, base // w))
    return pl.pallas_call(
        functools.partial(_mlstm_kernel, nc=nc, n=n),
        grid=(B,),
        in_specs=[pl.BlockSpec((1, T, 512), cb(C_MQK, 512)),
                  pl.BlockSpec((1, T, 256), cb(C_MV, 256)),
                  pl.BlockSpec((1, T, LANE), cb(C_SM, LANE)),
                  pl.BlockSpec((1, Tc, 512), cb(C_MQK, 512)),
                  pl.BlockSpec((1, Tc, 256), cb(C_MV, 256)),
                  pl.BlockSpec((1, Tc, LANE), cb(C_SM, LANE)),
                  pl.BlockSpec((T, 256), lambda b: (0, 0)),
                  pl.BlockSpec((T, 256), lambda b: (0, 0)),
                  pl.BlockSpec((CONV_K, 512), lambda b: (0, 0))],
        out_specs=[pl.BlockSpec((1, T, 256), lambda b: (b, 0, 0)),
                   pl.BlockSpec((1, Tc, 256), lambda b: (b, 0, 0))],
        out_shape=[jax.ShapeDtypeStruct((B, T, 256), F32),
                   jax.ShapeDtypeStruct((B, Tc, 256), F32)],
        scratch_shapes=[pltpu.VMEM((n, 256, CHUNK), BF16),
                        pltpu.VMEM((n, CHUNK, 256), BF16),
                        pltpu.VMEM((4, n, LANE, CHUNK), BF16),
                        pltpu.VMEM((n, CHUNK, LANE), F32),
                        pltpu.VMEM((n, 16, CHUNK), F32),
                        pltpu.VMEM((n, 16, CHUNK), F32),
                        pltpu.VMEM((n, 16, CHUNK), F32),
                        pltpu.VMEM((n, 256, CHUNK), F32),
                        pltpu.VMEM((LANE, 256), F32)],
        compiler_params=_cparams(("arbitrary",)),
        name="mlstm",
    )(p_lat, p_lat, p_lat, p_ctx, p_ctx, p_ctx, cos, sin, conv_w_l)


GLA_BLK = 64
GLA_CLAMP = 80.0


def _gla_kernel(ql_ref, kl_ref, vl_ref, sml_ref, qc_ref, kc_ref, vc_ref, smc_ref, cos_ref, sin_ref, w2_ref, b2_ref,
                ol_ref, oc_ref,
                q_s, k_s, v_s, a_s, b_s, o_s, st_s, *, nc, n):
    low, upp = _tri_masks()
    tri = jnp.where(low, 1.0, 0.0).astype(BF16)
    triu = jnp.where(upp, 1.0, 0.0).astype(BF16)
    lane128 = lax.broadcasted_iota(jnp.int32, (1, LANE), 1)
    lane256 = lax.broadcasted_iota(jnp.int32, (1, 2 * LANE), 1)
    first = (lane128 % 16) < 8
    hm128 = [(lane128 >= 32 * h) & (lane128 < 32 * (h + 1)) for h in range(4)]
    hm256 = [(lane256 >= 64 * h) & (lane256 < 64 * (h + 1)) for h in range(4)]
    nb = CHUNK // GLA_BLK
    r_st = lax.broadcasted_iota(jnp.int32, (nb * GLA_BLK, CHUNK), 0)
    c_st = lax.broadcasted_iota(jnp.int32, (nb * GLA_BLK, CHUNK), 1)
    bd_r = lax.broadcasted_iota(jnp.int32, (LANE, 2 * LANE), 0)
    bd_c = lax.broadcasted_iota(jnp.int32, (LANE, 2 * LANE), 1)
    blockdiag = (bd_r // 32) == (bd_c // 64)

    def prep(q_ref, k_ref, v_ref, sm_ref, ci, dst, use_rope):
        r0 = ci * CHUNK
        q = q_ref[0, r0:r0 + CHUNK, :] * 32 ** -0.5
        k = k_ref[0, r0:r0 + CHUNK, :]
        if use_rope:
            cs = cos_ref[r0:r0 + CHUNK, :]
            sn = sin_ref[r0:r0 + CHUNK, :]
            q = _rope(q, cs, sn, first, 8)
            k = _rope(k, cs, sn, first, 8)
        q_s[dst] = q
        k_s[dst] = k
        v_s[dst] = v_ref[0, r0:r0 + CHUNK, :].astype(BF16)
        lr = sm_ref[0, r0:r0 + CHUNK, :].astype(BF16)
        for d in range(2):
            a = jax.nn.log_sigmoid(_dot(lr, w2_ref[d]) + b2_ref[d]) / GLA_TAU
            a_s[d, dst] = a
            b_s[d, dst] = _dot_exact_l(triu if d else tri, a)

    for ci in range(nc):
        prep(qc_ref, kc_ref, vc_ref, smc_ref, ci, ci, False)
    for ci in range(n - nc):
        prep(ql_ref, kl_ref, vl_ref, sml_ref, ci, nc + ci, True)

    def scan_dir(bwd):
        d = 1 if bwd else 0
        st_s[...] = jnp.zeros_like(st_s)

        def chunk_body(c, carry):
            ch = _bwd_chunk(c, nc, n) if bwd else c
            q = q_s[ch]
            k = k_s[ch]
            vb = v_s[ch]
            a = a_s[d, ch]
            b = b_s[d, ch]
            st_b = st_s[...].astype(BF16)
            o_inter = _dot((q * jnp.exp(b)).astype(BF16), st_b)
            atts = []
            for i in range(nb):
                rows = slice(i * GLA_BLK, (i + 1) * GLA_BLK)
                e = (i + 1) * GLA_BLK - 1 if bwd else i * GLA_BLK
                ref = b[e:e + 1, :] - a[e:e + 1, :]
                qs = q[rows] * jnp.exp(b[rows] - ref)
                ks = (k * jnp.exp(jnp.minimum(ref - b, GLA_CLAMP))).astype(BF16)
                lhs = jnp.concatenate([jnp.where(hm128[h], qs, 0.0) for h in range(4)], 0).astype(BF16)
                att = _dot_nt(lhs, ks)
                t_idx = (r_st % GLA_BLK) + i * GLA_BLK
                ok = (c_st >= t_idx) if bwd else (c_st <= t_idx)
                atts.append(jnp.where(ok, att, 0.0).astype(BF16))
            for i in range(nb):
                rows = slice(i * GLA_BLK, (i + 1) * GLA_BLK)
                oh = _dot(atts[i], vb)
                o_blk = o_inter[rows]
                for h in range(4):
                    o_blk = o_blk + jnp.where(hm256[h], oh[h * GLA_BLK:(h + 1) * GLA_BLK], 0.0)
                if bwd:
                    o_s[ch, rows, :] = o_s[ch, rows, :] + o_blk
                else:
                    o_s[ch, rows, :] = o_blk
            bt = b.T
            tot = bt[:, 0:1] if bwd else bt[:, CHUNK - 1:CHUNK]
            kt = (k.T * jnp.exp(tot - bt)).astype(BF16)
            upd = jnp.where(blockdiag, _dot(kt, vb), 0.0)
            st_s[...] = jnp.exp(tot) * st_s[...] + upd
            return carry

        lax.fori_loop(0, n, chunk_body, 0)

    scan_dir(False)
    scan_dir(True)
    for ci in range(nc):
        oc_ref[0, ci * CHUNK:(ci + 1) * CHUNK, :] = o_s[ci]
    for ci in range(n - nc):
        ol_ref[0, ci * CHUNK:(ci + 1) * CHUNK, :] = o_s[nc + ci]


def _gla_call(p_lat, p_ctx, cos, sin, w2p, b2p):
    B, T, _ = p_lat.shape
    Tc = p_ctx.shape[1]
    nc, n = Tc // CHUNK, (Tc + T) // CHUNK
    cb = lambda base, w: (lambda b: (b, 0, base // w))
    return pl.pallas_call(
        functools.partial(_gla_kernel, nc=nc, n=n),
        grid=(B,),
        in_specs=[pl.BlockSpec((1, T, LANE), cb(C_GQ, LANE)),
                  pl.BlockSpec((1, T, LANE), cb(C_GK, LANE)),
                  pl.BlockSpec((1, T, 256), cb(C_GV, 256)),
                  pl.BlockSpec((1, T, LANE), cb(C_SM, LANE)),
                  pl.BlockSpec((1, Tc, LANE), cb(C_GQ, LANE)),
                  pl.BlockSpec((1, Tc, LANE), cb(C_GK, LANE)),
                  pl.BlockSpec((1, Tc, 256), cb(C_GV, 256)),
                  pl.BlockSpec((1, Tc, LANE), cb(C_SM, LANE)),
                  pl.BlockSpec((T, LANE), lambda b: (0, 0)),
                  pl.BlockSpec((T, LANE), lambda b: (0, 0)),
                  pl.BlockSpec((2, LANE, LANE), lambda b: (0, 0, 0)),
                  pl.BlockSpec((2, 1, LANE), lambda b: (0, 0, 0))],
        out_specs=[pl.BlockSpec((1, T, 256), lambda b: (b, 0, 0)),
                   pl.BlockSpec((1, Tc, 256), lambda b: (b, 0, 0))],
        out_shape=[jax.ShapeDtypeStruct((B, T, 256), F32),
                   jax.ShapeDtypeStruct((B, Tc, 256), F32)],
        scratch_shapes=[pltpu.VMEM((n, CHUNK, LANE), F32),
                        pltpu.VMEM((n, CHUNK, LANE), F32),
                        pltpu.VMEM((n, CHUNK, 256), BF16),
                        pltpu.VMEM((2, n, CHUNK, LANE), F32),
                        pltpu.VMEM((2, n, CHUNK, LANE), F32),
                        pltpu.VMEM((n, CHUNK, 256), F32),
                        pltpu.VMEM((LANE, 256), F32)],
        compiler_params=_cparams(("arbitrary",)),
        name="gla",
    )(p_lat, p_lat, p_lat, p_lat, p_ctx, p_ctx, p_ctx, p_ctx, cos, sin, w2p, b2p)


def _gla_gate_weights(gla_w2, gla_b2):
    w = jnp.zeros((DEPTH, 2, LANE, LANE), F32)
    w = w.at[:, 0, 16:32].set(gla_w2[:, 0]).at[:, 1, 32:48].set(gla_w2[:, 1])
    return w.astype(BF16), gla_b2.astype(F32).reshape(DEPTH, 2, 1, LANE)


TILE_ROWS = D // LANE


def _store_token_tiles(ref, val):
    tm = val.shape[0]
    for s in range(TILE_ROWS):
        ref[0, pl.ds(s, tm, stride=TILE_ROWS), :] = val[:, s * LANE:(s + 1) * LANE]


def _load_token_tiles(ref, lead, t0, tm):
    return [ref[lead + (pl.ds(t0 * TILE_ROWS + s, tm, stride=TILE_ROWS), slice(None))] for s in range(TILE_ROWS)]


def _layer_norm(z, g, b):
    mu = z.mean(-1, keepdims=True)
    zc = z - mu
    var = jnp.mean(jnp.square(zc), -1, keepdims=True)
    return zc * lax.rsqrt(var + LN_EPS) * g + b


def _merge_kernel(hm_ref, hg_ref, hn_ref, mo_ref, gr_ref, x_ref, mod_ref, wout_ref, wr_ref, ng_ref, ln_ref,
                  x1_ref, u2_ref, lg_ref):
    r = lax.broadcasted_iota(jnp.int32, (256, 256), 0)
    c = lax.broadcasted_iota(jnp.int32, (256, 256), 1)
    avg = jnp.where((r // HEAD_DIM) == (c // HEAD_DIM), 1.0 / HEAD_DIM, 0.0).astype(BF16)

    def seg_mean(x):
        hi = x.astype(BF16)
        lo = (x - hi.astype(F32)).astype(BF16)
        return _dot(hi, avg) + _dot(lo, avg)

    def head_norm(h):
        d = h - seg_mean(h)
        return d * lax.rsqrt(seg_mean(d * d) + LN_EPS)

    ym = head_norm(hm_ref[0]) * ng_ref[0:1, :] * jax.nn.sigmoid(mo_ref[0])
    yg = head_norm(hg_ref[0]) * ng_ref[1:2, :] * jax.nn.silu(gr_ref[0])
    y = (_dot(ym.astype(BF16), wout_ref[0:256, :]) + _dot(yg.astype(BF16), wout_ref[256:512, :])
         + _dot(hn_ref[0].astype(BF16), wout_ref[512:1024, :]))
    m = mod_ref[0]
    x1 = _layer_norm(DEEPNORM_ALPHA * x_ref[0] + m[:, 2 * D:3 * D] * y, ln_ref[0:1, :], ln_ref[1:2, :])
    x1_ref[0] = x1
    u2 = x1 * (1.0 + m[:, 4 * D:5 * D]) + m[:, 3 * D:4 * D]
    _store_token_tiles(u2_ref, u2)
    lg_ref[0] = _dot_nt(wr_ref[...], u2.astype(BF16))


def _merge_call(hm, hg, hn, p, x, mod3, wout_b, wrt_b, ng, ln, shared_row):
    B, T, _ = x.shape
    tm = 256
    mod_map = (lambda b, i: (b, 0, 0)) if shared_row is None else (lambda b, i: (shared_row, 0, 0))
    tok = lambda w: pl.BlockSpec((1, tm, w), lambda b, i: (b, i, 0))
    return pl.pallas_call(
        _merge_kernel,
        grid=(B, T // tm),
        in_specs=[tok(256), tok(256), tok(512),
                  pl.BlockSpec((1, tm, 256), lambda b, i: (b, i, C_MO // 256)),
                  pl.BlockSpec((1, tm, 256), lambda b, i: (b, i, C_GR // 256)),
                  tok(D),
                  pl.BlockSpec((1, 1, 6 * D), mod_map),
                  pl.BlockSpec((D, D), lambda b, i: (0, 0)),
                  pl.BlockSpec((N_EXPERTS, D), lambda b, i: (0, 0)),
                  pl.BlockSpec((2, 256), lambda b, i: (0, 0)),
                  pl.BlockSpec((2, D), lambda b, i: (0, 0))],
        out_specs=[tok(D), pl.BlockSpec((1, tm * TILE_ROWS, LANE), lambda b, i: (b, i, 0)),
                   pl.BlockSpec((1, N_EXPERTS, tm), lambda b, i: (b, 0, i))],
        out_shape=[jax.ShapeDtypeStruct((B, T, D), F32), jax.ShapeDtypeStruct((B, T * TILE_ROWS, LANE), F32),
                   jax.ShapeDtypeStruct((B, N_EXPERTS, T), F32)],
        compiler_params=_cparams(("arbitrary", "arbitrary")),
        name="merge",
    )(hm, hg, hn, p, p, x, mod3, wout_b, wrt_b, ng, ln)


ROUTER_SAMPLES = 8


def _router_samples(B):
    ns = ROUTER_SAMPLES
    while B % ns:
        ns -= 1
    return ns


def _router_kernel(lg_ref, o_ref, aff_s, sp_s, *, T, cap):
    J = T // LANE
    E = N_EXPERTS
    NS = lg_ref.shape[0]
    keys = [[] for _ in range(NS)]
    for j in range(J):
        for si in range(NS):
            lg = lg_ref[si, :, j * LANE:(j + 1) * LANE]
            ex = jnp.exp(lg - lg.max(0, keepdims=True))
            aff = ex / ex.sum(0, keepdims=True)
            aff_s[si, j * E:(j + 1) * E, :] = aff
            keys[si].append(pltpu.bitcast(aff, jnp.int32))

    def count(ks, pred):
        cnt = None
        for k in ks:
            cj = jnp.where(pred(k), 1.0, 0.0)
            cnt = cj if cnt is None else cnt + cj
        return cnt.sum(-1, keepdims=True)

    thrs = [jnp.zeros((E, 1), jnp.int32) for _ in range(NS)]
    for bit in range(30, -1, -1):
        for si in range(NS):
            cand = thrs[si] | (1 << bit)
            thrs[si] = jnp.where(count(keys[si], lambda k: k >= cand) >= cap, cand, thrs[si])

    r = lax.broadcasted_iota(jnp.int32, (LANE, LANE), 0)
    c = lax.broadcasted_iota(jnp.int32, (LANE, LANE), 1)
    upper = jnp.where(r <= c, 1.0, 0.0).astype(BF16)
    ones = jnp.ones((LANE, LANE), BF16)
    rr = lax.broadcasted_iota(jnp.int32, (J * E, J * E), 0)
    cc = lax.broadcasted_iota(jnp.int32, (J * E, J * E), 1)
    earlier = jnp.where(((rr % E) == (cc % E)) & ((cc // E) < (rr // E)), 1.0, 0.0).astype(BF16)

    def prefix(x01):
        xb = x01.astype(BF16)
        return _dot(xb, upper) + _dot(earlier, _dot(xb, ones).astype(BF16))

    for si in range(NS):
        thr = thrs[si]
        need = cap - count(keys[si], lambda k: k > thr)
        gt = jnp.concatenate([jnp.where(k > thr, 1.0, 0.0) for k in keys[si]], 0)
        eq = jnp.concatenate([jnp.where(k == thr, 1.0, 0.0) for k in keys[si]], 0)
        need_t = jnp.concatenate([need] * J, 0)
        sel = jnp.maximum(gt, jnp.where(prefix(eq) <= need_t, eq, 0.0))
        sp_s[si] = jnp.where(sel > 0.0, prefix(sel) - 1.0, -1.0)

    sb = min(cap, LANE)
    lane = lax.broadcasted_iota(jnp.int32, (1, LANE), 1)
    o_ref[...] = jnp.zeros_like(o_ref)
    for e in range(E):
        for half in range(cap // sb):
            slot = (lax.broadcasted_iota(jnp.int32, (sb, LANE), 0) + half * sb).astype(F32)
            rows = slice(half * sb, (half + 1) * sb)
            for si in range(NS):
                def jbody(j, acc):
                    acc_i, acc_g = acc
                    sp = sp_s[si, pl.ds(j * E + e, 1), :]
                    af = aff_s[si, pl.ds(j * E + e, 1), :]
                    hit = sp == slot
                    tid = (lane + j * LANE).astype(F32)
                    return jnp.where(hit, tid, acc_i), jnp.where(hit, af, acc_g)

                acc_i, acc_g = lax.fori_loop(0, J, jbody, (jnp.zeros((sb, LANE), F32), jnp.zeros((sb, LANE), F32)),
                                             unroll=2)
                icol = acc_i.sum(-1, keepdims=True)
                gcol = acc_g.sum(-1, keepdims=True)
                o_ref[si, rows, :] = jnp.where(lane == e, icol, jnp.where(lane == E + e, gcol, o_ref[si, rows, :]))


def _router_call(logits):
    B, _, T = logits.shape
    cap = CAPACITY_FACTOR * T // N_EXPERTS
    J = T // LANE
    ns = _router_samples(B)
    out = pl.pallas_call(
        functools.partial(_router_kernel, T=T, cap=cap),
        grid=(B // ns,),
        in_specs=[pl.BlockSpec((ns, N_EXPERTS, T), lambda b: (b, 0, 0))],
        out_specs=pl.BlockSpec((ns, cap, LANE), lambda b: (b, 0, 0)),
        out_shape=jax.ShapeDtypeStruct((B, cap, LANE), F32),
        scratch_shapes=[pltpu.VMEM((ns, J * N_EXPERTS, LANE), F32), pltpu.VMEM((ns, J * N_EXPERTS, LANE), F32)],
        compiler_params=_cparams(("arbitrary",)),
        name="router",
    )(logits)
    idx = out[:, :, :N_EXPERTS].astype(jnp.int32).transpose(0, 2, 1).reshape(B, 1, N_EXPERTS * cap)
    gate = out[:, :, N_EXPERTS:2 * N_EXPERTS].transpose(0, 2, 1).reshape(B, 1, N_EXPERTS * cap)
    return idx, gate


def _gather_kernel(idx_ref, u_ref, o_ref, *, cap, eg):
    g = pl.program_id(1)
    for e in range(eg):
        def body(s, carry):
            r = idx_ref[0, 0, (g * eg + e) * cap + s]
            src = pl.ds(pl.multiple_of(r * TILE_ROWS, TILE_ROWS), TILE_ROWS)
            o_ref[e, 0, pl.ds(pl.multiple_of(s * TILE_ROWS, TILE_ROWS), TILE_ROWS), :] = u_ref[0, src, :]
            return carry
        lax.fori_loop(0, cap, body, 0, unroll=8)


def _gather_call(idx, u2t):
    B = u2t.shape[0]
    T = u2t.shape[1] // TILE_ROWS
    cap = CAPACITY_FACTOR * T // N_EXPERTS
    eg = 4
    smem = lambda: pl.BlockSpec((1, 1, N_EXPERTS * cap), lambda b, g: (b, 0, 0), memory_space=pltpu.SMEM)
    return pl.pallas_call(
        functools.partial(_gather_kernel, cap=cap, eg=eg),
        grid=(B, N_EXPERTS // eg),
        in_specs=[smem(), pl.BlockSpec((1, T * TILE_ROWS, LANE), lambda b, g: (b, 0, 0))],
        out_specs=pl.BlockSpec((eg, 1, cap * TILE_ROWS, LANE), lambda b, g: (g, b, 0, 0)),
        out_shape=jax.ShapeDtypeStruct((N_EXPERTS, B, cap * TILE_ROWS, LANE), F32),
        compiler_params=_cparams(("arbitrary", "arbitrary")),
        name="gather",
    )(idx, u2t)


FF_CHUNK = 512


def _ffn_kernel(x_ref, wg_ref, wu_ref, wd_ref, o_ref, xs, *, tm):
    for s, piece in enumerate(_load_token_tiles(x_ref, (0,), 0, tm)):
        xs[:, s * LANE:(s + 1) * LANE] = piece.astype(BF16)
    xb = xs[...]
    acc = None
    for c in range(EXPERT_FF // FF_CHUNK):
        cols = slice(c * FF_CHUNK, (c + 1) * FF_CHUNK)
        h = (jax.nn.silu(_dot(xb, wg_ref[0, :, cols])) * _dot(xb, wu_ref[0, :, cols])).astype(BF16)
        t = _dot(h, wd_ref[0, cols, :])
        acc = t if acc is None else acc + t
    _store_token_tiles(o_ref, acc)


def _ffn_call(xe, wg_b, wu_b, wd_b, layer):
    E = xe.shape[0]
    M = xe.shape[1] // TILE_ROWS
    tm = min(M, 512)
    w_map = lambda e, i: (layer * E + e, 0, 0)
    return pl.pallas_call(
        functools.partial(_ffn_kernel, tm=tm),
        grid=(E, M // tm),
        in_specs=[pl.BlockSpec((1, tm * TILE_ROWS, LANE), lambda e, i: (e, i, 0)),
                  pl.BlockSpec((1, D, EXPERT_FF), w_map),
                  pl.BlockSpec((1, D, EXPERT_FF), w_map),
                  pl.BlockSpec((1, EXPERT_FF, D), w_map)],
        out_specs=pl.BlockSpec((1, tm * TILE_ROWS, LANE), lambda e, i: (e, i, 0)),
        out_shape=jax.ShapeDtypeStruct((E, M * TILE_ROWS, LANE), F32),
        scratch_shapes=[pltpu.VMEM((tm, D), BF16)],
        compiler_params=_cparams(("arbitrary", "arbitrary")),
        name="ffn",
    )(xe, wg_b, wu_b, wd_b)


SCATTER_FIN_ROWS = 256


def _scatter_kernel(idx_ref, gate_ref, y_ref, x1_ref, mod_ref, ln_ref, o_ref, acc_a, acc_b, *, cap, n_g, T):
    g = pl.program_id(1)

    @pl.when(g == 0)
    def _():
        acc_a[...] = jnp.zeros_like(acc_a)
        acc_b[...] = jnp.zeros_like(acc_b)

    def body(s, carry):
        p0 = (g * 2) * cap + s
        p1 = p0 + cap
        src = pl.ds(pl.multiple_of(s * TILE_ROWS, TILE_ROWS), TILE_ROWS)
        d0 = pl.ds(pl.multiple_of(idx_ref[0, 0, p0] * TILE_ROWS, TILE_ROWS), TILE_ROWS)
        d1 = pl.ds(pl.multiple_of(idx_ref[0, 0, p1] * TILE_ROWS, TILE_ROWS), TILE_ROWS)
        acc_a[d0, :] = acc_a[d0, :] + y_ref[0, 0, src, :] * gate_ref[0, 0, p0]
        acc_b[d1, :] = acc_b[d1, :] + y_ref[1, 0, src, :] * gate_ref[0, 0, p1]
        return carry
    lax.fori_loop(0, cap, body, 0, unroll=8)

    @pl.when(g == n_g - 1)
    def _():
        g2 = mod_ref[0][:, 5 * D:6 * D]
        tb = min(T, SCATTER_FIN_ROWS)
        for i in range(T // tb):
            rows = slice(i * tb, (i + 1) * tb)
            f = jnp.concatenate([pa + pb for pa, pb in zip(_load_token_tiles(acc_a, (), i * tb, tb),
                                                             _load_token_tiles(acc_b, (), i * tb, tb))], -1)
            z = DEEPNORM_ALPHA * x1_ref[0, rows, :] + g2 * f
            o_ref[0, rows, :] = _layer_norm(z, ln_ref[0:1, :], ln_ref[1:2, :])


def _scatter_call(idx, gate, y4, x1, mod3, ln, shared_row):
    B, T, _ = x1.shape
    cap = CAPACITY_FACTOR * T // N_EXPERTS
    eg = 2
    n_g = N_EXPERTS // eg
    mod_map = (lambda b, g: (b, 0, 0)) if shared_row is None else (lambda b, g: (shared_row, 0, 0))
    smem = lambda: pl.BlockSpec((1, 1, N_EXPERTS * cap), lambda b, g: (b, 0, 0), memory_space=pltpu.SMEM)
    return pl.pallas_call(
        functools.partial(_scatter_kernel, cap=cap, n_g=n_g, T=T),
        grid=(B, n_g),
        scratch_shapes=[pltpu.VMEM((T * TILE_ROWS, LANE), F32), pltpu.VMEM((T * TILE_ROWS, LANE), F32)],
        in_specs=[smem(), smem(),
                  pl.BlockSpec((eg, 1, cap * TILE_ROWS, LANE), lambda b, g: (g, b, 0, 0)),
                  pl.BlockSpec((1, T, D), lambda b, g: (b, 0, 0), pipeline_mode=pl.Buffered(1)),
                  pl.BlockSpec((1, 1, 6 * D), mod_map),
                  pl.BlockSpec((2, D), lambda b, g: (0, 0))],
        out_specs=pl.BlockSpec((1, T, D), lambda b, g: (b, 0, 0)),
        out_shape=jax.ShapeDtypeStruct((B, T, D), F32),
        compiler_params=_cparams(("arbitrary", "arbitrary")),
        name="scatter",
    )(idx, gate, y4, x1, mod3, ln)


def _moe(x1, u2t, logits, mod3, wg_b, wu_b, wd_b, layer, ln2, shared_row):
    B, T, _ = x1.shape
    cap = CAPACITY_FACTOR * T // N_EXPERTS
    idx, gate = _router_call(logits)
    xe = _gather_call(idx, u2t)
    y = _ffn_call(xe.reshape(N_EXPERTS, B * cap * TILE_ROWS, LANE), wg_b, wu_b, wd_b, layer)
    return _scatter_call(idx, gate, y.reshape(N_EXPERTS, B, cap * TILE_ROWS, LANE), x1, mod3, ln2, shared_row)


def kernel(x, c, ctx, c_ctx, w_mod, b_mod, w_in, b_in, conv_w, gla_w2, gla_b2, mlstm_norm_g, gla_norm_g, rpb, w_out, ln1_g, ln1_b, w_router, w_gate, w_up, w_down, ln2_g, ln2_b):
    B, T, _ = x.shape
    n_mod = -(-(B + 1) // 8) * 8
    c_all = jnp.concatenate([c, c_ctx[None], jnp.zeros((n_mod - B - 1, D), F32)], 0)
    mods = _mod_call(c_all, w_mod, b_mod)
    w_p = _repack_columns(w_in).astype(BF16)
    b_p = _repack_columns(b_in)
    bias = _natten_bias_tables(rpb)
    mcos, msin = _rope_tables(T, 4, 16)
    gcos, gsin = _rope_tables(T, 4, 8)
    w2p, b2p = _gla_gate_weights(gla_w2, gla_b2)
    wout_b = w_out.astype(BF16)
    wrt_b = jnp.swapaxes(w_router, 1, 2).astype(BF16)
    wg_b = w_gate.astype(BF16).reshape(DEPTH * N_EXPERTS, D, EXPERT_FF)
    wu_b = w_up.astype(BF16).reshape(DEPTH * N_EXPERTS, D, EXPERT_FF)
    wd_b = w_down.astype(BF16).reshape(DEPTH * N_EXPERTS, EXPERT_FF, D)
    ng = jnp.stack([mlstm_norm_g, gla_norm_g], 1)
    ln1 = jnp.stack([ln1_g, ln1_b], 1)
    ln2 = jnp.stack([ln2_g, ln2_b], 1)
    for l in range(DEPTH):
        mod3 = mods[l].reshape(n_mod, 1, 6 * D)
        p_lat = _inproj_call(x, mod3, w_p[l], b_p[l][None], None)
        p_ctx = _inproj_call(ctx, mod3, w_p[l], b_p[l][None], B)
        n_lat, n_ctx = _natten_call(p_lat, p_ctx, bias[l])
        m_lat, m_ctx = _mlstm_call(p_lat, p_ctx, mcos, msin, conv_w[l])
        g_lat, g_ctx = _gla_call(p_lat, p_ctx, gcos, gsin, w2p[l], b2p[l])
        x1, u2, lg = _merge_call(m_lat, g_lat, n_lat, p_lat, x, mod3, wout_b[l], wrt_b[l], ng[l], ln1[l], None)
        x = _moe(x1, u2, lg, mod3, wg_b, wu_b, wd_b, l, ln2[l], None)
        if l < DEPTH - 1:
            c1, u2c, lgc = _merge_call(m_ctx, g_ctx, n_ctx, p_ctx, ctx, mod3, wout_b[l], wrt_b[l], ng[l], ln1[l], B)
            ctx = _moe(c1, u2c, lgc, mod3, wg_b, wu_b, wd_b, l, ln2[l], B)
    return x
```

```python
import functools

import numpy as np
import jax
import jax.numpy as jnp
from jax import lax
from jax.experimental import pallas as pl
from jax.experimental.pallas import tpu as pltpu

F32 = jnp.float32
BF16 = jnp.bfloat16

D = 1024
DEPTH = 4
GRID_W = 64
HEAD_DIM = 64
WIN_ROWS = 8
WIN_COLS = 16
CONV_K = 3
ROPE_BASE = 10000.0
N_EXPERTS = 16
EXPERT_FF = 2 * D
CAPACITY_FACTOR = 2
LN_EPS = 1e-5
GLA_TAU = 16.0
GLA_RANK = 16
DEEPNORM_ALPHA = (2 * DEPTH) ** 0.25
NEG = -1e30

VMEM_LIMIT = 56 * 1024 * 1024
LANE = 128
CHUNK = 256

C_MQK, C_NQ, C_NK, C_NV = 0, 512, 1024, 1536
C_MV, C_MO, C_GV, C_GR = 2048, 2304, 2560, 2816
C_GQ, C_GK, C_SM = 3072, 3200, 3328
PW = 3456
_ORIG = dict(m_q=0, m_k=256, m_v=512, m_o=768, m_g=1024, g_q=1040, g_k=1168, g_v=1296, g_r=1552,
             g_lr=1808, n_q=1840, n_k=2352, n_v=2864)
_PERM = np.concatenate([
    np.arange(0, 512), np.arange(1840, 2352), np.arange(2352, 2864), np.arange(2864, 3376),
    np.arange(512, 768), np.arange(768, 1024), np.arange(1296, 1552), np.arange(1552, 1808),
    np.arange(1040, 1168), np.arange(1168, 1296), np.arange(1024, 1040), np.arange(1808, 1840)])
_NPAD = PW - _PERM.size
_SEGMENTS = ((0, 512), (1840, 2352), (2352, 2864), (2864, 3376), (512, 768), (768, 1024), (1296, 1552),
             (1552, 1808), (1040, 1168), (1168, 1296), (1024, 1040), (1808, 1840))


def _repack_columns(w):
    parts = [w[..., a:b] for a, b in _SEGMENTS]
    return jnp.concatenate(parts + [jnp.zeros(w.shape[:-1] + (_NPAD,), w.dtype)], -1)


def _cparams(sem):
    return pltpu.CompilerParams(dimension_semantics=sem, vmem_limit_bytes=VMEM_LIMIT)


def _dot(a, b):
    return jnp.dot(a, b, preferred_element_type=F32)


def _dot_nt(a, b):
    return lax.dot_general(a, b, (((1,), (1,)), ((), ())), preferred_element_type=F32)


def _split3(x):
    hi = x.astype(BF16)
    r1 = x - hi.astype(F32)
    mid = r1.astype(BF16)
    lo = (r1 - mid.astype(F32)).astype(BF16)
    return hi, mid, lo


def _dot_exact_l(m01, x):
    hi, mid, lo = _split3(x)
    return _dot(m01, hi) + _dot(m01, mid) + _dot(m01, lo)


def _dot_exact_r(x, m01):
    hi, mid, lo = _split3(x)
    return _dot(hi, m01) + _dot(mid, m01) + _dot(lo, m01)


def _mod_kernel(c_ref, w_ref, b_ref, o_ref):
    s = jax.nn.silu(c_ref[...]).astype(BF16)
    o_ref[0] = _dot(s, w_ref[0].astype(BF16)) + b_ref[0]


def _mod_call(c_all, w_mod, b_mod):
    rows = c_all.shape[0]
    tn = 1536
    return pl.pallas_call(
        _mod_kernel,
        grid=(DEPTH, 6 * D // tn),
        in_specs=[pl.BlockSpec((rows, D), lambda l, j: (0, 0)),
                  pl.BlockSpec((1, D, tn), lambda l, j: (l, 0, j)),
                  pl.BlockSpec((1, 1, tn), lambda l, j: (l, 0, j))],
        out_specs=pl.BlockSpec((1, rows, tn), lambda l, j: (l, 0, j)),
        out_shape=jax.ShapeDtypeStruct((DEPTH, rows, 6 * D), F32),
        compiler_params=_cparams(("arbitrary", "arbitrary")),
        name="mod",
    )(c_all, w_mod, b_mod.reshape(DEPTH, 1, 6 * D))


def _inproj_kernel(x_ref, mod_ref, w_ref, b_ref, o_ref):
    m = mod_ref[0]
    u = x_ref[0] * (1.0 + m[:, D:2 * D]) + m[:, 0:D]
    o_ref[0] = _dot(u.astype(BF16), w_ref[...]) + b_ref[...]


def _inproj_call(x, mod3, w_p, b_p, shared_row):
    B, T, _ = x.shape
    tm = min(T, 512)
    if shared_row is None:
        mod_map = lambda b, i: (b, 0, 0)
    else:
        mod_map = lambda b, i: (shared_row, 0, 0)
    return pl.pallas_call(
        _inproj_kernel,
        grid=(B, T // tm),
        in_specs=[pl.BlockSpec((1, tm, D), lambda b, i: (b, i, 0)),
                  pl.BlockSpec((1, 1, 6 * D), mod_map),
                  pl.BlockSpec((D, PW), lambda b, i: (0, 0)),
                  pl.BlockSpec((1, PW), lambda b, i: (0, 0))],
        out_specs=pl.BlockSpec((1, tm, PW), lambda b, i: (b, i, 0)),
        out_shape=jax.ShapeDtypeStruct((B, T, PW), F32),
        compiler_params=_cparams(("arbitrary", "arbitrary")),
        name="inproj",
    )(x, mod3, w_p, b_p)


def _softmax_av(s_list, v_list):
    m = s_list[0].max(-1, keepdims=True)
    for s in s_list[1:]:
        m = jnp.maximum(m, s.max(-1, keepdims=True))
    acc = None
    l = None
    for s, v in zip(s_list, v_list):
        p = jnp.exp(s - m)
        ls = p.sum(-1, keepdims=True)
        o = _dot(p.astype(BF16), v)
        acc = o if acc is None else acc + o
        l = ls if l is None else l + ls
    return acc / l


NAT_GROUP = 8


def _natten_kernel(q_ref, k_ref, v_ref, qc_ref, kc_ref, vc_ref, bias_ref, o_ref, oc_ref, ks, vs, *, n_rows):
    ks[...] = k_ref[0].astype(BF16)
    vs[...] = v_ref[0].astype(BF16)
    kcb = kc_ref[0].astype(BF16)
    vcb = vc_ref[0].astype(BF16)
    lane = lax.broadcasted_iota(jnp.int32, (1, LANE), 1)
    head0 = lane < HEAD_DIM
    scale = HEAD_DIM ** -0.5

    def stack_heads(q):
        return jnp.concatenate([jnp.where(head0, q, 0.0), jnp.where(head0, 0.0, q)], 0).astype(BF16)

    def unstack(o, n):
        return jnp.where(head0, o[:n], o[n:])

    def rows_body(g, carry):
        koffs, scores = [], []
        for i in range(NAT_GROUP):
            r = g * NAT_GROUP + i
            rs = jnp.clip(r - WIN_ROWS // 2, 0, n_rows - WIN_ROWS)
            qs = stack_heads(q_ref[0, pl.ds(pl.multiple_of(r * GRID_W, GRID_W), GRID_W), :] * scale)
            koff = pl.multiple_of(rs * GRID_W, GRID_W)
            kl = ks[pl.ds(koff, WIN_ROWS * GRID_W), :]
            koffs.append(koff)
            scores.append([_dot_nt(qs, kl) + bias_ref[r - rs, 0], _dot_nt(qs, kcb)])
        for i in range(NAT_GROUP):
            r = g * NAT_GROUP + i
            vl = vs[pl.ds(koffs[i], WIN_ROWS * GRID_W), :]
            o = _softmax_av(scores[i], [vl, vcb])
            o_ref[0, pl.ds(pl.multiple_of(r * GRID_W, GRID_W), GRID_W), :] = unstack(o, GRID_W)
        return carry

    lax.fori_loop(0, n_rows // NAT_GROUP, rows_body, 0)

    tc = qc_ref.shape[1]
    oc = _softmax_av([_dot_nt(stack_heads(qc_ref[0] * scale), kcb)], [vcb])
    oc_ref[0] = unstack(oc, tc)


def _natten_call(p_lat, p_ctx, bias_l):
    B, T, _ = p_lat.shape
    Tc = p_ctx.shape[1]
    n_rows = T // GRID_W
    assert n_rows >= WIN_ROWS
    cb = lambda base: (lambda b, p: (b, 0, base // LANE + p))
    return pl.pallas_call(
        functools.partial(_natten_kernel, n_rows=n_rows),
        grid=(B, 4),
        in_specs=[pl.BlockSpec((1, T, LANE), cb(C_NQ)),
                  pl.BlockSpec((1, T, LANE), cb(C_NK)),
                  pl.BlockSpec((1, T, LANE), cb(C_NV)),
                  pl.BlockSpec((1, Tc, LANE), cb(C_NQ)),
                  pl.BlockSpec((1, Tc, LANE), cb(C_NK)),
                  pl.BlockSpec((1, Tc, LANE), cb(C_NV)),
                  pl.BlockSpec((WIN_ROWS, 1, 2 * GRID_W, WIN_ROWS * GRID_W), lambda b, p: (0, p, 0, 0))],
        out_specs=[pl.BlockSpec((1, T, LANE), lambda b, p: (b, 0, p)),
                   pl.BlockSpec((1, Tc, LANE), lambda b, p: (b, 0, p))],
        out_shape=[jax.ShapeDtypeStruct((B, T, 4 * LANE), F32),
                   jax.ShapeDtypeStruct((B, Tc, 4 * LANE), F32)],
        scratch_shapes=[pltpu.VMEM((T, LANE), BF16), pltpu.VMEM((T, LANE), BF16)],
        compiler_params=_cparams(("arbitrary", "arbitrary")),
        name="natten",
    )(p_lat, p_lat, p_lat, p_ctx, p_ctx, p_ctx, bias_l)


def _natten_bias_tables(rpb):
    col = np.arange(GRID_W)
    cstart = np.clip(col - WIN_COLS // 2, 0, GRID_W - WIN_COLS)
    col_ok = (col[None, :] >= cstart[:, None]) & (col[None, :] < cstart[:, None] + WIN_COLS)
    dc_idx = np.clip(col[None, :] - col[:, None] + WIN_COLS - 1, 0, 2 * WIN_COLS - 2)
    toe = rpb.astype(F32)[..., dc_idx]
    toe = jnp.where(col_ok[None, None, None], toe, NEG)
    t = jnp.stack([toe[:, :, WIN_ROWS - 1 - d:2 * WIN_ROWS - 1 - d] for d in range(WIN_ROWS)], 1)
    t = t.transpose(0, 1, 2, 4, 3, 5)
    return t.reshape(DEPTH, WIN_ROWS, 4, 2 * GRID_W, WIN_ROWS * GRID_W)


def _tri_masks():
    r = lax.broadcasted_iota(jnp.int32, (CHUNK, CHUNK), 0)
    c = lax.broadcasted_iota(jnp.int32, (CHUNK, CHUNK), 1)
    return r >= c, r <= c


def _rope(x, cs, sn, first, dist):
    w = x.shape[-1]
    partner = jnp.where(first, pltpu.roll(x, w - dist, 1), pltpu.roll(x, dist, 1))
    return x * cs + partner * sn


def _rope_tables(T, n_heads, half):
    t = jnp.arange(T)
    rows = (t // GRID_W).astype(F32)
    cols = (t % GRID_W).astype(F32)
    inv = ROPE_BASE ** (-jnp.arange(half, dtype=F32) / half)
    ar = rows[:, None] * inv[None, :]
    ac = cols[:, None] * inv[None, :]
    cos = jnp.concatenate([jnp.cos(ar), jnp.cos(ar), jnp.cos(ac), jnp.cos(ac)], -1)
    sin = jnp.concatenate([-jnp.sin(ar), jnp.sin(ar), -jnp.sin(ac), jnp.sin(ac)], -1)
    return jnp.tile(cos, (1, n_heads)), jnp.tile(sin, (1, n_heads))


def _bwd_chunk(c, nc, n):
    return jnp.where(c < nc, nc - 1 - c, n - 1 - c + nc)


def _mlstm_kernel(qkl_ref, vl_ref, sml_ref, qkc_ref, vc_ref, smc_ref, cos_ref, sin_ref, cw_ref,
                  ol_ref, oc_ref,
                  qt_s, k_s, vat_s, gcol_s, grow_s, brow_s, cm_s, ot_s, ck_s, *, nc, n):
    low, upp = _tri_masks()
    tri = jnp.where(low, 1.0, 0.0).astype(BF16)
    triu = jnp.where(upp, 1.0, 0.0).astype(BF16)
    lane128 = lax.broadcasted_iota(jnp.int32, (1, LANE), 1)
    lane256 = lax.broadcasted_iota(jnp.int32, (1, 2 * LANE), 1)
    rowi = lax.broadcasted_iota(jnp.int32, (CHUNK, 1), 0)
    row16 = lax.broadcasted_iota(jnp.int32, (16, 1), 0)
    first = (lane256 % 32) < 16
    hmask = [(lane256 >= HEAD_DIM * h) & (lane256 < HEAD_DIM * (h + 1)) for h in range(4)]
    hrows = [(rowi >= HEAD_DIM * h) & (rowi < HEAD_DIM * (h + 1)) for h in range(4)]
    ones_blk = jnp.where(lax.broadcasted_iota(jnp.int32, (HEAD_DIM, CHUNK), 0) == 0, 1.0, 0.0)
    fwd_rows = row16 < 8
    cw = cw_ref[...]

    def cummax_lanes(x, suffix):
        sh = 1
        while sh < CHUNK:
            if suffix:
                moved = jnp.where(lane256 < CHUNK - sh, pltpu.roll(x, CHUNK - sh, 1), NEG)
            else:
                moved = jnp.where(lane256 >= sh, pltpu.roll(x, sh, 1), NEG)
            x = jnp.maximum(x, moved)
            sh *= 2
        return x

    def prep(qk_ref, v_ref, sm_ref, ci, n_str, dst, use_rope):
        r0 = ci * CHUNK
        xc = qk_ref[0, r0:r0 + CHUNK, :]
        prev = qk_ref[0, r0 - 1:r0, :] if ci > 0 else jnp.zeros((1, 4 * LANE), F32)
        nxt = qk_ref[0, r0 + CHUNK:r0 + CHUNK + 1, :] if ci < n_str - 1 else jnp.zeros((1, 4 * LANE), F32)
        xp = jnp.where(rowi == 0, prev, pltpu.roll(xc, 1, 0))
        xn = jnp.where(rowi == CHUNK - 1, nxt, pltpu.roll(xc, CHUNK - 1, 0))
        y = jax.nn.silu(xp * cw[0:1] + xc * cw[1:2] + xn * cw[2:3])
        q = y[:, :2 * LANE]
        k = y[:, 2 * LANE:] * HEAD_DIM ** -0.5
        if use_rope:
            cs = cos_ref[r0:r0 + CHUNK, :]
            sn = sin_ref[r0:r0 + CHUNK, :]
            q = _rope(q, cs, sn, first, 16)
            k = _rope(k, cs, sn, first, 16)
        qt_s[dst] = q.T.astype(BF16)
        k_s[dst] = k.astype(BF16)
        vt = v_ref[0, r0:r0 + CHUNK, :].T
        for h in range(4):
            vat_s[h, dst] = jnp.concatenate([vt[HEAD_DIM * h:HEAD_DIM * (h + 1)], ones_blk], 0).astype(BF16)
        g = sm_ref[0, r0:r0 + CHUNK, :]
        lf = pltpu.roll(jax.nn.log_sigmoid(g), LANE - 4, 1)
        gcol_s[dst] = g - jnp.where(lane128 < 8, _dot_exact_l(tri, lf), _dot_exact_l(triu, lf))
        gt = g.T[0:16]
        lft = pltpu.roll(jax.nn.log_sigmoid(gt), 12, 0)
        brow = jnp.where(fwd_rows, _dot_exact_r(lft, triu), _dot_exact_r(lft, tri))
        brow_s[dst] = brow
        grow_s[dst] = gt - brow

    for ci in range(nc):
        prep(qkc_ref, vc_ref, smc_ref, ci, nc, ci, False)
    for ci in range(n - nc):
        prep(qkl_ref, vl_ref, sml_ref, ci, n - nc, nc + ci, True)
    g_all = grow_s[...].reshape(n * 16, CHUNK)
    fwd_all = (lax.broadcasted_iota(jnp.int32, (n * 16, 1), 0) % 16) < 8
    cm_s[...] = jnp.where(fwd_all, cummax_lanes(g_all, False), cummax_lanes(g_all, True)).reshape(n, 16, CHUNK)

    def scan_dir(bwd):
        di = 1 if bwd else 0
        gi = 8 if bwd else 0
        causal = low if bwd else upp

        def matmuls_first(c):
            ch = _bwd_chunk(c, nc, n) if bwd else c
            qt = qt_s[ch]
            kb = k_s[ch]
            ck_b = ck_s[di].astype(BF16)
            scores, inters = [], []
            for h in range(4):
                pr = slice(LANE * (h // 2), LANE * (h // 2 + 1))
                own = (lane128 < HEAD_DIM) if h % 2 == 0 else (lane128 >= HEAD_DIM)
                zero = jnp.zeros((), BF16)
                scores.append(_dot(jnp.where(own, kb[:, pr], zero), qt[pr, :]))
                inters.append(_dot(jnp.where(own, ck_b[:, pr], zero), qt[pr, :]))
            return ch, kb, scores, inters

        def rest(first, ms):
            ch, kb, scores, inters = first
            gcol_all = gcol_s[ch]
            new_ms = []
            for h in range(4):
                m = ms[h]
                g_row = grow_s[ch, gi + h:gi + h + 1, :]
                b_row = brow_s[ch, gi + h:gi + h + 1, :]
                a_row = jnp.maximum(m, cm_s[ch, gi + h:gi + h + 1, :])
                w = jnp.exp(jnp.where(causal, gcol_all[:, gi + h:gi + h + 1] - a_row, NEG))
                pt = (scores[h] * w).astype(BF16)
                vat = vat_s[h, ch]
                nd = _dot(vat, pt) + jnp.exp(m - a_row) * inters[h]
                den = nd[HEAD_DIM:HEAD_DIM + 1, :]
                ht = nd[0:HEAD_DIM] / jnp.maximum(jnp.abs(den), jnp.exp(-(b_row + a_row)))
                ot_s[di, ch, HEAD_DIM * h:HEAD_DIM * (h + 1), :] = ht
                bl = b_row[:, 0:1] if bwd else b_row[:, CHUNK - 1:CHUNK]
                lw_end = bl + g_row
                m_new = jnp.maximum(bl + m, lw_end.max(-1, keepdims=True))
                upd = _dot((vat * jnp.exp(lw_end - m_new)).astype(BF16), kb)
                ck_s[di] = jnp.where(hmask[h], jnp.exp(bl + m - m_new) * ck_s[di] + upd, ck_s[di])
                new_ms.append(m_new)
            return tuple(new_ms)

        return matmuls_first, rest

    ck_s[...] = jnp.zeros_like(ck_s)
    first_f, rest_f = scan_dir(False)
    first_b, rest_b = scan_dir(True)

    def chunk_body(c, carry):
        ff, fb = first_f(c), first_b(c)
        return rest_f(ff, carry[0]), rest_b(fb, carry[1])

    zeros4 = tuple(jnp.zeros((1, 1), F32) for _ in range(4))
    lax.fori_loop(0, n, chunk_body, (zeros4, zeros4))
    for ci in range(nc):
        oc_ref[0, ci * CHUNK:(ci + 1) * CHUNK, :] = (ot_s[0, ci] + ot_s[1, ci]).T
    for ci in range(n - nc):
        ol_ref[0, ci * CHUNK:(ci + 1) * CHUNK, :] = (ot_s[0, nc + ci] + ot_s[1, nc + ci]).T


def _mlstm_call(p_lat, p_ctx, cos, sin, conv_w_l):
    B, T, _ = p_lat.shape
    Tc = p_ctx.shape[1]
    nc, n = Tc // CHUNK, (Tc + T) // CHUNK
    cb = lambda base, w: (lambda b: (b, 0, base // w))
    return pl.pallas_call(
        functools.partial(_mlstm_kernel, nc=nc, n=n),
        grid=(B,),
        in_specs=[pl.BlockSpec((1, T, 512), cb(C_MQK, 512)),
                  pl.BlockSpec((1, T, 256), cb(C_MV, 256)),
                  pl.BlockSpec((1, T, LANE), cb(C_SM, LANE)),
                  pl.BlockSpec((1, Tc, 512), cb(C_MQK, 512)),
                  pl.BlockSpec((1, Tc, 256), cb(C_MV, 256)),
                  pl.BlockSpec((1, Tc, LANE), cb(C_SM, LANE)),
                  pl.BlockSpec((T, 256), lambda b: (0, 0)),
                  pl.BlockSpec((T, 256), lambda b: (0, 0)),
                  pl.BlockSpec((CONV_K, 512), lambda b: (0, 0))],
        out_specs=[pl.BlockSpec((1, T, 256), lambda b: (b, 0, 0)),
                   pl.BlockSpec((1, Tc, 256), lambda b: (b, 0, 0))],
        out_shape=[jax.ShapeDtypeStruct((B, T, 256), F32),
                   jax.ShapeDtypeStruct((B, Tc, 256), F32)],
        scratch_shapes=[pltpu.VMEM((n, 256, CHUNK), BF16),
                        pltpu.VMEM((n, CHUNK, 256), BF16),
                        pltpu.VMEM((4, n, LANE, CHUNK), BF16),
                        pltpu.VMEM((n, CHUNK, LANE), F32),
                        pltpu.VMEM((n, 16, CHUNK), F32),
                        pltpu.VMEM((n, 16, CHUNK), F32),
                        pltpu.VMEM((n, 16, CHUNK), F32),
                        pltpu.VMEM((2, n, 256, CHUNK), F32),
                        pltpu.VMEM((2, LANE, 256), F32)],
        compiler_params=_cparams(("arbitrary",)),
        name="mlstm",
    )(p_lat, p_lat, p_lat, p_ctx, p_ctx, p_ctx, cos, sin, conv_w_l)


GLA_BLK = 64
GLA_CLAMP = 80.0


def _gla_kernel(ql_ref, kl_ref, vl_ref, sml_ref, qc_ref, kc_ref, vc_ref, smc_ref, cos_ref, sin_ref, w2_ref, b2_ref,
                ol_ref, oc_ref,
                q_s, k_s, v_s, a_s, b_s, o_s, st_s, *, nc, n):
    low, upp = _tri_masks()
    tri = jnp.where(low, 1.0, 0.0).astype(BF16)
    triu = jnp.where(upp, 1.0, 0.0).astype(BF16)
    lane128 = lax.broadcasted_iota(jnp.int32, (1, LANE), 1)
    lane256 = lax.broadcasted_iota(jnp.int32, (1, 2 * LANE), 1)
    first = (lane128 % 16) < 8
    hm128 = [(lane128 >= 32 * h) & (lane128 < 32 * (h + 1)) for h in range(4)]
    hm256 = [(lane256 >= 64 * h) & (lane256 < 64 * (h + 1)) for h in range(4)]
    nb = CHUNK // GLA_BLK
    r_st = lax.broadcasted_iota(jnp.int32, (nb * GLA_BLK, CHUNK), 0)
    c_st = lax.broadcasted_iota(jnp.int32, (nb * GLA_BLK, CHUNK), 1)
    bd_r = lax.broadcasted_iota(jnp.int32, (LANE, 2 * LANE), 0)
    bd_c = lax.broadcasted_iota(jnp.int32, (LANE, 2 * LANE), 1)
    blockdiag = (bd_r // 32) == (bd_c // 64)

    def prep(q_ref, k_ref, v_ref, sm_ref, ci, dst, use_rope):
        r0 = ci * CHUNK
        q = q_ref[0, r0:r0 + CHUNK, :] * 32 ** -0.5
        k = k_ref[0, r0:r0 + CHUNK, :]
        if use_rope:
            cs = cos_ref[r0:r0 + CHUNK, :]
            sn = sin_ref[r0:r0 + CHUNK, :]
            q = _rope(q, cs, sn, first, 8)
            k = _rope(k, cs, sn, first, 8)
        q_s[dst] = q
        k_s[dst] = k
        v_s[dst] = v_ref[0, r0:r0 + CHUNK, :].astype(BF16)
        lr = sm_ref[0, r0:r0 + CHUNK, :].astype(BF16)
        for d in range(2):
            a = jax.nn.log_sigmoid(_dot(lr, w2_ref[d]) + b2_ref[d]) / GLA_TAU
            a_s[d, dst] = a
            b_s[d, dst] = _dot_exact_l(triu if d else tri, a)

    for ci in range(nc):
        prep(qc_ref, kc_ref, vc_ref, smc_ref, ci, ci, False)
    for ci in range(n - nc):
        prep(ql_ref, kl_ref, vl_ref, sml_ref, ci, nc + ci, True)

    def scan_dir(bwd):
        d = 1 if bwd else 0
        st_s[...] = jnp.zeros_like(st_s)

        def chunk_body(c, carry):
            ch = _bwd_chunk(c, nc, n) if bwd else c
            q = q_s[ch]
            k = k_s[ch]
            vb = v_s[ch]
            a = a_s[d, ch]
            b = b_s[d, ch]
            st_b = st_s[...].astype(BF16)
            o_inter = _dot((q * jnp.exp(b)).astype(BF16), st_b)
            atts = []
            for i in range(nb):
                rows = slice(i * GLA_BLK, (i + 1) * GLA_BLK)
                e = (i + 1) * GLA_BLK - 1 if bwd else i * GLA_BLK
                ref = b[e:e + 1, :] - a[e:e + 1, :]
                qs = q[rows] * jnp.exp(b[rows] - ref)
                ks = (k * jnp.exp(jnp.minimum(ref - b, GLA_CLAMP))).astype(BF16)
                lhs = jnp.concatenate([jnp.where(hm128[h], qs, 0.0) for h in range(4)], 0).astype(BF16)
                att = _dot_nt(lhs, ks)
                t_idx = (r_st % GLA_BLK) + i * GLA_BLK
                ok = (c_st >= t_idx) if bwd else (c_st <= t_idx)
                atts.append(jnp.where(ok, att, 0.0).astype(BF16))
            for i in range(nb):
                rows = slice(i * GLA_BLK, (i + 1) * GLA_BLK)
                oh = _dot(atts[i], vb)
                o_blk = o_inter[rows]
                for h in range(4):
                    o_blk = o_blk + jnp.where(hm256[h], oh[h * GLA_BLK:(h + 1) * GLA_BLK], 0.0)
                if bwd:
                    o_s[ch, rows, :] = o_s[ch, rows, :] + o_blk
                else:
                    o_s[ch, rows, :] = o_blk
            bt = b.T
            tot = bt[:, 0:1] if bwd else bt[:, CHUNK - 1:CHUNK]
            kt = (k.T * jnp.exp(tot - bt)).astype(BF16)
            upd = jnp.where(blockdiag, _dot(kt, vb), 0.0)
            st_s[...] = jnp.exp(tot) * st_s[...] + upd
            return carry

        lax.fori_loop(0, n, chunk_body, 0)

    scan_dir(False)
    scan_dir(True)
    for ci in range(nc):
        oc_ref[0, ci * CHUNK:(ci + 1) * CHUNK, :] = o_s[ci]
    for ci in range(n - nc):
        ol_ref[0, ci * CHUNK:(ci + 1) * CHUNK, :] = o_s[nc + ci]


def _gla_call(p_lat, p_ctx, cos, sin, w2p, b2p):
    B, T, _ = p_lat.shape
    Tc = p_ctx.shape[1]
    nc, n = Tc // CHUNK, (Tc + T) // CHUNK
    cb = lambda base, w: (lambda b: (b, 0, base // w))
    return pl.pallas_call(
        functools.partial(_gla_kernel, nc=nc, n=n),
        grid=(B,),
        in_specs=[pl.BlockSpec((1, T, LANE), cb(C_GQ, LANE)),
                  pl.BlockSpec((1, T, LANE), cb(C_GK, LANE)),
                  pl.BlockSpec((1, T, 256), cb(C_GV, 256)),
                  pl.BlockSpec((1, T, LANE), cb(C_SM, LANE)),
                  pl.BlockSpec((1, Tc, LANE), cb(C_GQ, LANE)),
                  pl.BlockSpec((1, Tc, LANE), cb(C_GK, LANE)),
                  pl.BlockSpec((1, Tc, 256), cb(C_GV, 256)),
                  pl.BlockSpec((1, Tc, LANE), cb(C_SM, LANE)),
                  pl.BlockSpec((T, LANE), lambda b: (0, 0)),
                  pl.BlockSpec((T, LANE), lambda b: (0, 0)),
                  pl.BlockSpec((2, LANE, LANE), lambda b: (0, 0, 0)),
                  pl.BlockSpec((2, 1, LANE), lambda b: (0, 0, 0))],
        out_specs=[pl.BlockSpec((1, T, 256), lambda b: (b, 0, 0)),
                   pl.BlockSpec((1, Tc, 256), lambda b: (b, 0, 0))],
        out_shape=[jax.ShapeDtypeStruct((B, T, 256), F32),
                   jax.ShapeDtypeStruct((B, Tc, 256), F32)],
        scratch_shapes=[pltpu.VMEM((n, CHUNK, LANE), F32),
                        pltpu.VMEM((n, CHUNK, LANE), F32),
                        pltpu.VMEM((n, CHUNK, 256), BF16),
                        pltpu.VMEM((2, n, CHUNK, LANE), F32),
                        pltpu.VMEM((2, n, CHUNK, LANE), F32),
                        pltpu.VMEM((n, CHUNK, 256), F32),
                        pltpu.VMEM((LANE, 256), F32)],
        compiler_params=_cparams(("arbitrary",)),
        name="gla",
    )(p_lat, p_lat, p_lat, p_lat, p_ctx, p_ctx, p_ctx, p_ctx, cos, sin, w2p, b2p)


def _gla_gate_weights(gla_w2, gla_b2):
    w = jnp.zeros((DEPTH, 2, LANE, LANE), F32)
    w = w.at[:, 0, 16:32].set(gla_w2[:, 0]).at[:, 1, 32:48].set(gla_w2[:, 1])
    return w.astype(BF16), gla_b2.astype(F32).reshape(DEPTH, 2, 1, LANE)


TILE_ROWS = D // LANE


def _store_token_tiles(ref, val):
    tm = val.shape[0]
    for s in range(TILE_ROWS):
        ref[0, pl.ds(s, tm, stride=TILE_ROWS), :] = val[:, s * LANE:(s + 1) * LANE]


def _load_token_tiles(ref, lead, t0, tm):
    return [ref[lead + (pl.ds(t0 * TILE_ROWS + s, tm, stride=TILE_ROWS), slice(None))] for s in range(TILE_ROWS)]


def _layer_norm(z, g, b):
    mu = z.mean(-1, keepdims=True)
    zc = z - mu
    var = jnp.mean(jnp.square(zc), -1, keepdims=True)
    return zc * lax.rsqrt(var + LN_EPS) * g + b


def _merge_kernel(hm_ref, hg_ref, hn_ref, mo_ref, gr_ref, x_ref, mod_ref, wout_ref, wr_ref, ng_ref, ln_ref,
                  x1_ref, u2_ref, lg_ref):
    r = lax.broadcasted_iota(jnp.int32, (256, 256), 0)
    c = lax.broadcasted_iota(jnp.int32, (256, 256), 1)
    avg = jnp.where((r // HEAD_DIM) == (c // HEAD_DIM), 1.0 / HEAD_DIM, 0.0).astype(BF16)

    def seg_mean(x):
        hi = x.astype(BF16)
        lo = (x - hi.astype(F32)).astype(BF16)
        return _dot(hi, avg) + _dot(lo, avg)

    def head_norm(h):
        d = h - seg_mean(h)
        return d * lax.rsqrt(seg_mean(d * d) + LN_EPS)

    ym = head_norm(hm_ref[0]) * ng_ref[0:1, :] * jax.nn.sigmoid(mo_ref[0])
    yg = head_norm(hg_ref[0]) * ng_ref[1:2, :] * jax.nn.silu(gr_ref[0])
    y = (_dot(ym.astype(BF16), wout_ref[0:256, :]) + _dot(yg.astype(BF16), wout_ref[256:512, :])
         + _dot(hn_ref[0].astype(BF16), wout_ref[512:1024, :]))
    m = mod_ref[0]
    x1 = _layer_norm(DEEPNORM_ALPHA * x_ref[0] + m[:, 2 * D:3 * D] * y, ln_ref[0:1, :], ln_ref[1:2, :])
    x1_ref[0] = x1
    u2 = x1 * (1.0 + m[:, 4 * D:5 * D]) + m[:, 3 * D:4 * D]
    _store_token_tiles(u2_ref, u2)
    lg_ref[0] = _dot_nt(wr_ref[...], u2.astype(BF16))


def _merge_call(hm, hg, hn, p, x, mod3, wout_b, wrt_b, ng, ln, shared_row):
    B, T, _ = x.shape
    tm = min(T, 512)
    mod_map = (lambda b, i: (b, 0, 0)) if shared_row is None else (lambda b, i: (shared_row, 0, 0))
    tok = lambda w: pl.BlockSpec((1, tm, w), lambda b, i: (b, i, 0))
    return pl.pallas_call(
        _merge_kernel,
        grid=(B, T // tm),
        in_specs=[tok(256), tok(256), tok(512),
                  pl.BlockSpec((1, tm, 256), lambda b, i: (b, i, C_MO // 256)),
                  pl.BlockSpec((1, tm, 256), lambda b, i: (b, i, C_GR // 256)),
                  tok(D),
                  pl.BlockSpec((1, 1, 6 * D), mod_map),
                  pl.BlockSpec((D, D), lambda b, i: (0, 0)),
                  pl.BlockSpec((N_EXPERTS, D), lambda b, i: (0, 0)),
                  pl.BlockSpec((2, 256), lambda b, i: (0, 0)),
                  pl.BlockSpec((2, D), lambda b, i: (0, 0))],
        out_specs=[tok(D), pl.BlockSpec((1, tm * TILE_ROWS, LANE), lambda b, i: (b, i, 0)),
                   pl.BlockSpec((1, N_EXPERTS, tm), lambda b, i: (b, 0, i))],
        out_shape=[jax.ShapeDtypeStruct((B, T, D), F32), jax.ShapeDtypeStruct((B, T * TILE_ROWS, LANE), F32),
                   jax.ShapeDtypeStruct((B, N_EXPERTS, T), F32)],
        compiler_params=_cparams(("arbitrary", "arbitrary")),
        name="merge",
    )(hm, hg, hn, p, p, x, mod3, wout_b, wrt_b, ng, ln)


ROUTER_SAMPLES = 8


def _router_samples(B):
    ns = ROUTER_SAMPLES
    while B % ns:
        ns -= 1
    return ns


def _router_kernel(lg_ref, o_ref, aff_s, sp_s, *, T, cap):
    J = T // LANE
    E = N_EXPERTS
    NS = lg_ref.shape[0]
    keys = [[] for _ in range(NS)]
    for j in range(J):
        for si in range(NS):
            lg = lg_ref[si, :, j * LANE:(j + 1) * LANE]
            ex = jnp.exp(lg - lg.max(0, keepdims=True))
            aff = ex / ex.sum(0, keepdims=True)
            aff_s[si, j * E:(j + 1) * E, :] = aff
            keys[si].append(pltpu.bitcast(aff, jnp.int32))

    def count(ks, pred):
        cnt = None
        for k in ks:
            cj = jnp.where(pred(k), 1.0, 0.0)
            cnt = cj if cnt is None else cnt + cj
        return cnt.sum(-1, keepdims=True)

    thrs = [jnp.zeros((E, 1), jnp.int32) for _ in range(NS)]
    for bit in range(30, -1, -1):
        for si in range(NS):
            cand = thrs[si] | (1 << bit)
            thrs[si] = jnp.where(count(keys[si], lambda k: k >= cand) >= cap, cand, thrs[si])

    r = lax.broadcasted_iota(jnp.int32, (LANE, LANE), 0)
    c = lax.broadcasted_iota(jnp.int32, (LANE, LANE), 1)
    upper = jnp.where(r <= c, 1.0, 0.0).astype(BF16)
    ones = jnp.ones((LANE, LANE), BF16)
    rr = lax.broadcasted_iota(jnp.int32, (J * E, J * E), 0)
    cc = lax.broadcasted_iota(jnp.int32, (J * E, J * E), 1)
    earlier = jnp.where(((rr % E) == (cc % E)) & ((cc // E) < (rr // E)), 1.0, 0.0).astype(BF16)

    def prefix(x01):
        xb = x01.astype(BF16)
        return _dot(xb, upper) + _dot(earlier, _dot(xb, ones).astype(BF16))

    for si in range(NS):
        thr = thrs[si]
        need = cap - count(keys[si], lambda k: k > thr)
        gt = jnp.concatenate([jnp.where(k > thr, 1.0, 0.0) for k in keys[si]], 0)
        eq = jnp.concatenate([jnp.where(k == thr, 1.0, 0.0) for k in keys[si]], 0)
        need_t = jnp.concatenate([need] * J, 0)
        sel = jnp.maximum(gt, jnp.where(prefix(eq) <= need_t, eq, 0.0))
        sp_s[si] = jnp.where(sel > 0.0, prefix(sel) - 1.0, -1.0)

    sb = min(cap, LANE)
    lane = lax.broadcasted_iota(jnp.int32, (1, LANE), 1)
    o_ref[...] = jnp.zeros_like(o_ref)
    for e in range(E):
        for half in range(cap // sb):
            slot = (lax.broadcasted_iota(jnp.int32, (sb, LANE), 0) + half * sb).astype(F32)
            rows = slice(half * sb, (half + 1) * sb)
            for si in range(NS):
                def jbody(j, acc):
                    acc_i, acc_g = acc
                    sp = sp_s[si, pl.ds(j * E + e, 1), :]
                    af = aff_s[si, pl.ds(j * E + e, 1), :]
                    hit = sp == slot
                    tid = (lane + j * LANE).astype(F32)
                    return jnp.where(hit, tid, acc_i), jnp.where(hit, af, acc_g)

                acc_i, acc_g = lax.fori_loop(0, J, jbody, (jnp.zeros((sb, LANE), F32), jnp.zeros((sb, LANE), F32)),
                                             unroll=2)
                icol = acc_i.sum(-1, keepdims=True)
                gcol = acc_g.sum(-1, keepdims=True)
                o_ref[si, rows, :] = jnp.where(lane == e, icol, jnp.where(lane == E + e, gcol, o_ref[si, rows, :]))


def _router_call(logits):
    B, _, T = logits.shape
    cap = CAPACITY_FACTOR * T // N_EXPERTS
    J = T // LANE
    ns = _router_samples(B)
    out = pl.pallas_call(
        functools.partial(_router_kernel, T=T, cap=cap),
        grid=(B // ns,),
        in_specs=[pl.BlockSpec((ns, N_EXPERTS, T), lambda b: (b, 0, 0))],
        out_specs=pl.BlockSpec((ns, cap, LANE), lambda b: (b, 0, 0)),
        out_shape=jax.ShapeDtypeStruct((B, cap, LANE), F32),
        scratch_shapes=[pltpu.VMEM((ns, J * N_EXPERTS, LANE), F32), pltpu.VMEM((ns, J * N_EXPERTS, LANE), F32)],
        compiler_params=_cparams(("arbitrary",)),
        name="router",
    )(logits)
    idx = out[:, :, :N_EXPERTS].astype(jnp.int32).transpose(0, 2, 1).reshape(B, 1, N_EXPERTS * cap)
    gate = out[:, :, N_EXPERTS:2 * N_EXPERTS].transpose(0, 2, 1).reshape(B, 1, N_EXPERTS * cap)
    return idx, gate


GATHER_BLOCK_BYTES = 4 * 1024 * 1024
SCATTER_BLOCK_BYTES = 2 * 1024 * 1024


def _experts_per_step(cap, block_bytes):
    eg = 2
    while eg < N_EXPERTS and 2 * eg * cap * D * 4 <= block_bytes:
        eg *= 2
    return eg


def _gather_kernel(idx_ref, u_ref, o_ref, *, cap, eg):
    g = pl.program_id(1)
    for e in range(eg):
        def body(s, carry):
            r = idx_ref[0, 0, (g * eg + e) * cap + s]
            src = pl.ds(pl.multiple_of(r * TILE_ROWS, TILE_ROWS), TILE_ROWS)
            o_ref[e, 0, pl.ds(pl.multiple_of(s * TILE_ROWS, TILE_ROWS), TILE_ROWS), :] = u_ref[0, src, :]
            return carry
        lax.fori_loop(0, cap, body, 0, unroll=8)


def _gather_call(idx, u2t):
    B = u2t.shape[0]
    T = u2t.shape[1] // TILE_ROWS
    cap = CAPACITY_FACTOR * T // N_EXPERTS
    eg = _experts_per_step(cap, GATHER_BLOCK_BYTES)
    smem = lambda: pl.BlockSpec((1, 1, N_EXPERTS * cap), lambda b, g: (b, 0, 0), memory_space=pltpu.SMEM)
    return pl.pallas_call(
        functools.partial(_gather_kernel, cap=cap, eg=eg),
        grid=(B, N_EXPERTS // eg),
        in_specs=[smem(), pl.BlockSpec((1, T * TILE_ROWS, LANE), lambda b, g: (b, 0, 0))],
        out_specs=pl.BlockSpec((eg, 1, cap * TILE_ROWS, LANE), lambda b, g: (g, b, 0, 0)),
        out_shape=jax.ShapeDtypeStruct((N_EXPERTS, B, cap * TILE_ROWS, LANE), F32),
        compiler_params=_cparams(("arbitrary", "arbitrary")),
        name="gather",
    )(idx, u2t)


FF_CHUNK = 512


def _ffn_kernel(x_ref, wg_ref, wu_ref, wd_ref, o_ref, xs, *, tm):
    for s, piece in enumerate(_load_token_tiles(x_ref, (0,), 0, tm)):
        xs[:, s * LANE:(s + 1) * LANE] = piece.astype(BF16)
    xb = xs[...]
    acc = None
    for c in range(EXPERT_FF // FF_CHUNK):
        cols = slice(c * FF_CHUNK, (c + 1) * FF_CHUNK)
        h = (jax.nn.silu(_dot(xb, wg_ref[0, :, cols])) * _dot(xb, wu_ref[0, :, cols])).astype(BF16)
        t = _dot(h, wd_ref[0, cols, :])
        acc = t if acc is None else acc + t
    _store_token_tiles(o_ref, acc)


def _ffn_call(xe, wg_b, wu_b, wd_b, layer):
    E = xe.shape[0]
    M = xe.shape[1] // TILE_ROWS
    tm = min(M, 512)
    w_map = lambda e, i: (layer * E + e, 0, 0)
    return pl.pallas_call(
        functools.partial(_ffn_kernel, tm=tm),
        grid=(E, M // tm),
        in_specs=[pl.BlockSpec((1, tm * TILE_ROWS, LANE), lambda e, i: (e, i, 0)),
                  pl.BlockSpec((1, D, EXPERT_FF), w_map),
                  pl.BlockSpec((1, D, EXPERT_FF), w_map),
                  pl.BlockSpec((1, EXPERT_FF, D), w_map)],
        out_specs=pl.BlockSpec((1, tm * TILE_ROWS, LANE), lambda e, i: (e, i, 0)),
        out_shape=jax.ShapeDtypeStruct((E, M * TILE_ROWS, LANE), F32),
        scratch_shapes=[pltpu.VMEM((tm, D), BF16)],
        compiler_params=_cparams(("arbitrary", "arbitrary")),
        name="ffn",
    )(xe, wg_b, wu_b, wd_b)


SCATTER_FIN_ROWS = 256


def _scatter_kernel(idx_ref, gate_ref, y_ref, x1_ref, mod_ref, ln_ref, o_ref, acc_a, acc_b, *, cap, eg, n_g, T):
    g = pl.program_id(1)

    @pl.when(g == 0)
    def _():
        acc_a[...] = jnp.zeros_like(acc_a)
        acc_b[...] = jnp.zeros_like(acc_b)

    for pair in range(eg // 2):
        def body(s, carry):
            p0 = (g * eg + 2 * pair) * cap + s
            p1 = p0 + cap
            src = pl.ds(pl.multiple_of(s * TILE_ROWS, TILE_ROWS), TILE_ROWS)
            d0 = pl.ds(pl.multiple_of(idx_ref[0, 0, p0] * TILE_ROWS, TILE_ROWS), TILE_ROWS)
            d1 = pl.ds(pl.multiple_of(idx_ref[0, 0, p1] * TILE_ROWS, TILE_ROWS), TILE_ROWS)
            acc_a[d0, :] = acc_a[d0, :] + y_ref[2 * pair, 0, src, :] * gate_ref[0, 0, p0]
            acc_b[d1, :] = acc_b[d1, :] + y_ref[2 * pair + 1, 0, src, :] * gate_ref[0, 0, p1]
            return carry
        lax.fori_loop(0, cap, body, 0, unroll=8)

    @pl.when(g == n_g - 1)
    def _():
        g2 = mod_ref[0][:, 5 * D:6 * D]
        tb = min(T, SCATTER_FIN_ROWS)
        for i in range(T // tb):
            rows = slice(i * tb, (i + 1) * tb)
            f = jnp.concatenate([pa + pb for pa, pb in zip(_load_token_tiles(acc_a, (), i * tb, tb),
                                                             _load_token_tiles(acc_b, (), i * tb, tb))], -1)
            z = DEEPNORM_ALPHA * x1_ref[0, rows, :] + g2 * f
            o_ref[0, rows, :] = _layer_norm(z, ln_ref[0:1, :], ln_ref[1:2, :])


def _scatter_call(idx, gate, y4, x1, mod3, ln, shared_row):
    B, T, _ = x1.shape
    cap = CAPACITY_FACTOR * T // N_EXPERTS
    eg = _experts_per_step(cap, SCATTER_BLOCK_BYTES)
    n_g = N_EXPERTS // eg
    mod_map = (lambda b, g: (b, 0, 0)) if shared_row is None else (lambda b, g: (shared_row, 0, 0))
    smem = lambda: pl.BlockSpec((1, 1, N_EXPERTS * cap), lambda b, g: (b, 0, 0), memory_space=pltpu.SMEM)
    return pl.pallas_call(
        functools.partial(_scatter_kernel, cap=cap, eg=eg, n_g=n_g, T=T),
        grid=(B, n_g),
        scratch_shapes=[pltpu.VMEM((T * TILE_ROWS, LANE), F32), pltpu.VMEM((T * TILE_ROWS, LANE), F32)],
        in_specs=[smem(), smem(),
                  pl.BlockSpec((eg, 1, cap * TILE_ROWS, LANE), lambda b, g: (g, b, 0, 0)),
                  pl.BlockSpec((1, T, D), lambda b, g: (b, 0, 0), pipeline_mode=pl.Buffered(1)),
                  pl.BlockSpec((1, 1, 6 * D), mod_map),
                  pl.BlockSpec((2, D), lambda b, g: (0, 0))],
        out_specs=pl.BlockSpec((1, T, D), lambda b, g: (b, 0, 0)),
        out_shape=jax.ShapeDtypeStruct((B, T, D), F32),
        compiler_params=_cparams(("arbitrary", "arbitrary")),
        name="scatter",
    )(idx, gate, y4, x1, mod3, ln)


def _moe(x1, u2t, logits, mod3, wg_b, wu_b, wd_b, layer, ln2, shared_row):
    B, T, _ = x1.shape
    cap = CAPACITY_FACTOR * T // N_EXPERTS
    idx, gate = _router_call(logits)
    xe = _gather_call(idx, u2t)
    y = _ffn_call(xe.reshape(N_EXPERTS, B * cap * TILE_ROWS, LANE), wg_b, wu_b, wd_b, layer)
    return _scatter_call(idx, gate, y.reshape(N_EXPERTS, B, cap * TILE_ROWS, LANE), x1, mod3, ln2, shared_row)


def kernel(x, c, ctx, c_ctx, w_mod, b_mod, w_in, b_in, conv_w, gla_w2, gla_b2, mlstm_norm_g, gla_norm_g, rpb, w_out, ln1_g, ln1_b, w_router, w_gate, w_up, w_down, ln2_g, ln2_b):
    B, T, _ = x.shape
    n_mod = -(-(B + 1) // 8) * 8
    c_all = jnp.concatenate([c, c_ctx[None], jnp.zeros((n_mod - B - 1, D), F32)], 0)
    mods = _mod_call(c_all, w_mod, b_mod)
    w_p = _repack_columns(w_in).astype(BF16)
    b_p = _repack_columns(b_in)
    bias = _natten_bias_tables(rpb)
    mcos, msin = _rope_tables(T, 4, 16)
    gcos, gsin = _rope_tables(T, 4, 8)
    w2p, b2p = _gla_gate_weights(gla_w2, gla_b2)
    wout_b = w_out.astype(BF16)
    wrt_b = jnp.swapaxes(w_router, 1, 2).astype(BF16)
    wg_b = w_gate.astype(BF16).reshape(DEPTH * N_EXPERTS, D, EXPERT_FF)
    wu_b = w_up.astype(BF16).reshape(DEPTH * N_EXPERTS, D, EXPERT_FF)
    wd_b = w_down.astype(BF16).reshape(DEPTH * N_EXPERTS, EXPERT_FF, D)
    ng = jnp.stack([mlstm_norm_g, gla_norm_g], 1)
    ln1 = jnp.stack([ln1_g, ln1_b], 1)
    ln2 = jnp.stack([ln2_g, ln2_b], 1)
    for l in range(DEPTH):
        mod3 = mods[l].reshape(n_mod, 1, 6 * D)
        p_lat = _inproj_call(x, mod3, w_p[l], b_p[l][None], None)
        p_ctx = _inproj_call(ctx, mod3, w_p[l], b_p[l][None], B)
        n_lat, n_ctx = _natten_call(p_lat, p_ctx, bias[l])
        m_lat, m_ctx = _mlstm_call(p_lat, p_ctx, mcos, msin, conv_w[l])
        g_lat, g_ctx = _gla_call(p_lat, p_ctx, gcos, gsin, w2p[l], b2p[l])
        x1, u2, lg = _merge_call(m_lat, g_lat, n_lat, p_lat, x, mod3, wout_b[l], wrt_b[l], ng[l], ln1[l], None)
        x = _moe(x1, u2, lg, mod3, wg_b, wu_b, wd_b, l, ln2[l], None)
        if l < DEPTH - 1:
            c1, u2c, lgc = _merge_call(m_ctx, g_ctx, n_ctx, p_ctx, ctx, mod3, wout_b[l], wrt_b[l], ng[l], ln1[l], B)
            ctx = _moe(c1, u2c, lgc, mod3, wg_b, wu_b, wd_b, l, ln2[l], B)
    return x
```

```python
import functools

import numpy as np
import jax
import jax.numpy as jnp
from jax import lax
from jax.experimental import pallas as pl
from jax.experimental.pallas import tpu as pltpu

F32 = jnp.float32
BF16 = jnp.bfloat16

D = 1024
DEPTH = 4
GRID_W = 64
HEAD_DIM = 64
WIN_ROWS = 8
WIN_COLS = 16
CONV_K = 3
ROPE_BASE = 10000.0
N_EXPERTS = 16
EXPERT_FF = 2 * D
CAPACITY_FACTOR = 2
LN_EPS = 1e-5
GLA_TAU = 16.0
GLA_RANK = 16
DEEPNORM_ALPHA = (2 * DEPTH) ** 0.25
NEG = -1e30

VMEM_LIMIT = 56 * 1024 * 1024
LANE = 128
CHUNK = 256

C_MQK, C_MV, C_MO, C_GV, C_GR = 0, 512, 768, 1024, 1280
C_GQ, C_GK, C_SM = 1536, 1664, 1792
PM = 1920
C_NQ, C_NK, C_NV = 0, 512, 1024
PN = 1536
PW = PM + PN
_ORIG = dict(m_q=0, m_k=256, m_v=512, m_o=768, m_g=1024, g_q=1040, g_k=1168, g_v=1296, g_r=1552,
             g_lr=1808, n_q=1840, n_k=2352, n_v=2864)
_MAIN_SEGMENTS = ((0, 512), (512, 768), (768, 1024), (1296, 1552), (1552, 1808), (1040, 1168), (1168, 1296),
                  (1024, 1040), (1808, 1840))
_NAT_SEGMENTS = ((1840, 2352), (2352, 2864), (2864, 3376))
_NPAD = PM - sum(b - a for a, b in _MAIN_SEGMENTS)


def _repack_columns(w):
    parts = ([w[..., a:b] for a, b in _MAIN_SEGMENTS] + [jnp.zeros(w.shape[:-1] + (_NPAD,), w.dtype)]
             + [w[..., a:b] for a, b in _NAT_SEGMENTS])
    return jnp.concatenate(parts, -1)


def _cparams(sem):
    return pltpu.CompilerParams(dimension_semantics=sem, vmem_limit_bytes=VMEM_LIMIT)


def _dot(a, b):
    return jnp.dot(a, b, preferred_element_type=F32)


def _dot_nt(a, b):
    return lax.dot_general(a, b, (((1,), (1,)), ((), ())), preferred_element_type=F32)


def _split3(x):
    hi = x.astype(BF16)
    r1 = x - hi.astype(F32)
    mid = r1.astype(BF16)
    lo = (r1 - mid.astype(F32)).astype(BF16)
    return hi, mid, lo


def _dot_exact_l(m01, x):
    hi, mid, lo = _split3(x)
    return _dot(m01, hi) + _dot(m01, mid) + _dot(m01, lo)


def _dot_exact_r(x, m01):
    hi, mid, lo = _split3(x)
    return _dot(hi, m01) + _dot(mid, m01) + _dot(lo, m01)


def _mod_kernel(c_ref, w_ref, b_ref, o_ref):
    s = jax.nn.silu(c_ref[...]).astype(BF16)
    o_ref[0] = _dot(s, w_ref[0].astype(BF16)) + b_ref[0]


def _mod_call(c_all, w_mod, b_mod):
    rows = c_all.shape[0]
    tn = 1536
    return pl.pallas_call(
        _mod_kernel,
        grid=(DEPTH, 6 * D // tn),
        in_specs=[pl.BlockSpec((rows, D), lambda l, j: (0, 0)),
                  pl.BlockSpec((1, D, tn), lambda l, j: (l, 0, j)),
                  pl.BlockSpec((1, 1, tn), lambda l, j: (l, 0, j))],
        out_specs=pl.BlockSpec((1, rows, tn), lambda l, j: (l, 0, j)),
        out_shape=jax.ShapeDtypeStruct((DEPTH, rows, 6 * D), F32),
        compiler_params=_cparams(("arbitrary", "arbitrary")),
        name="mod",
    )(c_all, w_mod, b_mod.reshape(DEPTH, 1, 6 * D))


def _inproj_kernel(x_ref, mod_ref, w_ref, b_ref, o_ref, on_ref):
    m = mod_ref[0]
    u = (x_ref[0] * (1.0 + m[:, D:2 * D]) + m[:, 0:D]).astype(BF16)
    o_ref[0] = _dot(u, w_ref[:, :PM]) + b_ref[:, :PM]
    on_ref[0] = (_dot(u, w_ref[:, PM:]) + b_ref[:, PM:]).astype(BF16)


def _inproj_call(x, mod3, w_p, b_p, shared_row):
    B, T, _ = x.shape
    tm = min(T, 512)
    if shared_row is None:
        mod_map = lambda b, i: (b, 0, 0)
    else:
        mod_map = lambda b, i: (shared_row, 0, 0)
    return pl.pallas_call(
        _inproj_kernel,
        grid=(B, T // tm),
        in_specs=[pl.BlockSpec((1, tm, D), lambda b, i: (b, i, 0)),
                  pl.BlockSpec((1, 1, 6 * D), mod_map),
                  pl.BlockSpec((D, PW), lambda b, i: (0, 0)),
                  pl.BlockSpec((1, PW), lambda b, i: (0, 0))],
        out_specs=[pl.BlockSpec((1, tm, PM), lambda b, i: (b, i, 0)),
                   pl.BlockSpec((1, tm, PN), lambda b, i: (b, i, 0))],
        out_shape=[jax.ShapeDtypeStruct((B, T, PM), F32), jax.ShapeDtypeStruct((B, T, PN), BF16)],
        compiler_params=_cparams(("arbitrary", "arbitrary")),
        name="inproj",
    )(x, mod3, w_p, b_p)


def _softmax_av(s_list, v_list):
    m = s_list[0].max(-1, keepdims=True)
    for s in s_list[1:]:
        m = jnp.maximum(m, s.max(-1, keepdims=True))
    acc = None
    l = None
    for s, v in zip(s_list, v_list):
        p = jnp.exp(s - m)
        ls = p.sum(-1, keepdims=True)
        o = _dot(p.astype(BF16), v)
        acc = o if acc is None else acc + o
        l = ls if l is None else l + ls
    return acc / l


NAT_GROUP = 8


def _natten_kernel(q_ref, k_ref, v_ref, qc_ref, kc_ref, vc_ref, bias_ref, o_ref, oc_ref, *, n_rows):
    kcb = kc_ref[0]
    vcb = vc_ref[0]
    lane = lax.broadcasted_iota(jnp.int32, (1, LANE), 1)
    head0 = lane < HEAD_DIM
    scale = jnp.asarray(HEAD_DIM ** -0.5, BF16)
    zero = jnp.zeros((), BF16)

    def stack_heads(q):
        return jnp.concatenate([jnp.where(head0, q, zero), jnp.where(head0, zero, q)], 0)

    def unstack(o, n):
        return jnp.where(head0, o[:n], o[n:])

    def rows_body(g, carry):
        koffs, scores = [], []
        for i in range(NAT_GROUP):
            r = g * NAT_GROUP + i
            rs = jnp.clip(r - WIN_ROWS // 2, 0, n_rows - WIN_ROWS)
            qs = stack_heads(q_ref[0, pl.ds(pl.multiple_of(r * GRID_W, GRID_W), GRID_W), :] * scale)
            koff = pl.multiple_of(rs * GRID_W, GRID_W)
            kl = k_ref[0, pl.ds(koff, WIN_ROWS * GRID_W), :]
            koffs.append(koff)
            scores.append([_dot_nt(qs, kl) + bias_ref[r - rs, 0], _dot_nt(qs, kcb)])
        for i in range(NAT_GROUP):
            r = g * NAT_GROUP + i
            vl = v_ref[0, pl.ds(koffs[i], WIN_ROWS * GRID_W), :]
            o = _softmax_av(scores[i], [vl, vcb])
            o_ref[0, pl.ds(pl.multiple_of(r * GRID_W, GRID_W), GRID_W), :] = unstack(o, GRID_W)
        return carry

    lax.fori_loop(0, n_rows // NAT_GROUP, rows_body, 0)

    tc = qc_ref.shape[1]
    oc = _softmax_av([_dot_nt(stack_heads(qc_ref[0] * scale), kcb)], [vcb])
    oc_ref[0] = unstack(oc, tc)


def _natten_call(pn_lat, pn_ctx, bias_l):
    B, T, _ = pn_lat.shape
    Tc = pn_ctx.shape[1]
    n_rows = T // GRID_W
    assert n_rows >= WIN_ROWS
    cb = lambda base: (lambda b, p: (b, 0, base // LANE + p))
    return pl.pallas_call(
        functools.partial(_natten_kernel, n_rows=n_rows),
        grid=(B, 4),
        in_specs=[pl.BlockSpec((1, T, LANE), cb(C_NQ)),
                  pl.BlockSpec((1, T, LANE), cb(C_NK)),
                  pl.BlockSpec((1, T, LANE), cb(C_NV)),
                  pl.BlockSpec((1, Tc, LANE), cb(C_NQ)),
                  pl.BlockSpec((1, Tc, LANE), cb(C_NK)),
                  pl.BlockSpec((1, Tc, LANE), cb(C_NV)),
                  pl.BlockSpec((WIN_ROWS, 1, 2 * GRID_W, WIN_ROWS * GRID_W), lambda b, p: (0, p, 0, 0))],
        out_specs=[pl.BlockSpec((1, T, LANE), lambda b, p: (b, 0, p)),
                   pl.BlockSpec((1, Tc, LANE), lambda b, p: (b, 0, p))],
        out_shape=[jax.ShapeDtypeStruct((B, T, 4 * LANE), F32),
                   jax.ShapeDtypeStruct((B, Tc, 4 * LANE), F32)],
        compiler_params=_cparams(("arbitrary", "arbitrary")),
        name="natten",
    )(pn_lat, pn_lat, pn_lat, pn_ctx, pn_ctx, pn_ctx, bias_l)


def _natten_bias_tables(rpb):
    col = np.arange(GRID_W)
    cstart = np.clip(col - WIN_COLS // 2, 0, GRID_W - WIN_COLS)
    col_ok = (col[None, :] >= cstart[:, None]) & (col[None, :] < cstart[:, None] + WIN_COLS)
    dc_idx = np.clip(col[None, :] - col[:, None] + WIN_COLS - 1, 0, 2 * WIN_COLS - 2)
    toe = rpb.astype(F32)[..., dc_idx]
    toe = jnp.where(col_ok[None, None, None], toe, NEG)
    t = jnp.stack([toe[:, :, WIN_ROWS - 1 - d:2 * WIN_ROWS - 1 - d] for d in range(WIN_ROWS)], 1)
    t = t.transpose(0, 1, 2, 4, 3, 5)
    return t.reshape(DEPTH, WIN_ROWS, 4, 2 * GRID_W, WIN_ROWS * GRID_W)


def _tri_masks():
    r = lax.broadcasted_iota(jnp.int32, (CHUNK, CHUNK), 0)
    c = lax.broadcasted_iota(jnp.int32, (CHUNK, CHUNK), 1)
    return r >= c, r <= c


def _rope(x, cs, sn, first, dist):
    w = x.shape[-1]
    partner = jnp.where(first, pltpu.roll(x, w - dist, 1), pltpu.roll(x, dist, 1))
    return x * cs + partner * sn


def _rope_tables(T, n_heads, half):
    t = jnp.arange(T)
    rows = (t // GRID_W).astype(F32)
    cols = (t % GRID_W).astype(F32)
    inv = ROPE_BASE ** (-jnp.arange(half, dtype=F32) / half)
    ar = rows[:, None] * inv[None, :]
    ac = cols[:, None] * inv[None, :]
    cos = jnp.concatenate([jnp.cos(ar), jnp.cos(ar), jnp.cos(ac), jnp.cos(ac)], -1)
    sin = jnp.concatenate([-jnp.sin(ar), jnp.sin(ar), -jnp.sin(ac), jnp.sin(ac)], -1)
    return jnp.tile(cos, (1, n_heads)), jnp.tile(sin, (1, n_heads))


def _bwd_chunk(c, nc, n):
    return jnp.where(c < nc, nc - 1 - c, n - 1 - c + nc)


def _mlstm_kernel(qkl_ref, vl_ref, sml_ref, qkc_ref, vc_ref, smc_ref, cos_ref, sin_ref, cw_ref,
                  ol_ref, oc_ref,
                  qt_s, k_s, vat_s, gcol_s, grow_s, brow_s, cm_s, ot_s, ck_s, *, nc, n):
    low, upp = _tri_masks()
    tri = jnp.where(low, 1.0, 0.0).astype(BF16)
    triu = jnp.where(upp, 1.0, 0.0).astype(BF16)
    lane128 = lax.broadcasted_iota(jnp.int32, (1, LANE), 1)
    lane256 = lax.broadcasted_iota(jnp.int32, (1, 2 * LANE), 1)
    rowi = lax.broadcasted_iota(jnp.int32, (CHUNK, 1), 0)
    row16 = lax.broadcasted_iota(jnp.int32, (16, 1), 0)
    first = (lane256 % 32) < 16
    hmask = [(lane256 >= HEAD_DIM * h) & (lane256 < HEAD_DIM * (h + 1)) for h in range(4)]
    hrows = [(rowi >= HEAD_DIM * h) & (rowi < HEAD_DIM * (h + 1)) for h in range(4)]
    ones_blk = jnp.where(lax.broadcasted_iota(jnp.int32, (HEAD_DIM, CHUNK), 0) == 0, 1.0, 0.0)
    fwd_rows = row16 < 8
    cw = cw_ref[...]

    def cummax_lanes(x, suffix):
        sh = 1
        while sh < CHUNK:
            if suffix:
                moved = jnp.where(lane256 < CHUNK - sh, pltpu.roll(x, CHUNK - sh, 1), NEG)
            else:
                moved = jnp.where(lane256 >= sh, pltpu.roll(x, sh, 1), NEG)
            x = jnp.maximum(x, moved)
            sh *= 2
        return x

    def prep(qk_ref, v_ref, sm_ref, ci, n_str, dst, use_rope):
        r0 = ci * CHUNK
        xc = qk_ref[0, r0:r0 + CHUNK, :]
        prev = qk_ref[0, r0 - 1:r0, :] if ci > 0 else jnp.zeros((1, 4 * LANE), F32)
        nxt = qk_ref[0, r0 + CHUNK:r0 + CHUNK + 1, :] if ci < n_str - 1 else jnp.zeros((1, 4 * LANE), F32)
        xp = jnp.where(rowi == 0, prev, pltpu.roll(xc, 1, 0))
        xn = jnp.where(rowi == CHUNK - 1, nxt, pltpu.roll(xc, CHUNK - 1, 0))
        y = jax.nn.silu(xp * cw[0:1] + xc * cw[1:2] + xn * cw[2:3])
        q = y[:, :2 * LANE]
        k = y[:, 2 * LANE:] * HEAD_DIM ** -0.5
        if use_rope:
            cs = cos_ref[r0:r0 + CHUNK, :]
            sn = sin_ref[r0:r0 + CHUNK, :]
            q = _rope(q, cs, sn, first, 16)
            k = _rope(k, cs, sn, first, 16)
        qt_s[dst] = q.T.astype(BF16)
        k_s[dst] = k.astype(BF16)
        vt = v_ref[0, r0:r0 + CHUNK, :].T
        for h in range(4):
            vat_s[h, dst] = jnp.concatenate([vt[HEAD_DIM * h:HEAD_DIM * (h + 1)], ones_blk], 0).astype(BF16)
        g = sm_ref[0, r0:r0 + CHUNK, :]
        lf = pltpu.roll(jax.nn.log_sigmoid(g), LANE - 4, 1)
        gcol_s[dst] = g - jnp.where(lane128 < 8, _dot_exact_l(tri, lf), _dot_exact_l(triu, lf))
        gt = g.T[0:16]
        lft = pltpu.roll(jax.nn.log_sigmoid(gt), 12, 0)
        brow = jnp.where(fwd_rows, _dot_exact_r(lft, triu), _dot_exact_r(lft, tri))
        brow_s[dst] = brow
        grow_s[dst] = gt - brow

    for ci in range(nc):
        prep(qkc_ref, vc_ref, smc_ref, ci, nc, ci, False)
    for ci in range(n - nc):
        prep(qkl_ref, vl_ref, sml_ref, ci, n - nc, nc + ci, True)
    g_all = grow_s[...].reshape(n * 16, CHUNK)
    fwd_all = (lax.broadcasted_iota(jnp.int32, (n * 16, 1), 0) % 16) < 8
    cm_s[...] = jnp.where(fwd_all, cummax_lanes(g_all, False), cummax_lanes(g_all, True)).reshape(n, 16, CHUNK)

    def scan_dir(bwd):
        di = 1 if bwd else 0
        gi = 8 if bwd else 0
        causal = low if bwd else upp

        def matmuls_first(c):
            ch = _bwd_chunk(c, nc, n) if bwd else c
            qt = qt_s[ch]
            kb = k_s[ch]
            ck_b = ck_s[di].astype(BF16)
            scores, inters = [], []
            for h in range(4):
                pr = slice(LANE * (h // 2), LANE * (h // 2 + 1))
                own = (lane128 < HEAD_DIM) if h % 2 == 0 else (lane128 >= HEAD_DIM)
                zero = jnp.zeros((), BF16)
                scores.append(_dot(jnp.where(own, kb[:, pr], zero), qt[pr, :]))
                inters.append(_dot(jnp.where(own, ck_b[:, pr], zero), qt[pr, :]))
            return ch, kb, scores, inters

        def rest(first, ms):
            ch, kb, scores, inters = first
            gcol_all = gcol_s[ch]
            new_ms = []
            for h in range(4):
                m = ms[h]
                g_row = grow_s[ch, gi + h:gi + h + 1, :]
                b_row = brow_s[ch, gi + h:gi + h + 1, :]
                a_row = jnp.maximum(m, cm_s[ch, gi + h:gi + h + 1, :])
                w = jnp.exp(jnp.where(causal, gcol_all[:, gi + h:gi + h + 1] - a_row, NEG))
                pt = (scores[h] * w).astype(BF16)
                vat = vat_s[h, ch]
                nd = _dot(vat, pt) + jnp.exp(m - a_row) * inters[h]
                den = nd[HEAD_DIM:HEAD_DIM + 1, :]
                ht = nd[0:HEAD_DIM] / jnp.maximum(jnp.abs(den), jnp.exp(-(b_row + a_row)))
                ot_s[di, ch, HEAD_DIM * h:HEAD_DIM * (h + 1), :] = ht
                bl = b_row[:, 0:1] if bwd else b_row[:, CHUNK - 1:CHUNK]
                lw_end = bl + g_row
                m_new = jnp.maximum(bl + m, lw_end.max(-1, keepdims=True))
                upd = _dot((vat * jnp.exp(lw_end - m_new)).astype(BF16), kb)
                ck_s[di] = jnp.where(hmask[h], jnp.exp(bl + m - m_new) * ck_s[di] + upd, ck_s[di])
                new_ms.append(m_new)
            return tuple(new_ms)

        return matmuls_first, rest

    ck_s[...] = jnp.zeros_like(ck_s)
    first_f, rest_f = scan_dir(False)
    first_b, rest_b = scan_dir(True)

    def chunk_body(c, carry):
        ff, fb = first_f(c), first_b(c)
        return rest_f(ff, carry[0]), rest_b(fb, carry[1])

    zeros4 = tuple(jnp.zeros((1, 1), F32) for _ in range(4))
    lax.fori_loop(0, n, chunk_body, (zeros4, zeros4))
    for ci in range(nc):
        oc_ref[0, ci * CHUNK:(ci + 1) * CHUNK, :] = (ot_s[0, ci] + ot_s[1, ci]).T
    for ci in range(n - nc):
        ol_ref[0, ci * CHUNK:(ci + 1) * CHUNK, :] = (ot_s[0, nc + ci] + ot_s[1, nc + ci]).T


def _mlstm_call(p_lat, p_ctx, cos, sin, conv_w_l):
    B, T, _ = p_lat.shape
    Tc = p_ctx.shape[1]
    nc, n = Tc // CHUNK, (Tc + T) // CHUNK
    cb = lambda base, w: (lambda b: (b, 0, base // w))
    return pl.pallas_call(
        functools.partial(_mlstm_kernel, nc=nc, n=n),
        grid=(B,),
        in_specs=[pl.BlockSpec((1, T, 512), cb(C_MQK, 512)),
                  pl.BlockSpec((1, T, 256), cb(C_MV, 256)),
                  pl.BlockSpec((1, T, LANE), cb(C_SM, LANE)),
                  pl.BlockSpec((1, Tc, 512), cb(C_MQK, 512)),
                  pl.BlockSpec((1, Tc, 256), cb(C_MV, 256)),
                  pl.BlockSpec((1, Tc, LANE), cb(C_SM, LANE)),
                  pl.BlockSpec((T, 256), lambda b: (0, 0)),
                  pl.BlockSpec((T, 256), lambda b: (0, 0)),
                  pl.BlockSpec((CONV_K, 512), lambda b: (0, 0))],
        out_specs=[pl.BlockSpec((1, T, 256), lambda b: (b, 0, 0)),
                   pl.BlockSpec((1, Tc, 256), lambda b: (b, 0, 0))],
        out_shape=[jax.ShapeDtypeStruct((B, T, 256), F32),
                   jax.ShapeDtypeStruct((B, Tc, 256), F32)],
        scratch_shapes=[pltpu.VMEM((n, 256, CHUNK), BF16),
                        pltpu.VMEM((n, CHUNK, 256), BF16),
                        pltpu.VMEM((4, n, LANE, CHUNK), BF16),
                        pltpu.VMEM((n, CHUNK, LANE), F32),
                        pltpu.VMEM((n, 16, CHUNK), F32),
                        pltpu.VMEM((n, 16, CHUNK), F32),
                        pltpu.VMEM((n, 16, CHUNK), F32),
                        pltpu.VMEM((2, n, 256, CHUNK), F32),
                        pltpu.VMEM((2, LANE, 256), F32)],
        compiler_params=_cparams(("arbitrary",)),
        name="mlstm",
    )(p_lat, p_lat, p_lat, p_ctx, p_ctx, p_ctx, cos, sin, conv_w_l)


GLA_BLK = 64
GLA_CLAMP = 80.0


def _gla_kernel(ql_ref, kl_ref, vl_ref, sml_ref, qc_ref, kc_ref, vc_ref, smc_ref, cos_ref, sin_ref, w2_ref, b2_ref,
                ol_ref, oc_ref,
                q_s, k_s, v_s, a_s, b_s, o_s, st_s, *, nc, n):
    low, upp = _tri_masks()
    tri = jnp.where(low, 1.0, 0.0).astype(BF16)
    triu = jnp.where(upp, 1.0, 0.0).astype(BF16)
    lane128 = lax.broadcasted_iota(jnp.int32, (1, LANE), 1)
    lane256 = lax.broadcasted_iota(jnp.int32, (1, 2 * LANE), 1)
    first = (lane128 % 16) < 8
    hm128 = [(lane128 >= 32 * h) & (lane128 < 32 * (h + 1)) for h in range(4)]
    hm256 = [(lane256 >= 64 * h) & (lane256 < 64 * (h + 1)) for h in range(4)]
    nb = CHUNK // GLA_BLK
    r_st = lax.broadcasted_iota(jnp.int32, (nb * GLA_BLK, CHUNK), 0)
    c_st = lax.broadcasted_iota(jnp.int32, (nb * GLA_BLK, CHUNK), 1)
    bd_r = lax.broadcasted_iota(jnp.int32, (LANE, 2 * LANE), 0)
    bd_c = lax.broadcasted_iota(jnp.int32, (LANE, 2 * LANE), 1)
    blockdiag = (bd_r // 32) == (bd_c // 64)

    def prep(q_ref, k_ref, v_ref, sm_ref, ci, dst, use_rope):
        r0 = ci * CHUNK
        q = q_ref[0, r0:r0 + CHUNK, :] * 32 ** -0.5
        k = k_ref[0, r0:r0 + CHUNK, :]
        if use_rope:
            cs = cos_ref[r0:r0 + CHUNK, :]
            sn = sin_ref[r0:r0 + CHUNK, :]
            q = _rope(q, cs, sn, first, 8)
            k = _rope(k, cs, sn, first, 8)
        q_s[dst] = q
        k_s[dst] = k
        v_s[dst] = v_ref[0, r0:r0 + CHUNK, :].astype(BF16)
        lr = sm_ref[0, r0:r0 + CHUNK, :].astype(BF16)
        for d in range(2):
            a = jax.nn.log_sigmoid(_dot(lr, w2_ref[d]) + b2_ref[d]) / GLA_TAU
            a_s[d, dst] = a
            b_s[d, dst] = _dot_exact_l(triu if d else tri, a)

    for ci in range(nc):
        prep(qc_ref, kc_ref, vc_ref, smc_ref, ci, ci, False)
    for ci in range(n - nc):
        prep(ql_ref, kl_ref, vl_ref, sml_ref, ci, nc + ci, True)

    def scan_dir(bwd):
        d = 1 if bwd else 0
        st_s[...] = jnp.zeros_like(st_s)

        def chunk_body(c, carry):
            ch = _bwd_chunk(c, nc, n) if bwd else c
            q = q_s[ch]
            k = k_s[ch]
            vb = v_s[ch]
            a = a_s[d, ch]
            b = b_s[d, ch]
            st_b = st_s[...].astype(BF16)
            o_inter = _dot((q * jnp.exp(b)).astype(BF16), st_b)
            atts = []
            for i in range(nb):
                rows = slice(i * GLA_BLK, (i + 1) * GLA_BLK)
                e = (i + 1) * GLA_BLK - 1 if bwd else i * GLA_BLK
                ref = b[e:e + 1, :] - a[e:e + 1, :]
                qs = q[rows] * jnp.exp(b[rows] - ref)
                ks = (k * jnp.exp(jnp.minimum(ref - b, GLA_CLAMP))).astype(BF16)
                lhs = jnp.concatenate([jnp.where(hm128[h], qs, 0.0) for h in range(4)], 0).astype(BF16)
                att = _dot_nt(lhs, ks)
                t_idx = (r_st % GLA_BLK) + i * GLA_BLK
                ok = (c_st >= t_idx) if bwd else (c_st <= t_idx)
                atts.append(jnp.where(ok, att, 0.0).astype(BF16))
            for i in range(nb):
                rows = slice(i * GLA_BLK, (i + 1) * GLA_BLK)
                oh = _dot(atts[i], vb)
                o_blk = o_inter[rows]
                for h in range(4):
                    o_blk = o_blk + jnp.where(hm256[h], oh[h * GLA_BLK:(h + 1) * GLA_BLK], 0.0)
                if bwd:
                    o_s[ch, rows, :] = o_s[ch, rows, :] + o_blk
                else:
                    o_s[ch, rows, :] = o_blk
            bt = b.T
            tot = bt[:, 0:1] if bwd else bt[:, CHUNK - 1:CHUNK]
            kt = (k.T * jnp.exp(tot - bt)).astype(BF16)
            upd = jnp.where(blockdiag, _dot(kt, vb), 0.0)
            st_s[...] = jnp.exp(tot) * st_s[...] + upd
            return carry

        lax.fori_loop(0, n, chunk_body, 0)

    scan_dir(False)
    scan_dir(True)
    for ci in range(nc):
        oc_ref[0, ci * CHUNK:(ci + 1) * CHUNK, :] = o_s[ci]
    for ci in range(n - nc):
        ol_ref[0, ci * CHUNK:(ci + 1) * CHUNK, :] = o_s[nc + ci]


def _gla_call(p_lat, p_ctx, cos, sin, w2p, b2p):
    B, T, _ = p_lat.shape
    Tc = p_ctx.shape[1]
    nc, n = Tc // CHUNK, (Tc + T) // CHUNK
    cb = lambda base, w: (lambda b: (b, 0, base // w))
    return pl.pallas_call(
        functools.partial(_gla_kernel, nc=nc, n=n),
        grid=(B,),
        in_specs=[pl.BlockSpec((1, T, LANE), cb(C_GQ, LANE)),
                  pl.BlockSpec((1, T, LANE), cb(C_GK, LANE)),
                  pl.BlockSpec((1, T, 256), cb(C_GV, 256)),
                  pl.BlockSpec((1, T, LANE), cb(C_SM, LANE)),
                  pl.BlockSpec((1, Tc, LANE), cb(C_GQ, LANE)),
                  pl.BlockSpec((1, Tc, LANE), cb(C_GK, LANE)),
                  pl.BlockSpec((1, Tc, 256), cb(C_GV, 256)),
                  pl.BlockSpec((1, Tc, LANE), cb(C_SM, LANE)),
                  pl.BlockSpec((T, LANE), lambda b: (0, 0)),
                  pl.BlockSpec((T, LANE), lambda b: (0, 0)),
                  pl.BlockSpec((2, LANE, LANE), lambda b: (0, 0, 0)),
                  pl.BlockSpec((2, 1, LANE), lambda b: (0, 0, 0))],
        out_specs=[pl.BlockSpec((1, T, 256), lambda b: (b, 0, 0)),
                   pl.BlockSpec((1, Tc, 256), lambda b: (b, 0, 0))],
        out_shape=[jax.ShapeDtypeStruct((B, T, 256), F32),
                   jax.ShapeDtypeStruct((B, Tc, 256), F32)],
        scratch_shapes=[pltpu.VMEM((n, CHUNK, LANE), F32),
                        pltpu.VMEM((n, CHUNK, LANE), F32),
                        pltpu.VMEM((n, CHUNK, 256), BF16),
                        pltpu.VMEM((2, n, CHUNK, LANE), F32),
                        pltpu.VMEM((2, n, CHUNK, LANE), F32),
                        pltpu.VMEM((n, CHUNK, 256), F32),
                        pltpu.VMEM((LANE, 256), F32)],
        compiler_params=_cparams(("arbitrary",)),
        name="gla",
    )(p_lat, p_lat, p_lat, p_lat, p_ctx, p_ctx, p_ctx, p_ctx, cos, sin, w2p, b2p)


def _gla_gate_weights(gla_w2, gla_b2):
    w = jnp.zeros((DEPTH, 2, LANE, LANE), F32)
    w = w.at[:, 0, 16:32].set(gla_w2[:, 0]).at[:, 1, 32:48].set(gla_w2[:, 1])
    return w.astype(BF16), gla_b2.astype(F32).reshape(DEPTH, 2, 1, LANE)


TILE_ROWS = D // LANE


def _store_token_tiles(ref, val):
    tm = val.shape[0]
    for s in range(TILE_ROWS):
        ref[0, pl.ds(s, tm, stride=TILE_ROWS), :] = val[:, s * LANE:(s + 1) * LANE]


def _load_token_tiles(ref, lead, t0, tm):
    return [ref[lead + (pl.ds(t0 * TILE_ROWS + s, tm, stride=TILE_ROWS), slice(None))] for s in range(TILE_ROWS)]


def _layer_norm(z, g, b):
    mu = z.mean(-1, keepdims=True)
    zc = z - mu
    var = jnp.mean(jnp.square(zc), -1, keepdims=True)
    return zc * lax.rsqrt(var + LN_EPS) * g + b


def _merge_kernel(hm_ref, hg_ref, hn_ref, mo_ref, gr_ref, x_ref, mod_ref, wout_ref, wr_ref, ng_ref, ln_ref,
                  x1_ref, lg_ref):
    r = lax.broadcasted_iota(jnp.int32, (256, 256), 0)
    c = lax.broadcasted_iota(jnp.int32, (256, 256), 1)
    avg = jnp.where((r // HEAD_DIM) == (c // HEAD_DIM), 1.0 / HEAD_DIM, 0.0).astype(BF16)

    def seg_mean(x):
        hi = x.astype(BF16)
        lo = (x - hi.astype(F32)).astype(BF16)
        return _dot(hi, avg) + _dot(lo, avg)

    def head_norm(h):
        d = h - seg_mean(h)
        return d * lax.rsqrt(seg_mean(d * d) + LN_EPS)

    ym = head_norm(hm_ref[0]) * ng_ref[0:1, :] * jax.nn.sigmoid(mo_ref[0])
    yg = head_norm(hg_ref[0]) * ng_ref[1:2, :] * jax.nn.silu(gr_ref[0])
    y = (_dot(ym.astype(BF16), wout_ref[0:256, :]) + _dot(yg.astype(BF16), wout_ref[256:512, :])
         + _dot(hn_ref[0].astype(BF16), wout_ref[512:1024, :]))
    m = mod_ref[0]
    x1 = _layer_norm(DEEPNORM_ALPHA * x_ref[0] + m[:, 2 * D:3 * D] * y, ln_ref[0:1, :], ln_ref[1:2, :])
    _store_token_tiles(x1_ref, x1)
    u2 = x1 * (1.0 + m[:, 4 * D:5 * D]) + m[:, 3 * D:4 * D]
    lg_ref[0] = _dot_nt(wr_ref[...], u2.astype(BF16))


def _merge_call(hm, hg, hn, p, x, mod3, wout_b, wrt_b, ng, ln, shared_row):
    B, T, _ = x.shape
    tm = min(T, 512)
    mod_map = (lambda b, i: (b, 0, 0)) if shared_row is None else (lambda b, i: (shared_row, 0, 0))
    tok = lambda w: pl.BlockSpec((1, tm, w), lambda b, i: (b, i, 0))
    return pl.pallas_call(
        _merge_kernel,
        grid=(B, T // tm),
        in_specs=[tok(256), tok(256), tok(512),
                  pl.BlockSpec((1, tm, 256), lambda b, i: (b, i, C_MO // 256)),
                  pl.BlockSpec((1, tm, 256), lambda b, i: (b, i, C_GR // 256)),
                  tok(D),
                  pl.BlockSpec((1, 1, 6 * D), mod_map),
                  pl.BlockSpec((D, D), lambda b, i: (0, 0)),
                  pl.BlockSpec((N_EXPERTS, D), lambda b, i: (0, 0)),
                  pl.BlockSpec((2, 256), lambda b, i: (0, 0)),
                  pl.BlockSpec((2, D), lambda b, i: (0, 0))],
        out_specs=[pl.BlockSpec((1, tm * TILE_ROWS, LANE), lambda b, i: (b, i, 0)),
                   pl.BlockSpec((1, N_EXPERTS, tm), lambda b, i: (b, 0, i))],
        out_shape=[jax.ShapeDtypeStruct((B, T * TILE_ROWS, LANE), F32),
                   jax.ShapeDtypeStruct((B, N_EXPERTS, T), F32)],
        compiler_params=_cparams(("arbitrary", "arbitrary")),
        name="merge",
    )(hm, hg, hn, p, p, x, mod3, wout_b, wrt_b, ng, ln)


ROUTER_SAMPLES = 8


def _router_samples(B):
    ns = ROUTER_SAMPLES
    while B % ns:
        ns -= 1
    return ns


def _router_kernel(lg_ref, o_ref, aff_s, sp_s, *, T, cap):
    J = T // LANE
    E = N_EXPERTS
    NS = lg_ref.shape[0]
    keys = [[] for _ in range(NS)]
    for j in range(J):
        for si in range(NS):
            lg = lg_ref[si, :, j * LANE:(j + 1) * LANE]
            ex = jnp.exp(lg - lg.max(0, keepdims=True))
            aff = ex / ex.sum(0, keepdims=True)
            aff_s[si, j * E:(j + 1) * E, :] = aff
            keys[si].append(pltpu.bitcast(aff, jnp.int32))

    def count(ks, pred):
        cnt = None
        for k in ks:
            cj = jnp.where(pred(k), 1.0, 0.0)
            cnt = cj if cnt is None else cnt + cj
        return cnt.sum(-1, keepdims=True)

    thrs = [jnp.zeros((E, 1), jnp.int32) for _ in range(NS)]
    for bit in range(30, -1, -1):
        for si in range(NS):
            cand = thrs[si] | (1 << bit)
            thrs[si] = jnp.where(count(keys[si], lambda k: k >= cand) >= cap, cand, thrs[si])

    r = lax.broadcasted_iota(jnp.int32, (LANE, LANE), 0)
    c = lax.broadcasted_iota(jnp.int32, (LANE, LANE), 1)
    upper = jnp.where(r <= c, 1.0, 0.0).astype(BF16)
    ones = jnp.ones((LANE, LANE), BF16)
    rr = lax.broadcasted_iota(jnp.int32, (J * E, J * E), 0)
    cc = lax.broadcasted_iota(jnp.int32, (J * E, J * E), 1)
    earlier = jnp.where(((rr % E) == (cc % E)) & ((cc // E) < (rr // E)), 1.0, 0.0).astype(BF16)

    def prefix(x01):
        xb = x01.astype(BF16)
        return _dot(xb, upper) + _dot(earlier, _dot(xb, ones).astype(BF16))

    for si in range(NS):
        thr = thrs[si]
        need = cap - count(keys[si], lambda k: k > thr)
        gt = jnp.concatenate([jnp.where(k > thr, 1.0, 0.0) for k in keys[si]], 0)
        eq = jnp.concatenate([jnp.where(k == thr, 1.0, 0.0) for k in keys[si]], 0)
        need_t = jnp.concatenate([need] * J, 0)
        sel = jnp.maximum(gt, jnp.where(prefix(eq) <= need_t, eq, 0.0))
        sp_s[si] = jnp.where(sel > 0.0, prefix(sel) - 1.0, -1.0)

    sb = min(cap, LANE)
    lane = lax.broadcasted_iota(jnp.int32, (1, LANE), 1)
    o_ref[...] = jnp.zeros_like(o_ref)
    for e in range(E):
        for half in range(cap // sb):
            slot = (lax.broadcasted_iota(jnp.int32, (sb, LANE), 0) + half * sb).astype(F32)
            rows = slice(half * sb, (half + 1) * sb)
            for si in range(NS):
                def jbody(j, acc):
                    acc_i, acc_g = acc
                    sp = sp_s[si, pl.ds(j * E + e, 1), :]
                    af = aff_s[si, pl.ds(j * E + e, 1), :]
                    hit = sp == slot
                    tid = (lane + j * LANE).astype(F32)
                    return jnp.where(hit, tid, acc_i), jnp.where(hit, af, acc_g)

                acc_i, acc_g = lax.fori_loop(0, J, jbody, (jnp.zeros((sb, LANE), F32), jnp.zeros((sb, LANE), F32)),
                                             unroll=2)
                icol = acc_i.sum(-1, keepdims=True)
                gcol = acc_g.sum(-1, keepdims=True)
                o_ref[si, rows, :] = jnp.where(lane == e, icol, jnp.where(lane == E + e, gcol, o_ref[si, rows, :]))


def _router_call(logits):
    B, _, T = logits.shape
    cap = CAPACITY_FACTOR * T // N_EXPERTS
    J = T // LANE
    ns = _router_samples(B)
    out = pl.pallas_call(
        functools.partial(_router_kernel, T=T, cap=cap),
        grid=(B // ns,),
        in_specs=[pl.BlockSpec((ns, N_EXPERTS, T), lambda b: (b, 0, 0))],
        out_specs=pl.BlockSpec((ns, cap, LANE), lambda b: (b, 0, 0)),
        out_shape=jax.ShapeDtypeStruct((B, cap, LANE), F32),
        scratch_shapes=[pltpu.VMEM((ns, J * N_EXPERTS, LANE), F32), pltpu.VMEM((ns, J * N_EXPERTS, LANE), F32)],
        compiler_params=_cparams(("arbitrary",)),
        name="router",
    )(logits)
    idx = out[:, :, :N_EXPERTS].astype(jnp.int32).transpose(0, 2, 1).reshape(B, 1, N_EXPERTS * cap)
    gate = out[:, :, N_EXPERTS:2 * N_EXPERTS].transpose(0, 2, 1).reshape(B, 1, N_EXPERTS * cap)
    return idx, gate


GATHER_BLOCK_BYTES = 4 * 1024 * 1024
SCATTER_BLOCK_BYTES = 2 * 1024 * 1024


def _experts_per_step(cap, block_bytes):
    eg = 2
    while eg < N_EXPERTS and 2 * eg * cap * D * 4 <= block_bytes:
        eg *= 2
    return eg


def _as_token_tile(v):
    return jnp.concatenate([v[:, s * LANE:(s + 1) * LANE] for s in range(TILE_ROWS)], 0)


def _gather_kernel(idx_ref, x_ref, mod_ref, o_ref, *, cap, eg):
    g = pl.program_id(1)
    m = mod_ref[0]
    scale = _as_token_tile(1.0 + m[:, 4 * D:5 * D])
    shift = _as_token_tile(m[:, 3 * D:4 * D])
    for e in range(eg):
        def body(s, carry):
            r = idx_ref[0, 0, (g * eg + e) * cap + s]
            src = pl.ds(pl.multiple_of(r * TILE_ROWS, TILE_ROWS), TILE_ROWS)
            dst = pl.ds(pl.multiple_of(s * TILE_ROWS, TILE_ROWS), TILE_ROWS)
            o_ref[e, 0, dst, :] = x_ref[0, src, :] * scale + shift
            return carry
        lax.fori_loop(0, cap, body, 0, unroll=8)


def _gather_call(idx, x1t, mod3, shared_row):
    B = x1t.shape[0]
    T = x1t.shape[1] // TILE_ROWS
    cap = CAPACITY_FACTOR * T // N_EXPERTS
    eg = _experts_per_step(cap, GATHER_BLOCK_BYTES)
    mod_map = (lambda b, g: (b, 0, 0)) if shared_row is None else (lambda b, g: (shared_row, 0, 0))
    smem = lambda: pl.BlockSpec((1, 1, N_EXPERTS * cap), lambda b, g: (b, 0, 0), memory_space=pltpu.SMEM)
    return pl.pallas_call(
        functools.partial(_gather_kernel, cap=cap, eg=eg),
        grid=(B, N_EXPERTS // eg),
        in_specs=[smem(), pl.BlockSpec((1, T * TILE_ROWS, LANE), lambda b, g: (b, 0, 0)),
                  pl.BlockSpec((1, 1, 6 * D), mod_map)],
        out_specs=pl.BlockSpec((eg, 1, cap * TILE_ROWS, LANE), lambda b, g: (g, b, 0, 0)),
        out_shape=jax.ShapeDtypeStruct((N_EXPERTS, B, cap * TILE_ROWS, LANE), F32),
        compiler_params=_cparams(("arbitrary", "arbitrary")),
        name="gather",
    )(idx, x1t, mod3)


FF_CHUNK = 512


def _ffn_kernel(x_ref, wg_ref, wu_ref, wd_ref, o_ref, xs, *, tm):
    for s, piece in enumerate(_load_token_tiles(x_ref, (0,), 0, tm)):
        xs[:, s * LANE:(s + 1) * LANE] = piece.astype(BF16)
    xb = xs[...]
    acc = None
    for c in range(EXPERT_FF // FF_CHUNK):
        cols = slice(c * FF_CHUNK, (c + 1) * FF_CHUNK)
        h = (jax.nn.silu(_dot(xb, wg_ref[0, :, cols])) * _dot(xb, wu_ref[0, :, cols])).astype(BF16)
        t = _dot(h, wd_ref[0, cols, :])
        acc = t if acc is None else acc + t
    _store_token_tiles(o_ref, acc)


def _ffn_call(xe, wg_b, wu_b, wd_b, layer):
    E = xe.shape[0]
    M = xe.shape[1] // TILE_ROWS
    tm = min(M, 512)
    w_map = lambda e, i: (layer * E + e, 0, 0)
    return pl.pallas_call(
        functools.partial(_ffn_kernel, tm=tm),
        grid=(E, M // tm),
        in_specs=[pl.BlockSpec((1, tm * TILE_ROWS, LANE), lambda e, i: (e, i, 0)),
                  pl.BlockSpec((1, D, EXPERT_FF), w_map),
                  pl.BlockSpec((1, D, EXPERT_FF), w_map),
                  pl.BlockSpec((1, EXPERT_FF, D), w_map)],
        out_specs=pl.BlockSpec((1, tm * TILE_ROWS, LANE), lambda e, i: (e, i, 0)),
        out_shape=jax.ShapeDtypeStruct((E, M * TILE_ROWS, LANE), F32),
        scratch_shapes=[pltpu.VMEM((tm, D), BF16)],
        compiler_params=_cparams(("arbitrary", "arbitrary")),
        name="ffn",
    )(xe, wg_b, wu_b, wd_b)


SCATTER_FIN_ROWS = 256


def _scatter_kernel(idx_ref, gate_ref, y_ref, x1_ref, mod_ref, ln_ref, o_ref, acc_a, acc_b, *, cap, eg, n_g, T):
    g = pl.program_id(1)

    @pl.when(g == 0)
    def _():
        acc_a[...] = jnp.zeros_like(acc_a)
        acc_b[...] = jnp.zeros_like(acc_b)

    for pair in range(eg // 2):
        def body(s, carry):
            p0 = (g * eg + 2 * pair) * cap + s
            p1 = p0 + cap
            src = pl.ds(pl.multiple_of(s * TILE_ROWS, TILE_ROWS), TILE_ROWS)
            d0 = pl.ds(pl.multiple_of(idx_ref[0, 0, p0] * TILE_ROWS, TILE_ROWS), TILE_ROWS)
            d1 = pl.ds(pl.multiple_of(idx_ref[0, 0, p1] * TILE_ROWS, TILE_ROWS), TILE_ROWS)
            acc_a[d0, :] = acc_a[d0, :] + y_ref[2 * pair, 0, src, :] * gate_ref[0, 0, p0]
            acc_b[d1, :] = acc_b[d1, :] + y_ref[2 * pair + 1, 0, src, :] * gate_ref[0, 0, p1]
            return carry
        lax.fori_loop(0, cap, body, 0, unroll=8)

    @pl.when(g == n_g - 1)
    def _():
        g2 = mod_ref[0][:, 5 * D:6 * D]
        tb = min(T, SCATTER_FIN_ROWS)
        for i in range(T // tb):
            rows = slice(i * tb, (i + 1) * tb)
            f = jnp.concatenate([pa + pb for pa, pb in zip(_load_token_tiles(acc_a, (), i * tb, tb),
                                                             _load_token_tiles(acc_b, (), i * tb, tb))], -1)
            x1 = jnp.concatenate(_load_token_tiles(x1_ref, (0,), i * tb, tb), -1)
            z = DEEPNORM_ALPHA * x1 + g2 * f
            o_ref[0, rows, :] = _layer_norm(z, ln_ref[0:1, :], ln_ref[1:2, :])


def _scatter_call(idx, gate, y4, x1t, mod3, ln, shared_row):
    B = x1t.shape[0]
    T = x1t.shape[1] // TILE_ROWS
    cap = CAPACITY_FACTOR * T // N_EXPERTS
    eg = _experts_per_step(cap, SCATTER_BLOCK_BYTES)
    n_g = N_EXPERTS // eg
    mod_map = (lambda b, g: (b, 0, 0)) if shared_row is None else (lambda b, g: (shared_row, 0, 0))
    smem = lambda: pl.BlockSpec((1, 1, N_EXPERTS * cap), lambda b, g: (b, 0, 0), memory_space=pltpu.SMEM)
    return pl.pallas_call(
        functools.partial(_scatter_kernel, cap=cap, eg=eg, n_g=n_g, T=T),
        grid=(B, n_g),
        scratch_shapes=[pltpu.VMEM((T * TILE_ROWS, LANE), F32), pltpu.VMEM((T * TILE_ROWS, LANE), F32)],
        in_specs=[smem(), smem(),
                  pl.BlockSpec((eg, 1, cap * TILE_ROWS, LANE), lambda b, g: (g, b, 0, 0)),
                  pl.BlockSpec((1, T * TILE_ROWS, LANE), lambda b, g: (b, 0, 0), pipeline_mode=pl.Buffered(1)),
                  pl.BlockSpec((1, 1, 6 * D), mod_map),
                  pl.BlockSpec((2, D), lambda b, g: (0, 0))],
        out_specs=pl.BlockSpec((1, T, D), lambda b, g: (b, 0, 0)),
        out_shape=jax.ShapeDtypeStruct((B, T, D), F32),
        compiler_params=_cparams(("arbitrary", "arbitrary")),
        name="scatter",
    )(idx, gate, y4, x1t, mod3, ln)


def _moe(x1t, logits, mod3, wg_b, wu_b, wd_b, layer, ln2, shared_row):
    B = x1t.shape[0]
    T = x1t.shape[1] // TILE_ROWS
    cap = CAPACITY_FACTOR * T // N_EXPERTS
    idx, gate = _router_call(logits)
    xe = _gather_call(idx, x1t, mod3, shared_row)
    y = _ffn_call(xe.reshape(N_EXPERTS, B * cap * TILE_ROWS, LANE), wg_b, wu_b, wd_b, layer)
    return _scatter_call(idx, gate, y.reshape(N_EXPERTS, B, cap * TILE_ROWS, LANE), x1t, mod3, ln2, shared_row)


def kernel(x, c, ctx, c_ctx, w_mod, b_mod, w_in, b_in, conv_w, gla_w2, gla_b2, mlstm_norm_g, gla_norm_g, rpb, w_out, ln1_g, ln1_b, w_router, w_gate, w_up, w_down, ln2_g, ln2_b):
    B, T, _ = x.shape
    n_mod = -(-(B + 1) // 8) * 8
    c_all = jnp.concatenate([c, c_ctx[None], jnp.zeros((n_mod - B - 1, D), F32)], 0)
    mods = _mod_call(c_all, w_mod, b_mod)
    w_p = _repack_columns(w_in).astype(BF16)
    b_p = _repack_columns(b_in)
    bias = _natten_bias_tables(rpb)
    mcos, msin = _rope_tables(T, 4, 16)
    gcos, gsin = _rope_tables(T, 4, 8)
    w2p, b2p = _gla_gate_weights(gla_w2, gla_b2)
    wout_b = w_out.astype(BF16)
    wrt_b = jnp.swapaxes(w_router, 1, 2).astype(BF16)
    wg_b = w_gate.astype(BF16).reshape(DEPTH * N_EXPERTS, D, EXPERT_FF)
    wu_b = w_up.astype(BF16).reshape(DEPTH * N_EXPERTS, D, EXPERT_FF)
    wd_b = w_down.astype(BF16).reshape(DEPTH * N_EXPERTS, EXPERT_FF, D)
    ng = jnp.stack([mlstm_norm_g, gla_norm_g], 1)
    ln1 = jnp.stack([ln1_g, ln1_b], 1)
    ln2 = jnp.stack([ln2_g, ln2_b], 1)
    for l in range(DEPTH):
        mod3 = mods[l].reshape(n_mod, 1, 6 * D)
        p_lat, pn_lat = _inproj_call(x, mod3, w_p[l], b_p[l][None], None)
        p_ctx, pn_ctx = _inproj_call(ctx, mod3, w_p[l], b_p[l][None], B)
        n_lat, n_ctx = _natten_call(pn_lat, pn_ctx, bias[l])
        m_lat, m_ctx = _mlstm_call(p_lat, p_ctx, mcos, msin, conv_w[l])
        g_lat, g_ctx = _gla_call(p_lat, p_ctx, gcos, gsin, w2p[l], b2p[l])
        x1t, lg = _merge_call(m_lat, g_lat, n_lat, p_lat, x, mod3, wout_b[l], wrt_b[l], ng[l], ln1[l], None)
        x = _moe(x1t, lg, mod3, wg_b, wu_b, wd_b, l, ln2[l], None)
        if l < DEPTH - 1:
            c1t, lgc = _merge_call(m_ctx, g_ctx, n_ctx, p_ctx, ctx, mod3, wout_b[l], wrt_b[l], ng[l], ln1[l], B)
            ctx = _moe(c1t, lgc, mod3, wg_b, wu_b, wd_b, l, ln2[l], B)
    return x
```

```python
import functools

import numpy as np
import jax
import jax.numpy as jnp
from jax import lax
from jax.experimental import pallas as pl
from jax.experimental.pallas import tpu as pltpu

F32 = jnp.float32
BF16 = jnp.bfloat16

D = 1024
DEPTH = 4
GRID_W = 64
HEAD_DIM = 64
WIN_ROWS = 8
WIN_COLS = 16
CONV_K = 3
ROPE_BASE = 10000.0
N_EXPERTS = 16
EXPERT_FF = 2 * D
CAPACITY_FACTOR = 2
LN_EPS = 1e-5
GLA_TAU = 16.0
DEEPNORM_ALPHA = (2 * DEPTH) ** 0.25
NEG = -1e30

VMEM_LIMIT = 56 * 1024 * 1024
LANE = 128
CHUNK = 256

C_MQK, C_MV, C_MO, C_GV, C_GR = 0, 512, 768, 1024, 1280
C_GQ, C_GK, C_SM = 1536, 1664, 1792
PM = 1920
C_NQ, C_NK, C_NV = 0, 512, 1024
PN = 1536
PW = PM + PN
_MAIN_SEGMENTS = ((0, 512), (512, 768), (768, 1024), (1296, 1552), (1552, 1808), (1040, 1168), (1168, 1296),
                  (1024, 1040), (1808, 1840))
_NAT_SEGMENTS = ((1840, 2352), (2352, 2864), (2864, 3376))
_NPAD = PM - sum(b - a for a, b in _MAIN_SEGMENTS)


def _repack_columns(w):
    parts = ([w[..., a:b] for a, b in _MAIN_SEGMENTS] + [jnp.zeros(w.shape[:-1] + (_NPAD,), w.dtype)]
             + [w[..., a:b] for a, b in _NAT_SEGMENTS])
    return jnp.concatenate(parts, -1)


def _cparams(sem):
    return pltpu.CompilerParams(dimension_semantics=sem, vmem_limit_bytes=VMEM_LIMIT)


def _dot(a, b):
    return jnp.dot(a, b, preferred_element_type=F32)


def _dot_nt(a, b):
    return lax.dot_general(a, b, (((1,), (1,)), ((), ())), preferred_element_type=F32)


def _split3(x):
    hi = x.astype(BF16)
    r1 = x - hi.astype(F32)
    mid = r1.astype(BF16)
    lo = (r1 - mid.astype(F32)).astype(BF16)
    return hi, mid, lo


def _dot_exact_l(m01, x):
    hi, mid, lo = _split3(x)
    return _dot(m01, hi) + _dot(m01, mid) + _dot(m01, lo)


def _dot_exact_r(x, m01):
    hi, mid, lo = _split3(x)
    return _dot(hi, m01) + _dot(mid, m01) + _dot(lo, m01)


def _mod_kernel(c_ref, w_ref, b_ref, o_ref):
    s = jax.nn.silu(c_ref[...]).astype(BF16)
    o_ref[0] = _dot(s, w_ref[0].astype(BF16)) + b_ref[0]


def _mod_call(c_all, w_mod, b_mod):
    rows = c_all.shape[0]
    tn = 1536
    return pl.pallas_call(
        _mod_kernel,
        grid=(DEPTH, 6 * D // tn),
        in_specs=[pl.BlockSpec((rows, D), lambda l, j: (0, 0)),
                  pl.BlockSpec((1, D, tn), lambda l, j: (l, 0, j)),
                  pl.BlockSpec((1, 1, tn), lambda l, j: (l, 0, j))],
        out_specs=pl.BlockSpec((1, rows, tn), lambda l, j: (l, 0, j)),
        out_shape=jax.ShapeDtypeStruct((DEPTH, rows, 6 * D), F32),
        compiler_params=_cparams(("arbitrary", "arbitrary")),
        name="mod",
    )(c_all, w_mod, b_mod.reshape(DEPTH, 1, 6 * D))


def _inproj_kernel(x_ref, mod_ref, w_ref, b_ref, o_ref, on_ref):
    m = mod_ref[0]
    u = (x_ref[0] * (1.0 + m[:, D:2 * D]) + m[:, 0:D]).astype(BF16)
    o_ref[0] = _dot(u, w_ref[:, :PM]) + b_ref[:, :PM]
    on_ref[0] = (_dot(u, w_ref[:, PM:]) + b_ref[:, PM:]).astype(BF16)


def _inproj_call(x, mod3, w_p, b_p, shared_row):
    B, T, _ = x.shape
    tm = min(T, 512)
    if shared_row is None:
        mod_map = lambda b, i: (b, 0, 0)
    else:
        mod_map = lambda b, i: (shared_row, 0, 0)
    return pl.pallas_call(
        _inproj_kernel,
        grid=(B, T // tm),
        in_specs=[pl.BlockSpec((1, tm, D), lambda b, i: (b, i, 0)),
                  pl.BlockSpec((1, 1, 6 * D), mod_map),
                  pl.BlockSpec((D, PW), lambda b, i: (0, 0)),
                  pl.BlockSpec((1, PW), lambda b, i: (0, 0))],
        out_specs=[pl.BlockSpec((1, tm, PM), lambda b, i: (b, i, 0)),
                   pl.BlockSpec((1, tm, PN), lambda b, i: (b, i, 0))],
        out_shape=[jax.ShapeDtypeStruct((B, T, PM), F32), jax.ShapeDtypeStruct((B, T, PN), BF16)],
        compiler_params=_cparams(("arbitrary", "arbitrary")),
        name="inproj",
    )(x, mod3, w_p, b_p)


def _softmax_av(s_list, v_list):
    m = s_list[0].max(-1, keepdims=True)
    for s in s_list[1:]:
        m = jnp.maximum(m, s.max(-1, keepdims=True))
    acc = None
    l = None
    for s, v in zip(s_list, v_list):
        p = jnp.exp(s - m)
        ls = p.sum(-1, keepdims=True)
        o = _dot(p.astype(BF16), v)
        acc = o if acc is None else acc + o
        l = ls if l is None else l + ls
    return acc / l


NAT_GROUP = 8


def _natten_kernel(q_ref, k_ref, v_ref, qc_ref, kc_ref, vc_ref, bias_ref, o_ref, oc_ref, *, n_rows):
    kcb = kc_ref[0]
    vcb = vc_ref[0]
    lane = lax.broadcasted_iota(jnp.int32, (1, LANE), 1)
    head0 = lane < HEAD_DIM
    scale = jnp.asarray(HEAD_DIM ** -0.5, BF16)
    zero = jnp.zeros((), BF16)

    def stack_heads(q):
        return jnp.concatenate([jnp.where(head0, q, zero), jnp.where(head0, zero, q)], 0)

    def unstack(o, n):
        return jnp.where(head0, o[:n], o[n:])

    def rows_body(g, carry):
        koffs, scores = [], []
        for i in range(NAT_GROUP):
            r = g * NAT_GROUP + i
            rs = jnp.clip(r - WIN_ROWS // 2, 0, n_rows - WIN_ROWS)
            qs = stack_heads(q_ref[0, pl.ds(pl.multiple_of(r * GRID_W, GRID_W), GRID_W), :] * scale)
            koff = pl.multiple_of(rs * GRID_W, GRID_W)
            kl = k_ref[0, pl.ds(koff, WIN_ROWS * GRID_W), :]
            koffs.append(koff)
            scores.append([_dot_nt(qs, kl) + bias_ref[r - rs, 0], _dot_nt(qs, kcb)])
        for i in range(NAT_GROUP):
            r = g * NAT_GROUP + i
            vl = v_ref[0, pl.ds(koffs[i], WIN_ROWS * GRID_W), :]
            o = _softmax_av(scores[i], [vl, vcb])
            o_ref[0, pl.ds(pl.multiple_of(r * GRID_W, GRID_W), GRID_W), :] = unstack(o, GRID_W)
        return carry

    lax.fori_loop(0, n_rows // NAT_GROUP, rows_body, 0)

    tc = qc_ref.shape[1]
    oc = _softmax_av([_dot_nt(stack_heads(qc_ref[0] * scale), kcb)], [vcb])
    oc_ref[0] = unstack(oc, tc)


def _natten_call(pn_lat, pn_ctx, bias_l):
    B, T, _ = pn_lat.shape
    Tc = pn_ctx.shape[1]
    n_rows = T // GRID_W
    assert n_rows >= WIN_ROWS
    cb = lambda base: (lambda b, p: (b, 0, base // LANE + p))
    return pl.pallas_call(
        functools.partial(_natten_kernel, n_rows=n_rows),
        grid=(B, 4),
        in_specs=[pl.BlockSpec((1, T, LANE), cb(C_NQ)),
                  pl.BlockSpec((1, T, LANE), cb(C_NK)),
                  pl.BlockSpec((1, T, LANE), cb(C_NV)),
                  pl.BlockSpec((1, Tc, LANE), cb(C_NQ)),
                  pl.BlockSpec((1, Tc, LANE), cb(C_NK)),
                  pl.BlockSpec((1, Tc, LANE), cb(C_NV)),
                  pl.BlockSpec((WIN_ROWS, 1, 2 * GRID_W, WIN_ROWS * GRID_W), lambda b, p: (0, p, 0, 0))],
        out_specs=[pl.BlockSpec((1, T, LANE), lambda b, p: (b, 0, p)),
                   pl.BlockSpec((1, Tc, LANE), lambda b, p: (b, 0, p))],
        out_shape=[jax.ShapeDtypeStruct((B, T, 4 * LANE), F32),
                   jax.ShapeDtypeStruct((B, Tc, 4 * LANE), F32)],
        compiler_params=_cparams(("arbitrary", "arbitrary")),
        name="natten",
    )(pn_lat, pn_lat, pn_lat, pn_ctx, pn_ctx, pn_ctx, bias_l)


def _natten_bias_tables(rpb):
    col = np.arange(GRID_W)
    cstart = np.clip(col - WIN_COLS // 2, 0, GRID_W - WIN_COLS)
    col_ok = (col[None, :] >= cstart[:, None]) & (col[None, :] < cstart[:, None] + WIN_COLS)
    dc_idx = np.clip(col[None, :] - col[:, None] + WIN_COLS - 1, 0, 2 * WIN_COLS - 2)
    toe = rpb.astype(F32)[..., dc_idx]
    toe = jnp.where(col_ok[None, None, None], toe, NEG)
    t = jnp.stack([toe[:, :, WIN_ROWS - 1 - d:2 * WIN_ROWS - 1 - d] for d in range(WIN_ROWS)], 1)
    t = t.transpose(0, 1, 2, 4, 3, 5)
    return t.reshape(DEPTH, WIN_ROWS, 4, 2 * GRID_W, WIN_ROWS * GRID_W)


def _tri_masks():
    r = lax.broadcasted_iota(jnp.int32, (CHUNK, CHUNK), 0)
    c = lax.broadcasted_iota(jnp.int32, (CHUNK, CHUNK), 1)
    return r >= c, r <= c


def _rope(x, cs, sn, first, dist):
    w = x.shape[-1]
    partner = jnp.where(first, pltpu.roll(x, w - dist, 1), pltpu.roll(x, dist, 1))
    return x * cs + partner * sn


def _rope_tables(T, n_heads, half):
    t = jnp.arange(T)
    rows = (t // GRID_W).astype(F32)
    cols = (t % GRID_W).astype(F32)
    inv = ROPE_BASE ** (-jnp.arange(half, dtype=F32) / half)
    ar = rows[:, None] * inv[None, :]
    ac = cols[:, None] * inv[None, :]
    cos = jnp.concatenate([jnp.cos(ar), jnp.cos(ar), jnp.cos(ac), jnp.cos(ac)], -1)
    sin = jnp.concatenate([-jnp.sin(ar), jnp.sin(ar), -jnp.sin(ac), jnp.sin(ac)], -1)
    return jnp.tile(cos, (1, n_heads)), jnp.tile(sin, (1, n_heads))


def _bwd_chunk(c, nc, n):
    return jnp.where(c < nc, nc - 1 - c, n - 1 - c + nc)


def _mlstm_kernel(qkl_ref, vl_ref, sml_ref, qkc_ref, vc_ref, smc_ref, cos_ref, sin_ref, cw_ref,
                  ol_ref, oc_ref,
                  qt_s, k_s, vat_s, gcol_s, grow_s, brow_s, cm_s, ot_s, ck_s, *, nc, n):
    low, upp = _tri_masks()
    tri = jnp.where(low, 1.0, 0.0).astype(BF16)
    triu = jnp.where(upp, 1.0, 0.0).astype(BF16)
    lane128 = lax.broadcasted_iota(jnp.int32, (1, LANE), 1)
    lane256 = lax.broadcasted_iota(jnp.int32, (1, 2 * LANE), 1)
    rowi = lax.broadcasted_iota(jnp.int32, (CHUNK, 1), 0)
    row16 = lax.broadcasted_iota(jnp.int32, (16, 1), 0)
    first = (lane256 % 32) < 16
    hmask = [(lane256 >= HEAD_DIM * h) & (lane256 < HEAD_DIM * (h + 1)) for h in range(4)]
    ones_blk = jnp.where(lax.broadcasted_iota(jnp.int32, (HEAD_DIM, CHUNK), 0) == 0, 1.0, 0.0)
    fwd_rows = row16 < 8
    cw = cw_ref[...]

    def cummax_lanes(x, suffix):
        sh = 1
        while sh < CHUNK:
            if suffix:
                moved = jnp.where(lane256 < CHUNK - sh, pltpu.roll(x, CHUNK - sh, 1), NEG)
            else:
                moved = jnp.where(lane256 >= sh, pltpu.roll(x, sh, 1), NEG)
            x = jnp.maximum(x, moved)
            sh *= 2
        return x

    def prep(qk_ref, v_ref, sm_ref, ci, n_str, dst, use_rope):
        r0 = ci * CHUNK
        xc = qk_ref[0, r0:r0 + CHUNK, :]
        prev = qk_ref[0, r0 - 1:r0, :] if ci > 0 else jnp.zeros((1, 4 * LANE), F32)
        nxt = qk_ref[0, r0 + CHUNK:r0 + CHUNK + 1, :] if ci < n_str - 1 else jnp.zeros((1, 4 * LANE), F32)
        xp = jnp.where(rowi == 0, prev, pltpu.roll(xc, 1, 0))
        xn = jnp.where(rowi == CHUNK - 1, nxt, pltpu.roll(xc, CHUNK - 1, 0))
        y = jax.nn.silu(xp * cw[0:1] + xc * cw[1:2] + xn * cw[2:3])
        q = y[:, :2 * LANE]
        k = y[:, 2 * LANE:] * HEAD_DIM ** -0.5
        if use_rope:
            cs = cos_ref[r0:r0 + CHUNK, :]
            sn = sin_ref[r0:r0 + CHUNK, :]
            q = _rope(q, cs, sn, first, 16)
            k = _rope(k, cs, sn, first, 16)
        qt_s[dst] = q.T.astype(BF16)
        k_s[dst] = k.astype(BF16)
        vt = v_ref[0, r0:r0 + CHUNK, :].T
        for h in range(4):
            vat_s[h, dst] = jnp.concatenate([vt[HEAD_DIM * h:HEAD_DIM * (h + 1)], ones_blk], 0).astype(BF16)
        g = sm_ref[0, r0:r0 + CHUNK, :]
        lf = pltpu.roll(jax.nn.log_sigmoid(g), LANE - 4, 1)
        gcol_s[dst] = g - jnp.where(lane128 < 8, _dot_exact_l(tri, lf), _dot_exact_l(triu, lf))
        gt = g.T[0:16]
        lft = pltpu.roll(jax.nn.log_sigmoid(gt), 12, 0)
        brow = jnp.where(fwd_rows, _dot_exact_r(lft, triu), _dot_exact_r(lft, tri))
        brow_s[dst] = brow
        grow_s[dst] = gt - brow

    for ci in range(nc):
        prep(qkc_ref, vc_ref, smc_ref, ci, nc, ci, False)
    for ci in range(n - nc):
        prep(qkl_ref, vl_ref, sml_ref, ci, n - nc, nc + ci, True)
    g_all = grow_s[...].reshape(n * 16, CHUNK)
    fwd_all = (lax.broadcasted_iota(jnp.int32, (n * 16, 1), 0) % 16) < 8
    cm_s[...] = jnp.where(fwd_all, cummax_lanes(g_all, False), cummax_lanes(g_all, True)).reshape(n, 16, CHUNK)

    def scan_dir(bwd):
        di = 1 if bwd else 0
        gi = 8 if bwd else 0
        causal = low if bwd else upp

        def matmuls_first(c):
            ch = _bwd_chunk(c, nc, n) if bwd else c
            qt = qt_s[ch]
            kb = k_s[ch]
            ck_b = ck_s[di].astype(BF16)
            scores, inters = [], []
            for h in range(4):
                pr = slice(LANE * (h // 2), LANE * (h // 2 + 1))
                own = (lane128 < HEAD_DIM) if h % 2 == 0 else (lane128 >= HEAD_DIM)
                zero = jnp.zeros((), BF16)
                scores.append(_dot(jnp.where(own, kb[:, pr], zero), qt[pr, :]))
                inters.append(_dot(jnp.where(own, ck_b[:, pr], zero), qt[pr, :]))
            return ch, kb, scores, inters

        def rest(first, ms):
            ch, kb, scores, inters = first
            gcol_all = gcol_s[ch]
            new_ms = []
            for h in range(4):
                m = ms[h]
                g_row = grow_s[ch, gi + h:gi + h + 1, :]
                b_row = brow_s[ch, gi + h:gi + h + 1, :]
                a_row = jnp.maximum(m, cm_s[ch, gi + h:gi + h + 1, :])
                w = jnp.exp(jnp.where(causal, gcol_all[:, gi + h:gi + h + 1] - a_row, NEG))
                pt = (scores[h] * w).astype(BF16)
                vat = vat_s[h, ch]
                nd = _dot(vat, pt) + jnp.exp(m - a_row) * inters[h]
                den = nd[HEAD_DIM:HEAD_DIM + 1, :]
                ht = nd[0:HEAD_DIM] / jnp.maximum(jnp.abs(den), jnp.exp(-(b_row + a_row)))
                ot_s[di, ch, HEAD_DIM * h:HEAD_DIM * (h + 1), :] = ht
                bl = b_row[:, 0:1] if bwd else b_row[:, CHUNK - 1:CHUNK]
                lw_end = bl + g_row
                m_new = jnp.maximum(bl + m, lw_end.max(-1, keepdims=True))
                upd = _dot((vat * jnp.exp(lw_end - m_new)).astype(BF16), kb)
                ck_s[di] = jnp.where(hmask[h], jnp.exp(bl + m - m_new) * ck_s[di] + upd, ck_s[di])
                new_ms.append(m_new)
            return tuple(new_ms)

        return matmuls_first, rest

    ck_s[...] = jnp.zeros_like(ck_s)
    first_f, rest_f = scan_dir(False)
    first_b, rest_b = scan_dir(True)

    def chunk_body(c, carry):
        ff, fb = first_f(c), first_b(c)
        return rest_f(ff, carry[0]), rest_b(fb, carry[1])

    zeros4 = tuple(jnp.zeros((1, 1), F32) for _ in range(4))
    lax.fori_loop(0, n, chunk_body, (zeros4, zeros4))
    for ci in range(nc):
        oc_ref[0, ci * CHUNK:(ci + 1) * CHUNK, :] = (ot_s[0, ci] + ot_s[1, ci]).T
    for ci in range(n - nc):
        ol_ref[0, ci * CHUNK:(ci + 1) * CHUNK, :] = (ot_s[0, nc + ci] + ot_s[1, nc + ci]).T


def _mlstm_call(p_lat, p_ctx, cos, sin, conv_w_l):
    B, T, _ = p_lat.shape
    Tc = p_ctx.shape[1]
    nc, n = Tc // CHUNK, (Tc + T) // CHUNK
    cb = lambda base, w: (lambda b: (b, 0, base // w))
    return pl.pallas_call(
        functools.partial(_mlstm_kernel, nc=nc, n=n),
        grid=(B,),
        in_specs=[pl.BlockSpec((1, T, 512), cb(C_MQK, 512)),
                  pl.BlockSpec((1, T, 256), cb(C_MV, 256)),
                  pl.BlockSpec((1, T, LANE), cb(C_SM, LANE)),
                  pl.BlockSpec((1, Tc, 512), cb(C_MQK, 512)),
                  pl.BlockSpec((1, Tc, 256), cb(C_MV, 256)),
                  pl.BlockSpec((1, Tc, LANE), cb(C_SM, LANE)),
                  pl.BlockSpec((T, 256), lambda b: (0, 0)),
                  pl.BlockSpec((T, 256), lambda b: (0, 0)),
                  pl.BlockSpec((CONV_K, 512), lambda b: (0, 0))],
        out_specs=[pl.BlockSpec((1, T, 256), lambda b: (b, 0, 0)),
                   pl.BlockSpec((1, Tc, 256), lambda b: (b, 0, 0))],
        out_shape=[jax.ShapeDtypeStruct((B, T, 256), F32),
                   jax.ShapeDtypeStruct((B, Tc, 256), F32)],
        scratch_shapes=[pltpu.VMEM((n, 256, CHUNK), BF16),
                        pltpu.VMEM((n, CHUNK, 256), BF16),
                        pltpu.VMEM((4, n, LANE, CHUNK), BF16),
                        pltpu.VMEM((n, CHUNK, LANE), F32),
                        pltpu.VMEM((n, 16, CHUNK), F32),
                        pltpu.VMEM((n, 16, CHUNK), F32),
                        pltpu.VMEM((n, 16, CHUNK), F32),
                        pltpu.VMEM((2, n, 256, CHUNK), F32),
                        pltpu.VMEM((2, LANE, 256), F32)],
        compiler_params=_cparams(("arbitrary",)),
        name="mlstm",
    )(p_lat, p_lat, p_lat, p_ctx, p_ctx, p_ctx, cos, sin, conv_w_l)


GLA_BLK = 64
GLA_CLAMP = 80.0


def _gla_kernel(ql_ref, kl_ref, vl_ref, sml_ref, qc_ref, kc_ref, vc_ref, smc_ref, cos_ref, sin_ref, w2_ref, b2_ref,
                ol_ref, oc_ref,
                q_s, k_s, v_s, a_s, b_s, o_s, st_s, *, nc, n):
    low, upp = _tri_masks()
    tri = jnp.where(low, 1.0, 0.0).astype(BF16)
    triu = jnp.where(upp, 1.0, 0.0).astype(BF16)
    lane128 = lax.broadcasted_iota(jnp.int32, (1, LANE), 1)
    lane256 = lax.broadcasted_iota(jnp.int32, (1, 2 * LANE), 1)
    first = (lane128 % 16) < 8
    hm128 = [(lane128 >= 32 * h) & (lane128 < 32 * (h + 1)) for h in range(4)]
    hm256 = [(lane256 >= 64 * h) & (lane256 < 64 * (h + 1)) for h in range(4)]
    nb = CHUNK // GLA_BLK
    r_st = lax.broadcasted_iota(jnp.int32, (nb * GLA_BLK, CHUNK), 0)
    c_st = lax.broadcasted_iota(jnp.int32, (nb * GLA_BLK, CHUNK), 1)
    bd_r = lax.broadcasted_iota(jnp.int32, (LANE, 2 * LANE), 0)
    bd_c = lax.broadcasted_iota(jnp.int32, (LANE, 2 * LANE), 1)
    blockdiag = (bd_r // 32) == (bd_c // 64)

    def prep(q_ref, k_ref, v_ref, sm_ref, ci, dst, use_rope):
        r0 = ci * CHUNK
        q = q_ref[0, r0:r0 + CHUNK, :] * 32 ** -0.5
        k = k_ref[0, r0:r0 + CHUNK, :]
        if use_rope:
            cs = cos_ref[r0:r0 + CHUNK, :]
            sn = sin_ref[r0:r0 + CHUNK, :]
            q = _rope(q, cs, sn, first, 8)
            k = _rope(k, cs, sn, first, 8)
        q_s[dst] = q
        k_s[dst] = k
        v_s[dst] = v_ref[0, r0:r0 + CHUNK, :].astype(BF16)
        lr = sm_ref[0, r0:r0 + CHUNK, :].astype(BF16)
        for d in range(2):
            a = jax.nn.log_sigmoid(_dot(lr, w2_ref[d]) + b2_ref[d]) / GLA_TAU
            a_s[d, dst] = a
            b_s[d, dst] = _dot_exact_l(triu if d else tri, a)

    for ci in range(nc):
        prep(qc_ref, kc_ref, vc_ref, smc_ref, ci, ci, False)
    for ci in range(n - nc):
        prep(ql_ref, kl_ref, vl_ref, sml_ref, ci, nc + ci, True)

    def scan_dir(bwd):
        d = 1 if bwd else 0
        st_s[...] = jnp.zeros_like(st_s)

        def chunk_body(c, carry):
            ch = _bwd_chunk(c, nc, n) if bwd else c
            q = q_s[ch]
            k = k_s[ch]
            vb = v_s[ch]
            a = a_s[d, ch]
            b = b_s[d, ch]
            st_b = st_s[...].astype(BF16)
            o_inter = _dot((q * jnp.exp(b)).astype(BF16), st_b)
            atts = []
            for i in range(nb):
                rows = slice(i * GLA_BLK, (i + 1) * GLA_BLK)
                e = (i + 1) * GLA_BLK - 1 if bwd else i * GLA_BLK
                ref = b[e:e + 1, :] - a[e:e + 1, :]
                qs = q[rows] * jnp.exp(b[rows] - ref)
                ks = (k * jnp.exp(jnp.minimum(ref - b, GLA_CLAMP))).astype(BF16)
                lhs = jnp.concatenate([jnp.where(hm128[h], qs, 0.0) for h in range(4)], 0).astype(BF16)
                att = _dot_nt(lhs, ks)
                t_idx = (r_st % GLA_BLK) + i * GLA_BLK
                ok = (c_st >= t_idx) if bwd else (c_st <= t_idx)
                atts.append(jnp.where(ok, att, 0.0).astype(BF16))
            for i in range(nb):
                rows = slice(i * GLA_BLK, (i + 1) * GLA_BLK)
                oh = _dot(atts[i], vb)
                o_blk = o_inter[rows]
                for h in range(4):
                    o_blk = o_blk + jnp.where(hm256[h], oh[h * GLA_BLK:(h + 1) * GLA_BLK], 0.0)
                if bwd:
                    o_s[ch, rows, :] = o_s[ch, rows, :] + o_blk
                else:
                    o_s[ch, rows, :] = o_blk
            bt = b.T
            tot = bt[:, 0:1] if bwd else bt[:, CHUNK - 1:CHUNK]
            kt = (k.T * jnp.exp(tot - bt)).astype(BF16)
            upd = jnp.where(blockdiag, _dot(kt, vb), 0.0)
            st_s[...] = jnp.exp(tot) * st_s[...] + upd
            return carry

        lax.fori_loop(0, n, chunk_body, 0)

    scan_dir(False)
    scan_dir(True)
    for ci in range(nc):
        oc_ref[0, ci * CHUNK:(ci + 1) * CHUNK, :] = o_s[ci]
    for ci in range(n - nc):
        ol_ref[0, ci * CHUNK:(ci + 1) * CHUNK, :] = o_s[nc + ci]


def _gla_call(p_lat, p_ctx, cos, sin, w2p, b2p):
    B, T, _ = p_lat.shape
    Tc = p_ctx.shape[1]
    nc, n = Tc // CHUNK, (Tc + T) // CHUNK
    cb = lambda base, w: (lambda b: (b, 0, base // w))
    return pl.pallas_call(
        functools.partial(_gla_kernel, nc=nc, n=n),
        grid=(B,),
        in_specs=[pl.BlockSpec((1, T, LANE), cb(C_GQ, LANE)),
                  pl.BlockSpec((1, T, LANE), cb(C_GK, LANE)),
                  pl.BlockSpec((1, T, 256), cb(C_GV, 256)),
                  pl.BlockSpec((1, T, LANE), cb(C_SM, LANE)),
                  pl.BlockSpec((1, Tc, LANE), cb(C_GQ, LANE)),
                  pl.BlockSpec((1, Tc, LANE), cb(C_GK, LANE)),
                  pl.BlockSpec((1, Tc, 256), cb(C_GV, 256)),
                  pl.BlockSpec((1, Tc, LANE), cb(C_SM, LANE)),
                  pl.BlockSpec((T, LANE), lambda b: (0, 0)),
                  pl.BlockSpec((T, LANE), lambda b: (0, 0)),
                  pl.BlockSpec((2, LANE, LANE), lambda b: (0, 0, 0)),
                  pl.BlockSpec((2, 1, LANE), lambda b: (0, 0, 0))],
        out_specs=[pl.BlockSpec((1, T, 256), lambda b: (b, 0, 0)),
                   pl.BlockSpec((1, Tc, 256), lambda b: (b, 0, 0))],
        out_shape=[jax.ShapeDtypeStruct((B, T, 256), F32),
                   jax.ShapeDtypeStruct((B, Tc, 256), F32)],
        scratch_shapes=[pltpu.VMEM((n, CHUNK, LANE), F32),
                        pltpu.VMEM((n, CHUNK, LANE), F32),
                        pltpu.VMEM((n, CHUNK, 256), BF16),
                        pltpu.VMEM((2, n, CHUNK, LANE), F32),
                        pltpu.VMEM((2, n, CHUNK, LANE), F32),
                        pltpu.VMEM((n, CHUNK, 256), F32),
                        pltpu.VMEM((LANE, 256), F32)],
        compiler_params=_cparams(("arbitrary",)),
        name="gla",
    )(p_lat, p_lat, p_lat, p_lat, p_ctx, p_ctx, p_ctx, p_ctx, cos, sin, w2p, b2p)


def _gla_gate_weights(gla_w2, gla_b2):
    w = jnp.zeros((DEPTH, 2, LANE, LANE), F32)
    w = w.at[:, 0, 16:32].set(gla_w2[:, 0]).at[:, 1, 32:48].set(gla_w2[:, 1])
    return w.astype(BF16), gla_b2.astype(F32).reshape(DEPTH, 2, 1, LANE)


TILE_ROWS = D // LANE


def _store_token_tiles(ref, val):
    tm = val.shape[0]
    for s in range(TILE_ROWS):
        ref[0, pl.ds(s, tm, stride=TILE_ROWS), :] = val[:, s * LANE:(s + 1) * LANE]


def _load_token_tiles(ref, lead, t0, tm):
    return [ref[lead + (pl.ds(t0 * TILE_ROWS + s, tm, stride=TILE_ROWS), slice(None))] for s in range(TILE_ROWS)]


def _layer_norm(z, g, b):
    mu = z.mean(-1, keepdims=True)
    zc = z - mu
    var = jnp.mean(jnp.square(zc), -1, keepdims=True)
    return zc * lax.rsqrt(var + LN_EPS) * g + b


def _merge_kernel(hm_ref, hg_ref, hn_ref, mo_ref, gr_ref, x_ref, mod_ref, wout_ref, wr_ref, ng_ref, ln_ref,
                  x1_ref, lg_ref):
    r = lax.broadcasted_iota(jnp.int32, (256, 256), 0)
    c = lax.broadcasted_iota(jnp.int32, (256, 256), 1)
    avg = jnp.where((r // HEAD_DIM) == (c // HEAD_DIM), 1.0 / HEAD_DIM, 0.0).astype(BF16)

    def seg_mean(x):
        hi = x.astype(BF16)
        lo = (x - hi.astype(F32)).astype(BF16)
        return _dot(hi, avg) + _dot(lo, avg)

    def head_norm(h):
        d = h - seg_mean(h)
        return d * lax.rsqrt(seg_mean(d * d) + LN_EPS)

    ym = head_norm(hm_ref[0]) * ng_ref[0:1, :] * jax.nn.sigmoid(mo_ref[0])
    yg = head_norm(hg_ref[0]) * ng_ref[1:2, :] * jax.nn.silu(gr_ref[0])
    y = (_dot(ym.astype(BF16), wout_ref[0:256, :]) + _dot(yg.astype(BF16), wout_ref[256:512, :])
         + _dot(hn_ref[0].astype(BF16), wout_ref[512:1024, :]))
    m = mod_ref[0]
    x1 = _layer_norm(DEEPNORM_ALPHA * x_ref[0] + m[:, 2 * D:3 * D] * y, ln_ref[0:1, :], ln_ref[1:2, :])
    _store_token_tiles(x1_ref, x1)
    u2 = x1 * (1.0 + m[:, 4 * D:5 * D]) + m[:, 3 * D:4 * D]
    lg_ref[0] = _dot_nt(wr_ref[...], u2.astype(BF16))


def _merge_call(hm, hg, hn, p, x, mod3, wout_b, wrt_b, ng, ln, shared_row):
    B, T, _ = x.shape
    tm = min(T, 512)
    mod_map = (lambda b, i: (b, 0, 0)) if shared_row is None else (lambda b, i: (shared_row, 0, 0))
    tok = lambda w: pl.BlockSpec((1, tm, w), lambda b, i: (b, i, 0))
    return pl.pallas_call(
        _merge_kernel,
        grid=(B, T // tm),
        in_specs=[tok(256), tok(256), tok(512),
                  pl.BlockSpec((1, tm, 256), lambda b, i: (b, i, C_MO // 256)),
                  pl.BlockSpec((1, tm, 256), lambda b, i: (b, i, C_GR // 256)),
                  tok(D),
                  pl.BlockSpec((1, 1, 6 * D), mod_map),
                  pl.BlockSpec((D, D), lambda b, i: (0, 0)),
                  pl.BlockSpec((N_EXPERTS, D), lambda b, i: (0, 0)),
                  pl.BlockSpec((2, 256), lambda b, i: (0, 0)),
                  pl.BlockSpec((2, D), lambda b, i: (0, 0))],
        out_specs=[pl.BlockSpec((1, tm * TILE_ROWS, LANE), lambda b, i: (b, i, 0)),
                   pl.BlockSpec((1, N_EXPERTS, tm), lambda b, i: (b, 0, i))],
        out_shape=[jax.ShapeDtypeStruct((B, T * TILE_ROWS, LANE), F32),
                   jax.ShapeDtypeStruct((B, N_EXPERTS, T), F32)],
        compiler_params=_cparams(("arbitrary", "arbitrary")),
        name="merge",
    )(hm, hg, hn, p, p, x, mod3, wout_b, wrt_b, ng, ln)


ROUTER_SAMPLES = 8


def _router_samples(B):
    ns = ROUTER_SAMPLES
    while B % ns:
        ns -= 1
    return ns


def _router_kernel(lg_ref, o_ref, aff_s, sp_s, *, T, cap):
    J = T // LANE
    E = N_EXPERTS
    NS = lg_ref.shape[0]
    keys = [[] for _ in range(NS)]
    for j in range(J):
        for si in range(NS):
            lg = lg_ref[si, :, j * LANE:(j + 1) * LANE]
            ex = jnp.exp(lg - lg.max(0, keepdims=True))
            aff = ex / ex.sum(0, keepdims=True)
            aff_s[si, j * E:(j + 1) * E, :] = aff
            keys[si].append(pltpu.bitcast(aff, jnp.int32))

    def count(ks, pred):
        cnt = None
        for k in ks:
            cj = jnp.where(pred(k), 1.0, 0.0)
            cnt = cj if cnt is None else cnt + cj
        return cnt.sum(-1, keepdims=True)

    thrs = [jnp.zeros((E, 1), jnp.int32) for _ in range(NS)]
    for bit in range(30, -1, -1):
        for si in range(NS):
            cand = thrs[si] | (1 << bit)
            thrs[si] = jnp.where(count(keys[si], lambda k: k >= cand) >= cap, cand, thrs[si])

    r = lax.broadcasted_iota(jnp.int32, (LANE, LANE), 0)
    c = lax.broadcasted_iota(jnp.int32, (LANE, LANE), 1)
    upper = jnp.where(r <= c, 1.0, 0.0).astype(BF16)
    ones = jnp.ones((LANE, LANE), BF16)
    rr = lax.broadcasted_iota(jnp.int32, (J * E, J * E), 0)
    cc = lax.broadcasted_iota(jnp.int32, (J * E, J * E), 1)
    earlier = jnp.where(((rr % E) == (cc % E)) & ((cc // E) < (rr // E)), 1.0, 0.0).astype(BF16)

    def prefix(x01):
        xb = x01.astype(BF16)
        return _dot(xb, upper) + _dot(earlier, _dot(xb, ones).astype(BF16))

    for si in range(NS):
        thr = thrs[si]
        need = cap - count(keys[si], lambda k: k > thr)
        gt = jnp.concatenate([jnp.where(k > thr, 1.0, 0.0) for k in keys[si]], 0)
        eq = jnp.concatenate([jnp.where(k == thr, 1.0, 0.0) for k in keys[si]], 0)
        need_t = jnp.concatenate([need] * J, 0)
        sel = jnp.maximum(gt, jnp.where(prefix(eq) <= need_t, eq, 0.0))
        sp_s[si] = jnp.where(sel > 0.0, prefix(sel) - 1.0, -1.0)

    sb = min(cap, LANE)
    lane = lax.broadcasted_iota(jnp.int32, (1, LANE), 1)
    o_ref[...] = jnp.zeros_like(o_ref)
    for e in range(E):
        for half in range(cap // sb):
            slot = (lax.broadcasted_iota(jnp.int32, (sb, LANE), 0) + half * sb).astype(F32)
            rows = slice(half * sb, (half + 1) * sb)
            for si in range(NS):
                def jbody(j, acc):
                    acc_i, acc_g = acc
                    sp = sp_s[si, pl.ds(j * E + e, 1), :]
                    af = aff_s[si, pl.ds(j * E + e, 1), :]
                    hit = sp == slot
                    tid = (lane + j * LANE).astype(F32)
                    return jnp.where(hit, tid, acc_i), jnp.where(hit, af, acc_g)

                acc_i, acc_g = lax.fori_loop(0, J, jbody, (jnp.zeros((sb, LANE), F32), jnp.zeros((sb, LANE), F32)),
                                             unroll=2)
                icol = acc_i.sum(-1, keepdims=True)
                gcol = acc_g.sum(-1, keepdims=True)
                o_ref[si, rows, :] = jnp.where(lane == e, icol, jnp.where(lane == E + e, gcol, o_ref[si, rows, :]))


def _router_call(logits):
    B, _, T = logits.shape
    cap = CAPACITY_FACTOR * T // N_EXPERTS
    J = T // LANE
    ns = _router_samples(B)
    out = pl.pallas_call(
        functools.partial(_router_kernel, T=T, cap=cap),
        grid=(B // ns,),
        in_specs=[pl.BlockSpec((ns, N_EXPERTS, T), lambda b: (b, 0, 0))],
        out_specs=pl.BlockSpec((ns, cap, LANE), lambda b: (b, 0, 0)),
        out_shape=jax.ShapeDtypeStruct((B, cap, LANE), F32),
        scratch_shapes=[pltpu.VMEM((ns, J * N_EXPERTS, LANE), F32), pltpu.VMEM((ns, J * N_EXPERTS, LANE), F32)],
        compiler_params=_cparams(("arbitrary",)),
        name="router",
    )(logits)
    idx = out[:, :, :N_EXPERTS].astype(jnp.int32).transpose(0, 2, 1).reshape(B, 1, N_EXPERTS * cap)
    gate = out[:, :, N_EXPERTS:2 * N_EXPERTS].transpose(0, 2, 1).reshape(B, 1, N_EXPERTS * cap)
    return idx, gate


GATHER_BLOCK_BYTES = 4 * 1024 * 1024
SCATTER_BLOCK_BYTES = 4 * 1024 * 1024


def _experts_per_step(cap, block_bytes):
    eg = 2
    while eg < N_EXPERTS and 2 * eg * cap * D * 4 <= block_bytes:
        eg *= 2
    return eg


def _as_token_tile(v):
    return jnp.concatenate([v[:, s * LANE:(s + 1) * LANE] for s in range(TILE_ROWS)], 0)


def _gather_kernel(idx_ref, x_ref, mod_ref, o_ref, *, cap, eg):
    g = pl.program_id(1)
    m = mod_ref[0]
    scale = _as_token_tile(1.0 + m[:, 4 * D:5 * D])
    shift = _as_token_tile(m[:, 3 * D:4 * D])
    for e in range(eg):
        def body(s, carry):
            r = idx_ref[0, 0, (g * eg + e) * cap + s]
            src = pl.ds(pl.multiple_of(r * TILE_ROWS, TILE_ROWS), TILE_ROWS)
            dst = pl.ds(pl.multiple_of(s * TILE_ROWS, TILE_ROWS), TILE_ROWS)
            o_ref[e, 0, dst, :] = x_ref[0, src, :] * scale + shift
            return carry
        lax.fori_loop(0, cap, body, 0, unroll=8)


def _gather_call(idx, x1t, mod3, shared_row):
    B = x1t.shape[0]
    T = x1t.shape[1] // TILE_ROWS
    cap = CAPACITY_FACTOR * T // N_EXPERTS
    eg = _experts_per_step(cap, GATHER_BLOCK_BYTES)
    mod_map = (lambda b, g: (b, 0, 0)) if shared_row is None else (lambda b, g: (shared_row, 0, 0))
    smem = lambda: pl.BlockSpec((1, 1, N_EXPERTS * cap), lambda b, g: (b, 0, 0), memory_space=pltpu.SMEM)
    return pl.pallas_call(
        functools.partial(_gather_kernel, cap=cap, eg=eg),
        grid=(B, N_EXPERTS // eg),
        in_specs=[smem(), pl.BlockSpec((1, T * TILE_ROWS, LANE), lambda b, g: (b, 0, 0)),
                  pl.BlockSpec((1, 1, 6 * D), mod_map)],
        out_specs=pl.BlockSpec((eg, 1, cap * TILE_ROWS, LANE), lambda b, g: (g, b, 0, 0)),
        out_shape=jax.ShapeDtypeStruct((N_EXPERTS, B, cap * TILE_ROWS, LANE), F32),
        compiler_params=_cparams(("arbitrary", "arbitrary")),
        name="gather",
    )(idx, x1t, mod3)


FF_CHUNK = 512


def _ffn_kernel(x_ref, wg_ref, wu_ref, wd_ref, o_ref, xs, *, tm):
    for s, piece in enumerate(_load_token_tiles(x_ref, (0,), 0, tm)):
        xs[:, s * LANE:(s + 1) * LANE] = piece.astype(BF16)
    xb = xs[...]
    acc = None
    for c in range(EXPERT_FF // FF_CHUNK):
        cols = slice(c * FF_CHUNK, (c + 1) * FF_CHUNK)
        h = (jax.nn.silu(_dot(xb, wg_ref[0, :, cols])) * _dot(xb, wu_ref[0, :, cols])).astype(BF16)
        t = _dot(h, wd_ref[0, cols, :])
        acc = t if acc is None else acc + t
    _store_token_tiles(o_ref, acc)


def _ffn_call(xe, wg_b, wu_b, wd_b, layer):
    E = xe.shape[0]
    M = xe.shape[1] // TILE_ROWS
    tm = min(M, 512)
    w_map = lambda e, i: (layer * E + e, 0, 0)
    return pl.pallas_call(
        functools.partial(_ffn_kernel, tm=tm),
        grid=(E, M // tm),
        in_specs=[pl.BlockSpec((1, tm * TILE_ROWS, LANE), lambda e, i: (e, i, 0)),
                  pl.BlockSpec((1, D, EXPERT_FF), w_map),
                  pl.BlockSpec((1, D, EXPERT_FF), w_map),
                  pl.BlockSpec((1, EXPERT_FF, D), w_map)],
        out_specs=pl.BlockSpec((1, tm * TILE_ROWS, LANE), lambda e, i: (e, i, 0)),
        out_shape=jax.ShapeDtypeStruct((E, M * TILE_ROWS, LANE), F32),
        scratch_shapes=[pltpu.VMEM((tm, D), BF16)],
        compiler_params=_cparams(("arbitrary", "arbitrary")),
        name="ffn",
    )(xe, wg_b, wu_b, wd_b)


SCATTER_FIN_ROWS = 256


def _scatter_kernel(idx_ref, gate_ref, y_ref, x1_ref, mod_ref, ln_ref, o_ref, acc_a, acc_b, *, cap, eg, n_g, T):
    g = pl.program_id(1)

    @pl.when(g == 0)
    def _():
        acc_a[...] = jnp.zeros_like(acc_a)
        acc_b[...] = jnp.zeros_like(acc_b)

    for pair in range(eg // 2):
        def body(s, carry):
            p0 = (g * eg + 2 * pair) * cap + s
            p1 = p0 + cap
            src = pl.ds(pl.multiple_of(s * TILE_ROWS, TILE_ROWS), TILE_ROWS)
            d0 = pl.ds(pl.multiple_of(idx_ref[0, 0, p0] * TILE_ROWS, TILE_ROWS), TILE_ROWS)
            d1 = pl.ds(pl.multiple_of(idx_ref[0, 0, p1] * TILE_ROWS, TILE_ROWS), TILE_ROWS)
            acc_a[d0, :] = acc_a[d0, :] + y_ref[2 * pair, 0, src, :] * gate_ref[0, 0, p0]
            acc_b[d1, :] = acc_b[d1, :] + y_ref[2 * pair + 1, 0, src, :] * gate_ref[0, 0, p1]
            return carry
        lax.fori_loop(0, cap, body, 0, unroll=8)

    @pl.when(g == n_g - 1)
    def _():
        g2 = mod_ref[0][:, 5 * D:6 * D]
        tb = min(T, SCATTER_FIN_ROWS)
        for i in range(T // tb):
            rows = slice(i * tb, (i + 1) * tb)
            f = jnp.concatenate([pa + pb for pa, pb in zip(_load_token_tiles(acc_a, (), i * tb, tb),
                                                             _load_token_tiles(acc_b, (), i * tb, tb))], -1)
            x1 = jnp.concatenate(_load_token_tiles(x1_ref, (0,), i * tb, tb), -1)
            z = DEEPNORM_ALPHA * x1 + g2 * f
            o_ref[0, rows, :] = _layer_norm(z, ln_ref[0:1, :], ln_ref[1:2, :])


def _scatter_call(idx, gate, y4, x1t, mod3, ln, shared_row):
    B = x1t.shape[0]
    T = x1t.shape[1] // TILE_ROWS
    cap = CAPACITY_FACTOR * T // N_EXPERTS
    eg = _experts_per_step(cap, SCATTER_BLOCK_BYTES)
    n_g = N_EXPERTS // eg
    mod_map = (lambda b, g: (b, 0, 0)) if shared_row is None else (lambda b, g: (shared_row, 0, 0))
    smem = lambda: pl.BlockSpec((1, 1, N_EXPERTS * cap), lambda b, g: (b, 0, 0), memory_space=pltpu.SMEM)
    return pl.pallas_call(
        functools.partial(_scatter_kernel, cap=cap, eg=eg, n_g=n_g, T=T),
        grid=(B, n_g),
        scratch_shapes=[pltpu.VMEM((T * TILE_ROWS, LANE), F32), pltpu.VMEM((T * TILE_ROWS, LANE), F32)],
        in_specs=[smem(), smem(),
                  pl.BlockSpec((eg, 1, cap * TILE_ROWS, LANE), lambda b, g: (g, b, 0, 0)),
                  pl.BlockSpec((1, T * TILE_ROWS, LANE), lambda b, g: (b, 0, 0), pipeline_mode=pl.Buffered(1)),
                  pl.BlockSpec((1, 1, 6 * D), mod_map),
                  pl.BlockSpec((2, D), lambda b, g: (0, 0))],
        out_specs=pl.BlockSpec((1, T, D), lambda b, g: (b, 0, 0)),
        out_shape=jax.ShapeDtypeStruct((B, T, D), F32),
        compiler_params=_cparams(("arbitrary", "arbitrary")),
        name="scatter",
    )(idx, gate, y4, x1t, mod3, ln)


def _moe(x1t, logits, mod3, wg_b, wu_b, wd_b, layer, ln2, shared_row):
    B = x1t.shape[0]
    T = x1t.shape[1] // TILE_ROWS
    cap = CAPACITY_FACTOR * T // N_EXPERTS
    idx, gate = _router_call(logits)
    xe = _gather_call(idx, x1t, mod3, shared_row)
    y = _ffn_call(xe.reshape(N_EXPERTS, B * cap * TILE_ROWS, LANE), wg_b, wu_b, wd_b, layer)
    return _scatter_call(idx, gate, y.reshape(N_EXPERTS, B, cap * TILE_ROWS, LANE), x1t, mod3, ln2, shared_row)


def kernel(x, c, ctx, c_ctx, w_mod, b_mod, w_in, b_in, conv_w, gla_w2, gla_b2, mlstm_norm_g, gla_norm_g, rpb, w_out, ln1_g, ln1_b, w_router, w_gate, w_up, w_down, ln2_g, ln2_b):
    B, T, _ = x.shape
    n_mod = -(-(B + 1) // 8) * 8
    c_all = jnp.concatenate([c, c_ctx[None], jnp.zeros((n_mod - B - 1, D), F32)], 0)
    mods = _mod_call(c_all, w_mod, b_mod)
    w_p = _repack_columns(w_in).astype(BF16)
    b_p = _repack_columns(b_in)
    bias = _natten_bias_tables(rpb)
    mcos, msin = _rope_tables(T, 4, 16)
    gcos, gsin = _rope_tables(T, 4, 8)
    w2p, b2p = _gla_gate_weights(gla_w2, gla_b2)
    wout_b = w_out.astype(BF16)
    wrt_b = jnp.swapaxes(w_router, 1, 2).astype(BF16)
    wg_b = w_gate.astype(BF16).reshape(DEPTH * N_EXPERTS, D, EXPERT_FF)
    wu_b = w_up.astype(BF16).reshape(DEPTH * N_EXPERTS, D, EXPERT_FF)
    wd_b = w_down.astype(BF16).reshape(DEPTH * N_EXPERTS, EXPERT_FF, D)
    ng = jnp.stack([mlstm_norm_g, gla_norm_g], 1)
    ln1 = jnp.stack([ln1_g, ln1_b], 1)
    ln2 = jnp.stack([ln2_g, ln2_b], 1)
    for l in range(DEPTH):
        mod3 = mods[l].reshape(n_mod, 1, 6 * D)
        p_lat, pn_lat = _inproj_call(x, mod3, w_p[l], b_p[l][None], None)
        p_ctx, pn_ctx = _inproj_call(ctx, mod3, w_p[l], b_p[l][None], B)
        n_lat, n_ctx = _natten_call(pn_lat, pn_ctx, bias[l])
        m_lat, m_ctx = _mlstm_call(p_lat, p_ctx, mcos, msin, conv_w[l])
        g_lat, g_ctx = _gla_call(p_lat, p_ctx, gcos, gsin, w2p[l], b2p[l])
        x1t, lg = _merge_call(m_lat, g_lat, n_lat, p_lat, x, mod3, wout_b[l], wrt_b[l], ng[l], ln1[l], None)
        x = _moe(x1t, lg, mod3, wg_b, wu_b, wd_b, l, ln2[l], None)
        if l < DEPTH - 1:
            c1t, lgc = _merge_call(m_ctx, g_ctx, n_ctx, p_ctx, ctx, mod3, wout_b[l], wrt_b[l], ng[l], ln1[l], B)
            ctx = _moe(c1t, lgc, mod3, wg_b, wu_b, wd_b, l, ln2[l], B)
    return x
```

```python
import functools

import numpy as np
import jax
import jax.numpy as jnp
from jax import lax
from jax.experimental import pallas as pl
from jax.experimental.pallas import tpu as pltpu

F32 = jnp.float32
BF16 = jnp.bfloat16

D = 1024
DEPTH = 4
GRID_W = 64
HEAD_DIM = 64
WIN_ROWS = 8
WIN_COLS = 16
CONV_K = 3
ROPE_BASE = 10000.0
N_EXPERTS = 16
EXPERT_FF = 2 * D
CAPACITY_FACTOR = 2
LN_EPS = 1e-5
GLA_TAU = 16.0
DEEPNORM_ALPHA = (2 * DEPTH) ** 0.25
NEG = -1e30

VMEM_LIMIT = 56 * 1024 * 1024
LANE = 128
CHUNK = 256

C_MQK, C_MV, C_MO, C_GV, C_GR = 0, 512, 768, 1024, 1280
C_GQ, C_GK, C_SM = 1536, 1664, 1792
PM = 1920
C_NQ, C_NK, C_NV = 0, 512, 1024
PN = 1536
PW = PM + PN
_MAIN_SEGMENTS = ((0, 512), (512, 768), (768, 1024), (1296, 1552), (1552, 1808), (1040, 1168), (1168, 1296),
                  (1024, 1040), (1808, 1840))
_NAT_SEGMENTS = ((1840, 2352), (2352, 2864), (2864, 3376))
_NPAD = PM - sum(b - a for a, b in _MAIN_SEGMENTS)


def _repack_columns(w):
    parts = ([w[..., a:b] for a, b in _MAIN_SEGMENTS] + [jnp.zeros(w.shape[:-1] + (_NPAD,), w.dtype)]
             + [w[..., a:b] for a, b in _NAT_SEGMENTS])
    return jnp.concatenate(parts, -1)


def _cparams(sem):
    return pltpu.CompilerParams(dimension_semantics=sem, vmem_limit_bytes=VMEM_LIMIT)


def _dot(a, b):
    return jnp.dot(a, b, preferred_element_type=F32)


def _dot_nt(a, b):
    return lax.dot_general(a, b, (((1,), (1,)), ((), ())), preferred_element_type=F32)


def _split3(x):
    hi = x.astype(BF16)
    r1 = x - hi.astype(F32)
    mid = r1.astype(BF16)
    lo = (r1 - mid.astype(F32)).astype(BF16)
    return hi, mid, lo


def _dot_exact_l(m01, x):
    hi, mid, lo = _split3(x)
    return _dot(m01, hi) + _dot(m01, mid) + _dot(m01, lo)


def _dot_exact_r(x, m01):
    hi, mid, lo = _split3(x)
    return _dot(hi, m01) + _dot(mid, m01) + _dot(lo, m01)


def _mod_kernel(c_ref, w_ref, b_ref, o_ref):
    s = jax.nn.silu(c_ref[...]).astype(BF16)
    o_ref[0] = _dot(s, w_ref[0].astype(BF16)) + b_ref[0]


def _mod_call(c_all, w_mod, b_mod):
    rows = c_all.shape[0]
    tn = 1536
    return pl.pallas_call(
        _mod_kernel,
        grid=(DEPTH, 6 * D // tn),
        in_specs=[pl.BlockSpec((rows, D), lambda l, j: (0, 0)),
                  pl.BlockSpec((1, D, tn), lambda l, j: (l, 0, j)),
                  pl.BlockSpec((1, 1, tn), lambda l, j: (l, 0, j))],
        out_specs=pl.BlockSpec((1, rows, tn), lambda l, j: (l, 0, j)),
        out_shape=jax.ShapeDtypeStruct((DEPTH, rows, 6 * D), F32),
        compiler_params=_cparams(("arbitrary", "arbitrary")),
        name="mod",
    )(c_all, w_mod, b_mod.reshape(DEPTH, 1, 6 * D))


def _inproj_kernel(x_ref, mod_ref, w_ref, b_ref, o_ref, on_ref):
    m = mod_ref[0]
    u = (x_ref[0] * (1.0 + m[:, D:2 * D]) + m[:, 0:D]).astype(BF16)
    o_ref[0] = _dot(u, w_ref[:, :PM]) + b_ref[:, :PM]
    on_ref[0] = (_dot(u, w_ref[:, PM:]) + b_ref[:, PM:]).astype(BF16)


def _inproj_call(x, mod3, w_p, b_p, shared_row):
    B, T, _ = x.shape
    tm = min(T, 512)
    if shared_row is None:
        mod_map = lambda b, i: (b, 0, 0)
    else:
        mod_map = lambda b, i: (shared_row, 0, 0)
    return pl.pallas_call(
        _inproj_kernel,
        grid=(B, T // tm),
        in_specs=[pl.BlockSpec((1, tm, D), lambda b, i: (b, i, 0)),
                  pl.BlockSpec((1, 1, 6 * D), mod_map),
                  pl.BlockSpec((D, PW), lambda b, i: (0, 0)),
                  pl.BlockSpec((1, PW), lambda b, i: (0, 0))],
        out_specs=[pl.BlockSpec((1, tm, PM), lambda b, i: (b, i, 0)),
                   pl.BlockSpec((1, tm, PN), lambda b, i: (b, i, 0))],
        out_shape=[jax.ShapeDtypeStruct((B, T, PM), F32), jax.ShapeDtypeStruct((B, T, PN), BF16)],
        compiler_params=_cparams(("arbitrary", "arbitrary")),
        name="inproj",
    )(x, mod3, w_p, b_p)


def _softmax_av(s_list, v_list):
    m = s_list[0].max(-1, keepdims=True)
    for s in s_list[1:]:
        m = jnp.maximum(m, s.max(-1, keepdims=True))
    acc = None
    l = None
    for s, v in zip(s_list, v_list):
        p = jnp.exp(s - m)
        ls = p.sum(-1, keepdims=True)
        o = _dot(p.astype(BF16), v)
        acc = o if acc is None else acc + o
        l = ls if l is None else l + ls
    return acc / l


NAT_GROUP = 8


def _natten_kernel(q_ref, k_ref, v_ref, qc_ref, kc_ref, vc_ref, bias_ref, o_ref, oc_ref, *, n_rows):
    kcb = kc_ref[0]
    vcb = vc_ref[0]
    lane = lax.broadcasted_iota(jnp.int32, (1, LANE), 1)
    head0 = lane < HEAD_DIM
    scale = jnp.asarray(HEAD_DIM ** -0.5, BF16)
    zero = jnp.zeros((), BF16)

    def stack_heads(q):
        return jnp.concatenate([jnp.where(head0, q, zero), jnp.where(head0, zero, q)], 0)

    def unstack(o, n):
        return jnp.where(head0, o[:n], o[n:])

    def rows_body(g, carry):
        koffs, scores = [], []
        for i in range(NAT_GROUP):
            r = g * NAT_GROUP + i
            rs = jnp.clip(r - WIN_ROWS // 2, 0, n_rows - WIN_ROWS)
            qs = stack_heads(q_ref[0, pl.ds(pl.multiple_of(r * GRID_W, GRID_W), GRID_W), :] * scale)
            koff = pl.multiple_of(rs * GRID_W, GRID_W)
            kl = k_ref[0, pl.ds(koff, WIN_ROWS * GRID_W), :]
            koffs.append(koff)
            scores.append([_dot_nt(qs, kl) + bias_ref[r - rs, 0], _dot_nt(qs, kcb)])
        for i in range(NAT_GROUP):
            r = g * NAT_GROUP + i
            vl = v_ref[0, pl.ds(koffs[i], WIN_ROWS * GRID_W), :]
            o = _softmax_av(scores[i], [vl, vcb])
            o_ref[0, pl.ds(pl.multiple_of(r * GRID_W, GRID_W), GRID_W), :] = unstack(o, GRID_W)
        return carry

    lax.fori_loop(0, n_rows // NAT_GROUP, rows_body, 0)

    tc = qc_ref.shape[1]
    oc = _softmax_av([_dot_nt(stack_heads(qc_ref[0] * scale), kcb)], [vcb])
    oc_ref[0] = unstack(oc, tc)


def _natten_call(pn_lat, pn_ctx, bias_l):
    B, T, _ = pn_lat.shape
    Tc = pn_ctx.shape[1]
    n_rows = T // GRID_W
    assert n_rows >= WIN_ROWS
    cb = lambda base: (lambda b, p: (b, 0, base // LANE + p))
    return pl.pallas_call(
        functools.partial(_natten_kernel, n_rows=n_rows),
        grid=(B, 4),
        in_specs=[pl.BlockSpec((1, T, LANE), cb(C_NQ)),
                  pl.BlockSpec((1, T, LANE), cb(C_NK)),
                  pl.BlockSpec((1, T, LANE), cb(C_NV)),
                  pl.BlockSpec((1, Tc, LANE), cb(C_NQ)),
                  pl.BlockSpec((1, Tc, LANE), cb(C_NK)),
                  pl.BlockSpec((1, Tc, LANE), cb(C_NV)),
                  pl.BlockSpec((WIN_ROWS, 1, 2 * GRID_W, WIN_ROWS * GRID_W), lambda b, p: (0, p, 0, 0))],
        out_specs=[pl.BlockSpec((1, T, LANE), lambda b, p: (b, 0, p)),
                   pl.BlockSpec((1, Tc, LANE), lambda b, p: (b, 0, p))],
        out_shape=[jax.ShapeDtypeStruct((B, T, 4 * LANE), F32),
                   jax.ShapeDtypeStruct((B, Tc, 4 * LANE), F32)],
        compiler_params=_cparams(("arbitrary", "arbitrary")),
        name="natten",
    )(pn_lat, pn_lat, pn_lat, pn_ctx, pn_ctx, pn_ctx, bias_l)


def _natten_bias_tables(rpb):
    col = np.arange(GRID_W)
    cstart = np.clip(col - WIN_COLS // 2, 0, GRID_W - WIN_COLS)
    col_ok = (col[None, :] >= cstart[:, None]) & (col[None, :] < cstart[:, None] + WIN_COLS)
    dc_idx = np.clip(col[None, :] - col[:, None] + WIN_COLS - 1, 0, 2 * WIN_COLS - 2)
    toe = rpb.astype(F32)[..., dc_idx]
    toe = jnp.where(col_ok[None, None, None], toe, NEG)
    t = jnp.stack([toe[:, :, WIN_ROWS - 1 - d:2 * WIN_ROWS - 1 - d] for d in range(WIN_ROWS)], 1)
    t = t.transpose(0, 1, 2, 4, 3, 5)
    return t.reshape(DEPTH, WIN_ROWS, 4, 2 * GRID_W, WIN_ROWS * GRID_W)


def _tri_masks():
    r = lax.broadcasted_iota(jnp.int32, (CHUNK, CHUNK), 0)
    c = lax.broadcasted_iota(jnp.int32, (CHUNK, CHUNK), 1)
    return r >= c, r <= c


def _rope(x, cs, sn, first, dist):
    w = x.shape[-1]
    partner = jnp.where(first, pltpu.roll(x, w - dist, 1), pltpu.roll(x, dist, 1))
    return x * cs + partner * sn


def _rope_tables(T, n_heads, half):
    t = jnp.arange(T)
    rows = (t // GRID_W).astype(F32)
    cols = (t % GRID_W).astype(F32)
    inv = ROPE_BASE ** (-jnp.arange(half, dtype=F32) / half)
    ar = rows[:, None] * inv[None, :]
    ac = cols[:, None] * inv[None, :]
    cos = jnp.concatenate([jnp.cos(ar), jnp.cos(ar), jnp.cos(ac), jnp.cos(ac)], -1)
    sin = jnp.concatenate([-jnp.sin(ar), jnp.sin(ar), -jnp.sin(ac), jnp.sin(ac)], -1)
    return jnp.tile(cos, (1, n_heads)), jnp.tile(sin, (1, n_heads))


def _bwd_chunk(c, nc, n):
    return jnp.where(c < nc, nc - 1 - c, n - 1 - c + nc)


def _mlstm_kernel(qkl_ref, vl_ref, sml_ref, qkc_ref, vc_ref, smc_ref, cos_ref, sin_ref, cw_ref,
                  ol_ref, oc_ref,
                  qt_s, k_s, vat_s, gcol_s, grow_s, brow_s, cm_s, ot_s, ck_s, *, nc, n):
    low, upp = _tri_masks()
    tri = jnp.where(low, 1.0, 0.0).astype(BF16)
    triu = jnp.where(upp, 1.0, 0.0).astype(BF16)
    lane128 = lax.broadcasted_iota(jnp.int32, (1, LANE), 1)
    lane256 = lax.broadcasted_iota(jnp.int32, (1, 2 * LANE), 1)
    rowi = lax.broadcasted_iota(jnp.int32, (CHUNK, 1), 0)
    row16 = lax.broadcasted_iota(jnp.int32, (16, 1), 0)
    first = (lane256 % 32) < 16
    hmask = [(lane256 >= HEAD_DIM * h) & (lane256 < HEAD_DIM * (h + 1)) for h in range(4)]
    ones_blk = jnp.where(lax.broadcasted_iota(jnp.int32, (HEAD_DIM, CHUNK), 0) == 0, 1.0, 0.0)
    fwd_rows = row16 < 8
    cw = cw_ref[...]

    def cummax_lanes(x, suffix):
        sh = 1
        while sh < CHUNK:
            if suffix:
                moved = jnp.where(lane256 < CHUNK - sh, pltpu.roll(x, CHUNK - sh, 1), NEG)
            else:
                moved = jnp.where(lane256 >= sh, pltpu.roll(x, sh, 1), NEG)
            x = jnp.maximum(x, moved)
            sh *= 2
        return x

    def prep(qk_ref, v_ref, sm_ref, ci, n_str, dst, use_rope):
        r0 = ci * CHUNK
        xc = qk_ref[0, r0:r0 + CHUNK, :]
        prev = qk_ref[0, r0 - 1:r0, :] if ci > 0 else jnp.zeros((1, 4 * LANE), F32)
        nxt = qk_ref[0, r0 + CHUNK:r0 + CHUNK + 1, :] if ci < n_str - 1 else jnp.zeros((1, 4 * LANE), F32)
        xp = jnp.where(rowi == 0, prev, pltpu.roll(xc, 1, 0))
        xn = jnp.where(rowi == CHUNK - 1, nxt, pltpu.roll(xc, CHUNK - 1, 0))
        y = jax.nn.silu(xp * cw[0:1] + xc * cw[1:2] + xn * cw[2:3])
        q = y[:, :2 * LANE]
        k = y[:, 2 * LANE:] * HEAD_DIM ** -0.5
        if use_rope:
            cs = cos_ref[r0:r0 + CHUNK, :]
            sn = sin_ref[r0:r0 + CHUNK, :]
            q = _rope(q, cs, sn, first, 16)
            k = _rope(k, cs, sn, first, 16)
        qt_s[dst] = q.T.astype(BF16)
        k_s[dst] = k.astype(BF16)
        vt = v_ref[0, r0:r0 + CHUNK, :].T
        for h in range(4):
            vat_s[h, dst] = jnp.concatenate([vt[HEAD_DIM * h:HEAD_DIM * (h + 1)], ones_blk], 0).astype(BF16)
        g = sm_ref[0, r0:r0 + CHUNK, :]
        lf = pltpu.roll(jax.nn.log_sigmoid(g), LANE - 4, 1)
        gcol_s[dst] = g - jnp.where(lane128 < 8, _dot_exact_l(tri, lf), _dot_exact_l(triu, lf))
        gt = g.T[0:16]
        lft = pltpu.roll(jax.nn.log_sigmoid(gt), 12, 0)
        brow = jnp.where(fwd_rows, _dot_exact_r(lft, triu), _dot_exact_r(lft, tri))
        brow_s[dst] = brow
        grow_s[dst] = gt - brow

    for ci in range(nc):
        prep(qkc_ref, vc_ref, smc_ref, ci, nc, ci, False)
    for ci in range(n - nc):
        prep(qkl_ref, vl_ref, sml_ref, ci, n - nc, nc + ci, True)
    g_all = grow_s[...].reshape(n * 16, CHUNK)
    fwd_all = (lax.broadcasted_iota(jnp.int32, (n * 16, 1), 0) % 16) < 8
    cm_s[...] = jnp.where(fwd_all, cummax_lanes(g_all, False), cummax_lanes(g_all, True)).reshape(n, 16, CHUNK)

    def scan_dir(bwd):
        di = 1 if bwd else 0
        gi = 8 if bwd else 0
        causal = low if bwd else upp

        def matmuls_first(c):
            ch = _bwd_chunk(c, nc, n) if bwd else c
            qt = qt_s[ch]
            kb = k_s[ch]
            ck_b = ck_s[di].astype(BF16)
            scores, inters = [], []
            for h in range(4):
                pr = slice(LANE * (h // 2), LANE * (h // 2 + 1))
                own = (lane128 < HEAD_DIM) if h % 2 == 0 else (lane128 >= HEAD_DIM)
                zero = jnp.zeros((), BF16)
                scores.append(_dot(jnp.where(own, kb[:, pr], zero), qt[pr, :]))
                inters.append(_dot(jnp.where(own, ck_b[:, pr], zero), qt[pr, :]))
            return ch, kb, scores, inters

        def rest(first, ms):
            ch, kb, scores, inters = first
            gcol_all = gcol_s[ch]
            new_ms = []
            for h in range(4):
                m = ms[h]
                g_row = grow_s[ch, gi + h:gi + h + 1, :]
                b_row = brow_s[ch, gi + h:gi + h + 1, :]
                a_row = jnp.maximum(m, cm_s[ch, gi + h:gi + h + 1, :])
                w = jnp.exp(jnp.where(causal, gcol_all[:, gi + h:gi + h + 1] - a_row, NEG))
                pt = (scores[h] * w).astype(BF16)
                vat = vat_s[h, ch]
                nd = _dot(vat, pt) + jnp.exp(m - a_row) * inters[h]
                den = nd[HEAD_DIM:HEAD_DIM + 1, :]
                ht = nd[0:HEAD_DIM] / jnp.maximum(jnp.abs(den), jnp.exp(-(b_row + a_row)))
                ot_s[di, ch, HEAD_DIM * h:HEAD_DIM * (h + 1), :] = ht
                bl = b_row[:, 0:1] if bwd else b_row[:, CHUNK - 1:CHUNK]
                lw_end = bl + g_row
                m_new = jnp.maximum(bl + m, lw_end.max(-1, keepdims=True))
                upd = _dot((vat * jnp.exp(lw_end - m_new)).astype(BF16), kb)
                ck_s[di] = jnp.where(hmask[h], jnp.exp(bl + m - m_new) * ck_s[di] + upd, ck_s[di])
                new_ms.append(m_new)
            return tuple(new_ms)

        return matmuls_first, rest

    ck_s[...] = jnp.zeros_like(ck_s)
    first_f, rest_f = scan_dir(False)
    first_b, rest_b = scan_dir(True)

    def chunk_body(c, carry):
        ff, fb = first_f(c), first_b(c)
        return rest_f(ff, carry[0]), rest_b(fb, carry[1])

    zeros4 = tuple(jnp.zeros((1, 1), F32) for _ in range(4))
    lax.fori_loop(0, n, chunk_body, (zeros4, zeros4))
    for ci in range(nc):
        oc_ref[0, ci * CHUNK:(ci + 1) * CHUNK, :] = (ot_s[0, ci] + ot_s[1, ci]).T
    for ci in range(n - nc):
        ol_ref[0, ci * CHUNK:(ci + 1) * CHUNK, :] = (ot_s[0, nc + ci] + ot_s[1, nc + ci]).T


def _mlstm_call(p_lat, p_ctx, cos, sin, conv_w_l):
    B, T, _ = p_lat.shape
    Tc = p_ctx.shape[1]
    nc, n = Tc // CHUNK, (Tc + T) // CHUNK
    cb = lambda base, w: (lambda b: (b, 0, base // w))
    return pl.pallas_call(
        functools.partial(_mlstm_kernel, nc=nc, n=n),
        grid=(B,),
        in_specs=[pl.BlockSpec((1, T, 512), cb(C_MQK, 512)),
                  pl.BlockSpec((1, T, 256), cb(C_MV, 256)),
                  pl.BlockSpec((1, T, LANE), cb(C_SM, LANE)),
                  pl.BlockSpec((1, Tc, 512), cb(C_MQK, 512)),
                  pl.BlockSpec((1, Tc, 256), cb(C_MV, 256)),
                  pl.BlockSpec((1, Tc, LANE), cb(C_SM, LANE)),
                  pl.BlockSpec((T, 256), lambda b: (0, 0)),
                  pl.BlockSpec((T, 256), lambda b: (0, 0)),
                  pl.BlockSpec((CONV_K, 512), lambda b: (0, 0))],
        out_specs=[pl.BlockSpec((1, T, 256), lambda b: (b, 0, 0)),
                   pl.BlockSpec((1, Tc, 256), lambda b: (b, 0, 0))],
        out_shape=[jax.ShapeDtypeStruct((B, T, 256), F32),
                   jax.ShapeDtypeStruct((B, Tc, 256), F32)],
        scratch_shapes=[pltpu.VMEM((n, 256, CHUNK), BF16),
                        pltpu.VMEM((n, CHUNK, 256), BF16),
                        pltpu.VMEM((4, n, LANE, CHUNK), BF16),
                        pltpu.VMEM((n, CHUNK, LANE), F32),
                        pltpu.VMEM((n, 16, CHUNK), F32),
                        pltpu.VMEM((n, 16, CHUNK), F32),
                        pltpu.VMEM((n, 16, CHUNK), F32),
                        pltpu.VMEM((2, n, 256, CHUNK), F32),
                        pltpu.VMEM((2, LANE, 256), F32)],
        compiler_params=_cparams(("arbitrary",)),
        name="mlstm",
    )(p_lat, p_lat, p_lat, p_ctx, p_ctx, p_ctx, cos, sin, conv_w_l)


GLA_BLK = 64
GLA_CLAMP = 80.0


def _gla_kernel(ql_ref, kl_ref, vl_ref, sml_ref, qc_ref, kc_ref, vc_ref, smc_ref, cos_ref, sin_ref, w2_ref, b2_ref,
                ol_ref, oc_ref,
                q_s, k_s, v_s, a_s, b_s, o_s, st_s, *, nc, n):
    low, upp = _tri_masks()
    tri = jnp.where(low, 1.0, 0.0).astype(BF16)
    triu = jnp.where(upp, 1.0, 0.0).astype(BF16)
    lane128 = lax.broadcasted_iota(jnp.int32, (1, LANE), 1)
    lane256 = lax.broadcasted_iota(jnp.int32, (1, 2 * LANE), 1)
    first = (lane128 % 16) < 8
    hm128 = [(lane128 >= 32 * h) & (lane128 < 32 * (h + 1)) for h in range(4)]
    hm256 = [(lane256 >= 64 * h) & (lane256 < 64 * (h + 1)) for h in range(4)]
    nb = CHUNK // GLA_BLK
    r_st = lax.broadcasted_iota(jnp.int32, (nb * GLA_BLK, CHUNK), 0)
    c_st = lax.broadcasted_iota(jnp.int32, (nb * GLA_BLK, CHUNK), 1)
    bd_r = lax.broadcasted_iota(jnp.int32, (LANE, 2 * LANE), 0)
    bd_c = lax.broadcasted_iota(jnp.int32, (LANE, 2 * LANE), 1)
    blockdiag = (bd_r // 32) == (bd_c // 64)

    def prep(q_ref, k_ref, v_ref, sm_ref, ci, dst, use_rope):
        r0 = ci * CHUNK
        q = q_ref[0, r0:r0 + CHUNK, :] * 32 ** -0.5
        k = k_ref[0, r0:r0 + CHUNK, :]
        if use_rope:
            cs = cos_ref[r0:r0 + CHUNK, :]
            sn = sin_ref[r0:r0 + CHUNK, :]
            q = _rope(q, cs, sn, first, 8)
            k = _rope(k, cs, sn, first, 8)
        q_s[dst] = q
        k_s[dst] = k
        v_s[dst] = v_ref[0, r0:r0 + CHUNK, :].astype(BF16)
        lr = sm_ref[0, r0:r0 + CHUNK, :].astype(BF16)
        for d in range(2):
            a = jax.nn.log_sigmoid(_dot(lr, w2_ref[d]) + b2_ref[d]) / GLA_TAU
            a_s[d, dst] = a
            b_s[d, dst] = _dot_exact_l(triu if d else tri, a)

    for ci in range(nc):
        prep(qc_ref, kc_ref, vc_ref, smc_ref, ci, ci, False)
    for ci in range(n - nc):
        prep(ql_ref, kl_ref, vl_ref, sml_ref, ci, nc + ci, True)

    def scan_dir(bwd):
        d = 1 if bwd else 0
        st_s[...] = jnp.zeros_like(st_s)

        def chunk_body(c, carry):
            ch = _bwd_chunk(c, nc, n) if bwd else c
            q = q_s[ch]
            k = k_s[ch]
            vb = v_s[ch]
            a = a_s[d, ch]
            b = b_s[d, ch]
            st_b = st_s[...].astype(BF16)
            o_inter = _dot((q * jnp.exp(b)).astype(BF16), st_b)
            atts = []
            for i in range(nb):
                rows = slice(i * GLA_BLK, (i + 1) * GLA_BLK)
                e = (i + 1) * GLA_BLK - 1 if bwd else i * GLA_BLK
                ref = b[e:e + 1, :] - a[e:e + 1, :]
                qs = q[rows] * jnp.exp(b[rows] - ref)
                ks = (k * jnp.exp(jnp.minimum(ref - b, GLA_CLAMP))).astype(BF16)
                lhs = jnp.concatenate([jnp.where(hm128[h], qs, 0.0) for h in range(4)], 0).astype(BF16)
                att = _dot_nt(lhs, ks)
                t_idx = (r_st % GLA_BLK) + i * GLA_BLK
                ok = (c_st >= t_idx) if bwd else (c_st <= t_idx)
                atts.append(jnp.where(ok, att, 0.0).astype(BF16))
            for i in range(nb):
                rows = slice(i * GLA_BLK, (i + 1) * GLA_BLK)
                oh = _dot(atts[i], vb)
                o_blk = o_inter[rows]
                for h in range(4):
                    o_blk = o_blk + jnp.where(hm256[h], oh[h * GLA_BLK:(h + 1) * GLA_BLK], 0.0)
                if bwd:
                    o_s[ch, rows, :] = o_s[ch, rows, :] + o_blk
                else:
                    o_s[ch, rows, :] = o_blk
            bt = b.T
            tot = bt[:, 0:1] if bwd else bt[:, CHUNK - 1:CHUNK]
            kt = (k.T * jnp.exp(tot - bt)).astype(BF16)
            upd = jnp.where(blockdiag, _dot(kt, vb), 0.0)
            st_s[...] = jnp.exp(tot) * st_s[...] + upd
            return carry

        lax.fori_loop(0, n, chunk_body, 0)

    scan_dir(False)
    scan_dir(True)
    for ci in range(nc):
        oc_ref[0, ci * CHUNK:(ci + 1) * CHUNK, :] = o_s[ci]
    for ci in range(n - nc):
        ol_ref[0, ci * CHUNK:(ci + 1) * CHUNK, :] = o_s[nc + ci]


def _gla_call(p_lat, p_ctx, cos, sin, w2p, b2p):
    B, T, _ = p_lat.shape
    Tc = p_ctx.shape[1]
    nc, n = Tc // CHUNK, (Tc + T) // CHUNK
    cb = lambda base, w: (lambda b: (b, 0, base // w))
    return pl.pallas_call(
        functools.partial(_gla_kernel, nc=nc, n=n),
        grid=(B,),
        in_specs=[pl.BlockSpec((1, T, LANE), cb(C_GQ, LANE)),
                  pl.BlockSpec((1, T, LANE), cb(C_GK, LANE)),
                  pl.BlockSpec((1, T, 256), cb(C_GV, 256)),
                  pl.BlockSpec((1, T, LANE), cb(C_SM, LANE)),
                  pl.BlockSpec((1, Tc, LANE), cb(C_GQ, LANE)),
                  pl.BlockSpec((1, Tc, LANE), cb(C_GK, LANE)),
                  pl.BlockSpec((1, Tc, 256), cb(C_GV, 256)),
                  pl.BlockSpec((1, Tc, LANE), cb(C_SM, LANE)),
                  pl.BlockSpec((T, LANE), lambda b: (0, 0)),
                  pl.BlockSpec((T, LANE), lambda b: (0, 0)),
                  pl.BlockSpec((2, LANE, LANE), lambda b: (0, 0, 0)),
                  pl.BlockSpec((2, 1, LANE), lambda b: (0, 0, 0))],
        out_specs=[pl.BlockSpec((1, T, 256), lambda b: (b, 0, 0)),
                   pl.BlockSpec((1, Tc, 256), lambda b: (b, 0, 0))],
        out_shape=[jax.ShapeDtypeStruct((B, T, 256), F32),
                   jax.ShapeDtypeStruct((B, Tc, 256), F32)],
        scratch_shapes=[pltpu.VMEM((n, CHUNK, LANE), F32),
                        pltpu.VMEM((n, CHUNK, LANE), F32),
                        pltpu.VMEM((n, CHUNK, 256), BF16),
                        pltpu.VMEM((2, n, CHUNK, LANE), F32),
                        pltpu.VMEM((2, n, CHUNK, LANE), F32),
                        pltpu.VMEM((n, CHUNK, 256), F32),
                        pltpu.VMEM((LANE, 256), F32)],
        compiler_params=_cparams(("arbitrary",)),
        name="gla",
    )(p_lat, p_lat, p_lat, p_lat, p_ctx, p_ctx, p_ctx, p_ctx, cos, sin, w2p, b2p)


def _gla_gate_weights(gla_w2, gla_b2):
    w = jnp.zeros((DEPTH, 2, LANE, LANE), F32)
    w = w.at[:, 0, 16:32].set(gla_w2[:, 0]).at[:, 1, 32:48].set(gla_w2[:, 1])
    return w.astype(BF16), gla_b2.astype(F32).reshape(DEPTH, 2, 1, LANE)


TILE_ROWS = D // LANE


def _store_token_tiles(ref, val):
    tm = val.shape[0]
    for s in range(TILE_ROWS):
        ref[0, pl.ds(s, tm, stride=TILE_ROWS), :] = val[:, s * LANE:(s + 1) * LANE]


def _load_token_tiles(ref, lead, t0, tm):
    return [ref[lead + (pl.ds(t0 * TILE_ROWS + s, tm, stride=TILE_ROWS), slice(None))] for s in range(TILE_ROWS)]


def _layer_norm(z, g, b):
    mu = z.mean(-1, keepdims=True)
    zc = z - mu
    var = jnp.mean(jnp.square(zc), -1, keepdims=True)
    return zc * lax.rsqrt(var + LN_EPS) * g + b


def _merge_kernel(hm_ref, hg_ref, hn_ref, mo_ref, gr_ref, x_ref, mod_ref, wout_ref, wr_ref, ng_ref, ln_ref,
                  x1_ref, lg_ref):
    r = lax.broadcasted_iota(jnp.int32, (256, 256), 0)
    c = lax.broadcasted_iota(jnp.int32, (256, 256), 1)
    avg = jnp.where((r // HEAD_DIM) == (c // HEAD_DIM), 1.0 / HEAD_DIM, 0.0).astype(BF16)

    def seg_mean(x):
        hi = x.astype(BF16)
        lo = (x - hi.astype(F32)).astype(BF16)
        return _dot(hi, avg) + _dot(lo, avg)

    def head_norm(h):
        d = h - seg_mean(h)
        return d * lax.rsqrt(seg_mean(d * d) + LN_EPS)

    ym = head_norm(hm_ref[0]) * ng_ref[0:1, :] * jax.nn.sigmoid(mo_ref[0])
    yg = head_norm(hg_ref[0]) * ng_ref[1:2, :] * jax.nn.silu(gr_ref[0])
    y = (_dot(ym.astype(BF16), wout_ref[0:256, :]) + _dot(yg.astype(BF16), wout_ref[256:512, :])
         + _dot(hn_ref[0].astype(BF16), wout_ref[512:1024, :]))
    m = mod_ref[0]
    x1 = _layer_norm(DEEPNORM_ALPHA * x_ref[0] + m[:, 2 * D:3 * D] * y, ln_ref[0:1, :], ln_ref[1:2, :])
    _store_token_tiles(x1_ref, x1)
    u2 = x1 * (1.0 + m[:, 4 * D:5 * D]) + m[:, 3 * D:4 * D]
    lg_ref[0] = _dot_nt(wr_ref[...], u2.astype(BF16))


def _merge_call(hm, hg, hn, p, x, mod3, wout_b, wrt_b, ng, ln, shared_row):
    B, T, _ = x.shape
    tm = min(T, 512)
    mod_map = (lambda b, i: (b, 0, 0)) if shared_row is None else (lambda b, i: (shared_row, 0, 0))
    tok = lambda w: pl.BlockSpec((1, tm, w), lambda b, i: (b, i, 0))
    return pl.pallas_call(
        _merge_kernel,
        grid=(B, T // tm),
        in_specs=[tok(256), tok(256), tok(512),
                  pl.BlockSpec((1, tm, 256), lambda b, i: (b, i, C_MO // 256)),
                  pl.BlockSpec((1, tm, 256), lambda b, i: (b, i, C_GR // 256)),
                  tok(D),
                  pl.BlockSpec((1, 1, 6 * D), mod_map),
                  pl.BlockSpec((D, D), lambda b, i: (0, 0)),
                  pl.BlockSpec((N_EXPERTS, D), lambda b, i: (0, 0)),
                  pl.BlockSpec((2, 256), lambda b, i: (0, 0)),
                  pl.BlockSpec((2, D), lambda b, i: (0, 0))],
        out_specs=[pl.BlockSpec((1, tm * TILE_ROWS, LANE), lambda b, i: (b, i, 0)),
                   pl.BlockSpec((1, N_EXPERTS, tm), lambda b, i: (b, 0, i))],
        out_shape=[jax.ShapeDtypeStruct((B, T * TILE_ROWS, LANE), F32),
                   jax.ShapeDtypeStruct((B, N_EXPERTS, T), F32)],
        compiler_params=_cparams(("arbitrary", "arbitrary")),
        name="merge",
    )(hm, hg, hn, p, p, x, mod3, wout_b, wrt_b, ng, ln)


ROUTER_SAMPLES = 8


def _router_samples(B):
    ns = ROUTER_SAMPLES
    while B % ns:
        ns -= 1
    return ns


def _router_kernel(lg_ref, o_ref, aff_s, sp_s, *, T, cap):
    J = T // LANE
    E = N_EXPERTS
    NS = lg_ref.shape[0]
    keys = [[] for _ in range(NS)]
    for j in range(J):
        for si in range(NS):
            lg = lg_ref[si, :, j * LANE:(j + 1) * LANE]
            ex = jnp.exp(lg - lg.max(0, keepdims=True))
            aff = ex / ex.sum(0, keepdims=True)
            aff_s[si, j * E:(j + 1) * E, :] = aff
            keys[si].append(pltpu.bitcast(aff, jnp.int32))

    def count(ks, pred):
        cnt = None
        for k in ks:
            cj = jnp.where(pred(k), 1.0, 0.0)
            cnt = cj if cnt is None else cnt + cj
        return cnt.sum(-1, keepdims=True)

    thrs = [jnp.zeros((E, 1), jnp.int32) for _ in range(NS)]
    for bit in range(30, -1, -1):
        for si in range(NS):
            cand = thrs[si] | (1 << bit)
            thrs[si] = jnp.where(count(keys[si], lambda k: k >= cand) >= cap, cand, thrs[si])

    r = lax.broadcasted_iota(jnp.int32, (LANE, LANE), 0)
    c = lax.broadcasted_iota(jnp.int32, (LANE, LANE), 1)
    upper = jnp.where(r <= c, 1.0, 0.0).astype(BF16)
    ones = jnp.ones((LANE, LANE), BF16)
    rr = lax.broadcasted_iota(jnp.int32, (J * E, J * E), 0)
    cc = lax.broadcasted_iota(jnp.int32, (J * E, J * E), 1)
    earlier = jnp.where(((rr % E) == (cc % E)) & ((cc // E) < (rr // E)), 1.0, 0.0).astype(BF16)

    def prefix(x01):
        xb = x01.astype(BF16)
        return _dot(xb, upper) + _dot(earlier, _dot(xb, ones).astype(BF16))

    for si in range(NS):
        thr = thrs[si]
        need = cap - count(keys[si], lambda k: k > thr)
        gt = jnp.concatenate([jnp.where(k > thr, 1.0, 0.0) for k in keys[si]], 0)
        eq = jnp.concatenate([jnp.where(k == thr, 1.0, 0.0) for k in keys[si]], 0)
        need_t = jnp.concatenate([need] * J, 0)
        sel = jnp.maximum(gt, jnp.where(prefix(eq) <= need_t, eq, 0.0))
        sp_s[si] = jnp.where(sel > 0.0, prefix(sel) - 1.0, -1.0)

    sb = min(cap, LANE)
    lane = lax.broadcasted_iota(jnp.int32, (1, LANE), 1)
    o_ref[...] = jnp.zeros_like(o_ref)
    for e in range(E):
        for half in range(cap // sb):
            slot = (lax.broadcasted_iota(jnp.int32, (sb, LANE), 0) + half * sb).astype(F32)
            rows = slice(half * sb, (half + 1) * sb)
            for si in range(NS):
                def jbody(j, acc):
                    acc_i, acc_g = acc
                    sp = sp_s[si, pl.ds(j * E + e, 1), :]
                    af = aff_s[si, pl.ds(j * E + e, 1), :]
                    hit = sp == slot
                    tid = (lane + j * LANE).astype(F32)
                    return jnp.where(hit, tid, acc_i), jnp.where(hit, af, acc_g)

                acc_i, acc_g = lax.fori_loop(0, J, jbody, (jnp.zeros((sb, LANE), F32), jnp.zeros((sb, LANE), F32)),
                                             unroll=2)
                icol = acc_i.sum(-1, keepdims=True)
                gcol = acc_g.sum(-1, keepdims=True)
                o_ref[si, rows, :] = jnp.where(lane == e, icol, jnp.where(lane == E + e, gcol, o_ref[si, rows, :]))


def _router_call(logits):
    B, _, T = logits.shape
    cap = CAPACITY_FACTOR * T // N_EXPERTS
    J = T // LANE
    ns = _router_samples(B)
    out = pl.pallas_call(
        functools.partial(_router_kernel, T=T, cap=cap),
        grid=(B // ns,),
        in_specs=[pl.BlockSpec((ns, N_EXPERTS, T), lambda b: (b, 0, 0))],
        out_specs=pl.BlockSpec((ns, cap, LANE), lambda b: (b, 0, 0)),
        out_shape=jax.ShapeDtypeStruct((B, cap, LANE), F32),
        scratch_shapes=[pltpu.VMEM((ns, J * N_EXPERTS, LANE), F32), pltpu.VMEM((ns, J * N_EXPERTS, LANE), F32)],
        compiler_params=_cparams(("arbitrary",)),
        name="router",
    )(logits)
    idx = out[:, :, :N_EXPERTS].astype(jnp.int32).transpose(0, 2, 1).reshape(B, 1, N_EXPERTS * cap)
    gate = out[:, :, N_EXPERTS:2 * N_EXPERTS].transpose(0, 2, 1).reshape(B, 1, N_EXPERTS * cap)
    return idx, gate


GATHER_BLOCK_BYTES = 4 * 1024 * 1024
SCATTER_BLOCK_BYTES = 4 * 1024 * 1024


def _experts_per_step(cap, block_bytes):
    eg = 2
    while eg < N_EXPERTS and 2 * eg * cap * D * 4 <= block_bytes:
        eg *= 2
    return eg


def _as_token_tile(v):
    return jnp.concatenate([v[:, s * LANE:(s + 1) * LANE] for s in range(TILE_ROWS)], 0)


def _gather_kernel(idx_ref, x_ref, mod_ref, o_ref, *, cap, eg):
    g = pl.program_id(1)
    m = mod_ref[0]
    scale = _as_token_tile(1.0 + m[:, 4 * D:5 * D])
    shift = _as_token_tile(m[:, 3 * D:4 * D])
    half = cap // 2

    def row_bits(r):
        src = pl.ds(pl.multiple_of(r * TILE_ROWS, TILE_ROWS), TILE_ROWS)
        u = (x_ref[0, src, :] * scale + shift).astype(BF16).astype(F32)
        return pltpu.bitcast(u, jnp.uint32)

    for e in range(eg):
        def body(s, carry):
            base = (g * eg + e) * cap + s
            lo = row_bits(idx_ref[0, 0, base]) >> 16
            hi = row_bits(idx_ref[0, 0, base + half]) & jnp.uint32(0xFFFF0000)
            o_ref[e, 0, pl.ds(pl.multiple_of(s * TILE_ROWS, TILE_ROWS), TILE_ROWS), :] = lo | hi
            return carry
        lax.fori_loop(0, half, body, 0, unroll=4)


def _gather_call(idx, x1t, mod3, shared_row):
    B = x1t.shape[0]
    T = x1t.shape[1] // TILE_ROWS
    cap = CAPACITY_FACTOR * T // N_EXPERTS
    eg = _experts_per_step(cap, GATHER_BLOCK_BYTES)
    mod_map = (lambda b, g: (b, 0, 0)) if shared_row is None else (lambda b, g: (shared_row, 0, 0))
    smem = lambda: pl.BlockSpec((1, 1, N_EXPERTS * cap), lambda b, g: (b, 0, 0), memory_space=pltpu.SMEM)
    return pl.pallas_call(
        functools.partial(_gather_kernel, cap=cap, eg=eg),
        grid=(B, N_EXPERTS // eg),
        in_specs=[smem(), pl.BlockSpec((1, T * TILE_ROWS, LANE), lambda b, g: (b, 0, 0)),
                  pl.BlockSpec((1, 1, 6 * D), mod_map)],
        out_specs=pl.BlockSpec((eg, 1, cap // 2 * TILE_ROWS, LANE), lambda b, g: (g, b, 0, 0)),
        out_shape=jax.ShapeDtypeStruct((N_EXPERTS, B, cap // 2 * TILE_ROWS, LANE), jnp.uint32),
        compiler_params=_cparams(("arbitrary", "arbitrary")),
        name="gather",
    )(idx, x1t, mod3)


FF_CHUNK = 512


def _ffn_kernel(x_ref, wg_ref, wu_ref, wd_ref, o_ref, xs, *, tm, half):
    for s, words in enumerate(_load_token_tiles(x_ref, (0,), 0, tm // 2)):
        lo = pltpu.bitcast(words << 16, F32).astype(BF16)
        hi = pltpu.bitcast(words & jnp.uint32(0xFFFF0000), F32).astype(BF16)
        for blk in range(tm // (2 * half)):
            src = slice(blk * half, (blk + 1) * half)
            xs[2 * blk * half:(2 * blk + 1) * half, s * LANE:(s + 1) * LANE] = lo[src]
            xs[(2 * blk + 1) * half:(2 * blk + 2) * half, s * LANE:(s + 1) * LANE] = hi[src]
    xb = xs[...]
    acc = None
    for c in range(EXPERT_FF // FF_CHUNK):
        cols = slice(c * FF_CHUNK, (c + 1) * FF_CHUNK)
        h = (jax.nn.silu(_dot(xb, wg_ref[0, :, cols])) * _dot(xb, wu_ref[0, :, cols])).astype(BF16)
        t = _dot(h, wd_ref[0, cols, :])
        acc = t if acc is None else acc + t
    _store_token_tiles(o_ref, acc)


def _ffn_call(xe, wg_b, wu_b, wd_b, layer, cap):
    E = xe.shape[0]
    M = 2 * xe.shape[1] // TILE_ROWS
    tm = min(M, 512)
    assert tm % cap == 0
    w_map = lambda e, i: (layer * E + e, 0, 0)
    return pl.pallas_call(
        functools.partial(_ffn_kernel, tm=tm, half=cap // 2),
        grid=(E, M // tm),
        in_specs=[pl.BlockSpec((1, tm // 2 * TILE_ROWS, LANE), lambda e, i: (e, i, 0)),
                  pl.BlockSpec((1, D, EXPERT_FF), w_map),
                  pl.BlockSpec((1, D, EXPERT_FF), w_map),
                  pl.BlockSpec((1, EXPERT_FF, D), w_map)],
        out_specs=pl.BlockSpec((1, tm * TILE_ROWS, LANE), lambda e, i: (e, i, 0)),
        out_shape=jax.ShapeDtypeStruct((E, M * TILE_ROWS, LANE), F32),
        scratch_shapes=[pltpu.VMEM((tm, D), BF16)],
        compiler_params=_cparams(("arbitrary", "arbitrary")),
        name="ffn",
    )(xe, wg_b, wu_b, wd_b)


SCATTER_FIN_ROWS = 256


def _scatter_kernel(idx_ref, gate_ref, y_ref, x1_ref, mod_ref, ln_ref, o_ref, acc_a, acc_b, *, cap, eg, n_g, T):
    g = pl.program_id(1)

    @pl.when(g == 0)
    def _():
        acc_a[...] = jnp.zeros_like(acc_a)
        acc_b[...] = jnp.zeros_like(acc_b)

    for pair in range(eg // 2):
        def body(s, carry):
            p0 = (g * eg + 2 * pair) * cap + s
            p1 = p0 + cap
            src = pl.ds(pl.multiple_of(s * TILE_ROWS, TILE_ROWS), TILE_ROWS)
            d0 = pl.ds(pl.multiple_of(idx_ref[0, 0, p0] * TILE_ROWS, TILE_ROWS), TILE_ROWS)
            d1 = pl.ds(pl.multiple_of(idx_ref[0, 0, p1] * TILE_ROWS, TILE_ROWS), TILE_ROWS)
            acc_a[d0, :] = acc_a[d0, :] + y_ref[2 * pair, 0, src, :] * gate_ref[0, 0, p0]
            acc_b[d1, :] = acc_b[d1, :] + y_ref[2 * pair + 1, 0, src, :] * gate_ref[0, 0, p1]
            return carry
        lax.fori_loop(0, cap, body, 0, unroll=8)

    @pl.when(g == n_g - 1)
    def _():
        g2 = mod_ref[0][:, 5 * D:6 * D]
        tb = min(T, SCATTER_FIN_ROWS)
        for i in range(T // tb):
            rows = slice(i * tb, (i + 1) * tb)
            f = jnp.concatenate([pa + pb for pa, pb in zip(_load_token_tiles(acc_a, (), i * tb, tb),
                                                             _load_token_tiles(acc_b, (), i * tb, tb))], -1)
            x1 = jnp.concatenate(_load_token_tiles(x1_ref, (0,), i * tb, tb), -1)
            z = DEEPNORM_ALPHA * x1 + g2 * f
            o_ref[0, rows, :] = _layer_norm(z, ln_ref[0:1, :], ln_ref[1:2, :])


def _scatter_call(idx, gate, y4, x1t, mod3, ln, shared_row):
    B = x1t.shape[0]
    T = x1t.shape[1] // TILE_ROWS
    cap = CAPACITY_FACTOR * T // N_EXPERTS
    eg = _experts_per_step(cap, SCATTER_BLOCK_BYTES)
    n_g = N_EXPERTS // eg
    mod_map = (lambda b, g: (b, 0, 0)) if shared_row is None else (lambda b, g: (shared_row, 0, 0))
    smem = lambda: pl.BlockSpec((1, 1, N_EXPERTS * cap), lambda b, g: (b, 0, 0), memory_space=pltpu.SMEM)
    return pl.pallas_call(
        functools.partial(_scatter_kernel, cap=cap, eg=eg, n_g=n_g, T=T),
        grid=(B, n_g),
        scratch_shapes=[pltpu.VMEM((T * TILE_ROWS, LANE), F32), pltpu.VMEM((T * TILE_ROWS, LANE), F32)],
        in_specs=[smem(), smem(),
                  pl.BlockSpec((eg, 1, cap * TILE_ROWS, LANE), lambda b, g: (g, b, 0, 0)),
                  pl.BlockSpec((1, T * TILE_ROWS, LANE), lambda b, g: (b, 0, 0), pipeline_mode=pl.Buffered(1)),
                  pl.BlockSpec((1, 1, 6 * D), mod_map),
                  pl.BlockSpec((2, D), lambda b, g: (0, 0))],
        out_specs=pl.BlockSpec((1, T, D), lambda b, g: (b, 0, 0)),
        out_shape=jax.ShapeDtypeStruct((B, T, D), F32),
        compiler_params=_cparams(("arbitrary", "arbitrary")),
        name="scatter",
    )(idx, gate, y4, x1t, mod3, ln)


def _moe(x1t, logits, mod3, wg_b, wu_b, wd_b, layer, ln2, shared_row):
    B = x1t.shape[0]
    T = x1t.shape[1] // TILE_ROWS
    cap = CAPACITY_FACTOR * T // N_EXPERTS
    idx, gate = _router_call(logits)
    xe = _gather_call(idx, x1t, mod3, shared_row)
    y = _ffn_call(xe.reshape(N_EXPERTS, B * (cap // 2) * TILE_ROWS, LANE), wg_b, wu_b, wd_b, layer, cap)
    return _scatter_call(idx, gate, y.reshape(N_EXPERTS, B, cap * TILE_ROWS, LANE), x1t, mod3, ln2, shared_row)


def kernel(x, c, ctx, c_ctx, w_mod, b_mod, w_in, b_in, conv_w, gla_w2, gla_b2, mlstm_norm_g, gla_norm_g, rpb, w_out, ln1_g, ln1_b, w_router, w_gate, w_up, w_down, ln2_g, ln2_b):
    B, T, _ = x.shape
    n_mod = -(-(B + 1) // 8) * 8
    c_all = jnp.concatenate([c, c_ctx[None], jnp.zeros((n_mod - B - 1, D), F32)], 0)
    mods = _mod_call(c_all, w_mod, b_mod)
    w_p = _repack_columns(w_in).astype(BF16)
    b_p = _repack_columns(b_in)
    bias = _natten_bias_tables(rpb)
    mcos, msin = _rope_tables(T, 4, 16)
    gcos, gsin = _rope_tables(T, 4, 8)
    w2p, b2p = _gla_gate_weights(gla_w2, gla_b2)
    wout_b = w_out.astype(BF16)
    wrt_b = jnp.swapaxes(w_router, 1, 2).astype(BF16)
    wg_b = w_gate.astype(BF16).reshape(DEPTH * N_EXPERTS, D, EXPERT_FF)
    wu_b = w_up.astype(BF16).reshape(DEPTH * N_EXPERTS, D, EXPERT_FF)
    wd_b = w_down.astype(BF16).reshape(DEPTH * N_EXPERTS, EXPERT_FF, D)
    ng = jnp.stack([mlstm_norm_g, gla_norm_g], 1)
    ln1 = jnp.stack([ln1_g, ln1_b], 1)
    ln2 = jnp.stack([ln2_g, ln2_b], 1)
    for l in range(DEPTH):
        mod3 = mods[l].reshape(n_mod, 1, 6 * D)
        p_lat, pn_lat = _inproj_call(x, mod3, w_p[l], b_p[l][None], None)
        p_ctx, pn_ctx = _inproj_call(ctx, mod3, w_p[l], b_p[l][None], B)
        n_lat, n_ctx = _natten_call(pn_lat, pn_ctx, bias[l])
        m_lat, m_ctx = _mlstm_call(p_lat, p_ctx, mcos, msin, conv_w[l])
        g_lat, g_ctx = _gla_call(p_lat, p_ctx, gcos, gsin, w2p[l], b2p[l])
        x1t, lg = _merge_call(m_lat, g_lat, n_lat, p_lat, x, mod3, wout_b[l], wrt_b[l], ng[l], ln1[l], None)
        x = _moe(x1t, lg, mod3, wg_b, wu_b, wd_b, l, ln2[l], None)
        if l < DEPTH - 1:
            c1t, lgc = _merge_call(m_ctx, g_ctx, n_ctx, p_ctx, ctx, mod3, wout_b[l], wrt_b[l], ng[l], ln1[l], B)
            ctx = _moe(c1t, lgc, mod3, wg_b, wu_b, wd_b, l, ln2[l], B)
    return x
```

```python
import functools

import numpy as np
import jax
import jax.numpy as jnp
from jax import lax
from jax.experimental import pallas as pl
from jax.experimental.pallas import tpu as pltpu

F32 = jnp.float32
BF16 = jnp.bfloat16

D = 1024
DEPTH = 4
GRID_W = 64
HEAD_DIM = 64
WIN_ROWS = 8
WIN_COLS = 16
CONV_K = 3
ROPE_BASE = 10000.0
N_EXPERTS = 16
EXPERT_FF = 2 * D
CAPACITY_FACTOR = 2
LN_EPS = 1e-5
GLA_TAU = 16.0
DEEPNORM_ALPHA = (2 * DEPTH) ** 0.25
NEG = -1e30

VMEM_LIMIT = 56 * 1024 * 1024
LANE = 128
CHUNK = 256

C_MQK, C_MV, C_MO, C_GV, C_GR = 0, 512, 768, 1024, 1280
C_GQ, C_GK, C_SM = 1536, 1664, 1792
PM = 1920
C_NQ, C_NK, C_NV = 0, 512, 1024
PN = 1536
PW = PM + PN
_MAIN_SEGMENTS = ((0, 512), (512, 768), (768, 1024), (1296, 1552), (1552, 1808), (1040, 1168), (1168, 1296),
                  (1024, 1040), (1808, 1840))
_NAT_SEGMENTS = ((1840, 2352), (2352, 2864), (2864, 3376))
_NPAD = PM - sum(b - a for a, b in _MAIN_SEGMENTS)


def _repack_columns(w):
    parts = ([w[..., a:b] for a, b in _MAIN_SEGMENTS] + [jnp.zeros(w.shape[:-1] + (_NPAD,), w.dtype)]
             + [w[..., a:b] for a, b in _NAT_SEGMENTS])
    return jnp.concatenate(parts, -1)


def _cparams(sem):
    return pltpu.CompilerParams(dimension_semantics=sem, vmem_limit_bytes=VMEM_LIMIT)


def _dot(a, b):
    return jnp.dot(a, b, preferred_element_type=F32)


def _dot_nt(a, b):
    return lax.dot_general(a, b, (((1,), (1,)), ((), ())), preferred_element_type=F32)


def _split3(x):
    hi = x.astype(BF16)
    r1 = x - hi.astype(F32)
    mid = r1.astype(BF16)
    lo = (r1 - mid.astype(F32)).astype(BF16)
    return hi, mid, lo


def _dot_exact_l(m01, x):
    hi, mid, lo = _split3(x)
    return _dot(m01, hi) + _dot(m01, mid) + _dot(m01, lo)


def _dot_exact_r(x, m01):
    hi, mid, lo = _split3(x)
    return _dot(hi, m01) + _dot(mid, m01) + _dot(lo, m01)


def _mod_kernel(c_ref, w_ref, b_ref, o_ref):
    s = jax.nn.silu(c_ref[...]).astype(BF16)
    o_ref[0] = _dot(s, w_ref[0].astype(BF16)) + b_ref[0]


def _mod_call(c_all, w_mod, b_mod):
    rows = c_all.shape[0]
    tn = 1536
    return pl.pallas_call(
        _mod_kernel,
        grid=(DEPTH, 6 * D // tn),
        in_specs=[pl.BlockSpec((rows, D), lambda l, j: (0, 0)),
                  pl.BlockSpec((1, D, tn), lambda l, j: (l, 0, j)),
                  pl.BlockSpec((1, 1, tn), lambda l, j: (l, 0, j))],
        out_specs=pl.BlockSpec((1, rows, tn), lambda l, j: (l, 0, j)),
        out_shape=jax.ShapeDtypeStruct((DEPTH, rows, 6 * D), F32),
        compiler_params=_cparams(("arbitrary", "arbitrary")),
        name="mod",
    )(c_all, w_mod, b_mod.reshape(DEPTH, 1, 6 * D))


def _inproj_kernel(x_ref, mod_ref, w_ref, b_ref, o_ref, on_ref):
    m = mod_ref[0]
    u = (x_ref[0] * (1.0 + m[:, D:2 * D]) + m[:, 0:D]).astype(BF16)
    o_ref[0] = _dot(u, w_ref[:, :PM]) + b_ref[:, :PM]
    on_ref[0] = (_dot(u, w_ref[:, PM:]) + b_ref[:, PM:]).astype(BF16)


def _inproj_call(x, mod3, w_p, b_p, shared_row):
    B, T, _ = x.shape
    tm = min(T, 512)
    if shared_row is None:
        mod_map = lambda b, i: (b, 0, 0)
    else:
        mod_map = lambda b, i: (shared_row, 0, 0)
    return pl.pallas_call(
        _inproj_kernel,
        grid=(B, T // tm),
        in_specs=[pl.BlockSpec((1, tm, D), lambda b, i: (b, i, 0)),
                  pl.BlockSpec((1, 1, 6 * D), mod_map),
                  pl.BlockSpec((D, PW), lambda b, i: (0, 0)),
                  pl.BlockSpec((1, PW), lambda b, i: (0, 0))],
        out_specs=[pl.BlockSpec((1, tm, PM), lambda b, i: (b, i, 0)),
                   pl.BlockSpec((1, tm, PN), lambda b, i: (b, i, 0))],
        out_shape=[jax.ShapeDtypeStruct((B, T, PM), F32), jax.ShapeDtypeStruct((B, T, PN), BF16)],
        compiler_params=_cparams(("arbitrary", "arbitrary")),
        name="inproj",
    )(x, mod3, w_p, b_p)


def _softmax_av(s_list, v_list):
    m = s_list[0].max(-1, keepdims=True)
    for s in s_list[1:]:
        m = jnp.maximum(m, s.max(-1, keepdims=True))
    acc = None
    l = None
    for s, v in zip(s_list, v_list):
        p = jnp.exp(s - m)
        ls = p.sum(-1, keepdims=True)
        o = _dot(p.astype(BF16), v)
        acc = o if acc is None else acc + o
        l = ls if l is None else l + ls
    return acc / l


NAT_GROUP = 8


def _natten_kernel(q_ref, k_ref, v_ref, qc_ref, kc_ref, vc_ref, bias_ref, o_ref, oc_ref, *, n_rows):
    kcb = kc_ref[0]
    vcb = vc_ref[0]
    lane = lax.broadcasted_iota(jnp.int32, (1, LANE), 1)
    head0 = lane < HEAD_DIM
    scale = jnp.asarray(HEAD_DIM ** -0.5, BF16)
    zero = jnp.zeros((), BF16)

    def stack_heads(q):
        return jnp.concatenate([jnp.where(head0, q, zero), jnp.where(head0, zero, q)], 0)

    def unstack(o, n):
        return jnp.where(head0, o[:n], o[n:])

    def rows_body(g, carry):
        koffs, scores = [], []
        for i in range(NAT_GROUP):
            r = g * NAT_GROUP + i
            rs = jnp.clip(r - WIN_ROWS // 2, 0, n_rows - WIN_ROWS)
            qs = stack_heads(q_ref[0, pl.ds(pl.multiple_of(r * GRID_W, GRID_W), GRID_W), :] * scale)
            koff = pl.multiple_of(rs * GRID_W, GRID_W)
            kl = k_ref[0, pl.ds(koff, WIN_ROWS * GRID_W), :]
            koffs.append(koff)
            scores.append([_dot_nt(qs, kl) + bias_ref[r - rs, 0], _dot_nt(qs, kcb)])
        for i in range(NAT_GROUP):
            r = g * NAT_GROUP + i
            vl = v_ref[0, pl.ds(koffs[i], WIN_ROWS * GRID_W), :]
            o = _softmax_av(scores[i], [vl, vcb])
            o_ref[0, pl.ds(pl.multiple_of(r * GRID_W, GRID_W), GRID_W), :] = unstack(o, GRID_W)
        return carry

    lax.fori_loop(0, n_rows // NAT_GROUP, rows_body, 0)

    tc = qc_ref.shape[1]
    oc = _softmax_av([_dot_nt(stack_heads(qc_ref[0] * scale), kcb)], [vcb])
    oc_ref[0] = unstack(oc, tc)


def _natten_call(pn_lat, pn_ctx, bias_l):
    B, T, _ = pn_lat.shape
    Tc = pn_ctx.shape[1]
    n_rows = T // GRID_W
    assert n_rows >= WIN_ROWS
    cb = lambda base: (lambda b, p: (b, 0, base // LANE + p))
    return pl.pallas_call(
        functools.partial(_natten_kernel, n_rows=n_rows),
        grid=(B, 4),
        in_specs=[pl.BlockSpec((1, T, LANE), cb(C_NQ)),
                  pl.BlockSpec((1, T, LANE), cb(C_NK)),
                  pl.BlockSpec((1, T, LANE), cb(C_NV)),
                  pl.BlockSpec((1, Tc, LANE), cb(C_NQ)),
                  pl.BlockSpec((1, Tc, LANE), cb(C_NK)),
                  pl.BlockSpec((1, Tc, LANE), cb(C_NV)),
                  pl.BlockSpec((WIN_ROWS, 1, 2 * GRID_W, WIN_ROWS * GRID_W), lambda b, p: (0, p, 0, 0))],
        out_specs=[pl.BlockSpec((1, T, LANE), lambda b, p: (b, 0, p)),
                   pl.BlockSpec((1, Tc, LANE), lambda b, p: (b, 0, p))],
        out_shape=[jax.ShapeDtypeStruct((B, T, 4 * LANE), F32),
                   jax.ShapeDtypeStruct((B, Tc, 4 * LANE), F32)],
        compiler_params=_cparams(("arbitrary", "arbitrary")),
        name="natten",
    )(pn_lat, pn_lat, pn_lat, pn_ctx, pn_ctx, pn_ctx, bias_l)


def _natten_bias_tables(rpb):
    col = np.arange(GRID_W)
    cstart = np.clip(col - WIN_COLS // 2, 0, GRID_W - WIN_COLS)
    col_ok = (col[None, :] >= cstart[:, None]) & (col[None, :] < cstart[:, None] + WIN_COLS)
    dc_idx = np.clip(col[None, :] - col[:, None] + WIN_COLS - 1, 0, 2 * WIN_COLS - 2)
    toe = rpb.astype(F32)[..., dc_idx]
    toe = jnp.where(col_ok[None, None, None], toe, NEG)
    t = jnp.stack([toe[:, :, WIN_ROWS - 1 - d:2 * WIN_ROWS - 1 - d] for d in range(WIN_ROWS)], 1)
    t = t.transpose(0, 1, 2, 4, 3, 5)
    return t.reshape(DEPTH, WIN_ROWS, 4, 2 * GRID_W, WIN_ROWS * GRID_W)


def _tri_masks():
    r = lax.broadcasted_iota(jnp.int32, (CHUNK, CHUNK), 0)
    c = lax.broadcasted_iota(jnp.int32, (CHUNK, CHUNK), 1)
    return r >= c, r <= c


def _rope(x, cs, sn, first, dist):
    w = x.shape[-1]
    partner = jnp.where(first, pltpu.roll(x, w - dist, 1), pltpu.roll(x, dist, 1))
    return x * cs + partner * sn


def _rope_tables(T, n_heads, half):
    t = jnp.arange(T)
    rows = (t // GRID_W).astype(F32)
    cols = (t % GRID_W).astype(F32)
    inv = ROPE_BASE ** (-jnp.arange(half, dtype=F32) / half)
    ar = rows[:, None] * inv[None, :]
    ac = cols[:, None] * inv[None, :]
    cos = jnp.concatenate([jnp.cos(ar), jnp.cos(ar), jnp.cos(ac), jnp.cos(ac)], -1)
    sin = jnp.concatenate([-jnp.sin(ar), jnp.sin(ar), -jnp.sin(ac), jnp.sin(ac)], -1)
    return jnp.tile(cos, (1, n_heads)), jnp.tile(sin, (1, n_heads))


def _bwd_chunk(c, nc, n):
    return jnp.where(c < nc, nc - 1 - c, n - 1 - c + nc)


def _mlstm_kernel(qkl_ref, vl_ref, sml_ref, qkc_ref, vc_ref, smc_ref, cos_ref, sin_ref, cw_ref,
                  ol_ref, oc_ref,
                  qt_s, k_s, vat_s, gcol_s, grow_s, brow_s, cm_s, ot_s, ck_s, *, nc, n):
    low, upp = _tri_masks()
    tri = jnp.where(low, 1.0, 0.0).astype(BF16)
    triu = jnp.where(upp, 1.0, 0.0).astype(BF16)
    lane128 = lax.broadcasted_iota(jnp.int32, (1, LANE), 1)
    lane256 = lax.broadcasted_iota(jnp.int32, (1, 2 * LANE), 1)
    rowi = lax.broadcasted_iota(jnp.int32, (CHUNK, 1), 0)
    row16 = lax.broadcasted_iota(jnp.int32, (16, 1), 0)
    first = (lane256 % 32) < 16
    hmask = [(lane256 >= HEAD_DIM * h) & (lane256 < HEAD_DIM * (h + 1)) for h in range(4)]
    ones_blk = jnp.where(lax.broadcasted_iota(jnp.int32, (HEAD_DIM, CHUNK), 0) == 0, 1.0, 0.0)
    fwd_rows = row16 < 8
    cw = cw_ref[...]

    def cummax_lanes(x, suffix):
        sh = 1
        while sh < CHUNK:
            if suffix:
                moved = jnp.where(lane256 < CHUNK - sh, pltpu.roll(x, CHUNK - sh, 1), NEG)
            else:
                moved = jnp.where(lane256 >= sh, pltpu.roll(x, sh, 1), NEG)
            x = jnp.maximum(x, moved)
            sh *= 2
        return x

    def prep(qk_ref, v_ref, sm_ref, ci, n_str, dst, use_rope):
        r0 = ci * CHUNK
        xc = qk_ref[0, r0:r0 + CHUNK, :]
        prev = qk_ref[0, r0 - 1:r0, :] if ci > 0 else jnp.zeros((1, 4 * LANE), F32)
        nxt = qk_ref[0, r0 + CHUNK:r0 + CHUNK + 1, :] if ci < n_str - 1 else jnp.zeros((1, 4 * LANE), F32)
        xp = jnp.where(rowi == 0, prev, pltpu.roll(xc, 1, 0))
        xn = jnp.where(rowi == CHUNK - 1, nxt, pltpu.roll(xc, CHUNK - 1, 0))
        y = jax.nn.silu(xp * cw[0:1] + xc * cw[1:2] + xn * cw[2:3])
        q = y[:, :2 * LANE]
        k = y[:, 2 * LANE:] * HEAD_DIM ** -0.5
        if use_rope:
            cs = cos_ref[r0:r0 + CHUNK, :]
            sn = sin_ref[r0:r0 + CHUNK, :]
            q = _rope(q, cs, sn, first, 16)
            k = _rope(k, cs, sn, first, 16)
        qt_s[dst] = q.T.astype(BF16)
        k_s[dst] = k.astype(BF16)
        vt = v_ref[0, r0:r0 + CHUNK, :].T
        for h in range(4):
            vat_s[h, dst] = jnp.concatenate([vt[HEAD_DIM * h:HEAD_DIM * (h + 1)], ones_blk], 0).astype(BF16)
        g = sm_ref[0, r0:r0 + CHUNK, :]
        lf = pltpu.roll(jax.nn.log_sigmoid(g), LANE - 4, 1)
        gcol_s[dst] = g - jnp.where(lane128 < 8, _dot_exact_l(tri, lf), _dot_exact_l(triu, lf))
        gt = g.T[0:16]
        lft = pltpu.roll(jax.nn.log_sigmoid(gt), 12, 0)
        brow = jnp.where(fwd_rows, _dot_exact_r(lft, triu), _dot_exact_r(lft, tri))
        brow_s[dst] = brow
        grow_s[dst] = gt - brow

    for ci in range(nc):
        prep(qkc_ref, vc_ref, smc_ref, ci, nc, ci, False)
    for ci in range(n - nc):
        prep(qkl_ref, vl_ref, sml_ref, ci, n - nc, nc + ci, True)
    g_all = grow_s[...].reshape(n * 16, CHUNK)
    fwd_all = (lax.broadcasted_iota(jnp.int32, (n * 16, 1), 0) % 16) < 8
    cm_s[...] = jnp.where(fwd_all, cummax_lanes(g_all, False), cummax_lanes(g_all, True)).reshape(n, 16, CHUNK)

    def scan_dir(bwd):
        di = 1 if bwd else 0
        gi = 8 if bwd else 0
        causal = low if bwd else upp

        def matmuls_first(c):
            ch = _bwd_chunk(c, nc, n) if bwd else c
            qt = qt_s[ch]
            kb = k_s[ch]
            ck_b = ck_s[di].astype(BF16)
            scores, inters = [], []
            for h in range(4):
                pr = slice(LANE * (h // 2), LANE * (h // 2 + 1))
                own = (lane128 < HEAD_DIM) if h % 2 == 0 else (lane128 >= HEAD_DIM)
                zero = jnp.zeros((), BF16)
                scores.append(_dot(jnp.where(own, kb[:, pr], zero), qt[pr, :]))
                inters.append(_dot(jnp.where(own, ck_b[:, pr], zero), qt[pr, :]))
            return ch, kb, scores, inters

        def rest(first, ms):
            ch, kb, scores, inters = first
            gcol_all = gcol_s[ch]
            new_ms = []
            for h in range(4):
                m = ms[h]
                g_row = grow_s[ch, gi + h:gi + h + 1, :]
                b_row = brow_s[ch, gi + h:gi + h + 1, :]
                a_row = jnp.maximum(m, cm_s[ch, gi + h:gi + h + 1, :])
                w = jnp.exp(jnp.where(causal, gcol_all[:, gi + h:gi + h + 1] - a_row, NEG))
                pt = (scores[h] * w).astype(BF16)
                vat = vat_s[h, ch]
                nd = _dot(vat, pt) + jnp.exp(m - a_row) * inters[h]
                den = nd[HEAD_DIM:HEAD_DIM + 1, :]
                ht = nd[0:HEAD_DIM] / jnp.maximum(jnp.abs(den), jnp.exp(-(b_row + a_row)))
                ot_s[di, ch, HEAD_DIM * h:HEAD_DIM * (h + 1), :] = ht
                bl = b_row[:, 0:1] if bwd else b_row[:, CHUNK - 1:CHUNK]
                lw_end = bl + g_row
                m_new = jnp.maximum(bl + m, lw_end.max(-1, keepdims=True))
                upd = _dot((vat * jnp.exp(lw_end - m_new)).astype(BF16), kb)
                ck_s[di] = jnp.where(hmask[h], jnp.exp(bl + m - m_new) * ck_s[di] + upd, ck_s[di])
                new_ms.append(m_new)
            return tuple(new_ms)

        return matmuls_first, rest

    ck_s[...] = jnp.zeros_like(ck_s)
    first_f, rest_f = scan_dir(False)
    first_b, rest_b = scan_dir(True)

    def chunk_body(c, carry):
        ff, fb = first_f(c), first_b(c)
        return rest_f(ff, carry[0]), rest_b(fb, carry[1])

    zeros4 = tuple(jnp.zeros((1, 1), F32) for _ in range(4))
    lax.fori_loop(0, n, chunk_body, (zeros4, zeros4))
    for ci in range(nc):
        oc_ref[0, ci * CHUNK:(ci + 1) * CHUNK, :] = (ot_s[0, ci] + ot_s[1, ci]).T
    for ci in range(n - nc):
        ol_ref[0, ci * CHUNK:(ci + 1) * CHUNK, :] = (ot_s[0, nc + ci] + ot_s[1, nc + ci]).T


def _mlstm_call(p_lat, p_ctx, cos, sin, conv_w_l):
    B, T, _ = p_lat.shape
    Tc = p_ctx.shape[1]
    nc, n = Tc // CHUNK, (Tc + T) // CHUNK
    cb = lambda base, w: (lambda b: (b, 0, base // w))
    return pl.pallas_call(
        functools.partial(_mlstm_kernel, nc=nc, n=n),
        grid=(B,),
        in_specs=[pl.BlockSpec((1, T, 512), cb(C_MQK, 512)),
                  pl.BlockSpec((1, T, 256), cb(C_MV, 256)),
                  pl.BlockSpec((1, T, LANE), cb(C_SM, LANE)),
                  pl.BlockSpec((1, Tc, 512), cb(C_MQK, 512)),
                  pl.BlockSpec((1, Tc, 256), cb(C_MV, 256)),
                  pl.BlockSpec((1, Tc, LANE), cb(C_SM, LANE)),
                  pl.BlockSpec((T, 256), lambda b: (0, 0)),
                  pl.BlockSpec((T, 256), lambda b: (0, 0)),
                  pl.BlockSpec((CONV_K, 512), lambda b: (0, 0))],
        out_specs=[pl.BlockSpec((1, T, 256), lambda b: (b, 0, 0)),
                   pl.BlockSpec((1, Tc, 256), lambda b: (b, 0, 0))],
        out_shape=[jax.ShapeDtypeStruct((B, T, 256), F32),
                   jax.ShapeDtypeStruct((B, Tc, 256), F32)],
        scratch_shapes=[pltpu.VMEM((n, 256, CHUNK), BF16),
                        pltpu.VMEM((n, CHUNK, 256), BF16),
                        pltpu.VMEM((4, n, LANE, CHUNK), BF16),
                        pltpu.VMEM((n, CHUNK, LANE), F32),
                        pltpu.VMEM((n, 16, CHUNK), F32),
                        pltpu.VMEM((n, 16, CHUNK), F32),
                        pltpu.VMEM((n, 16, CHUNK), F32),
                        pltpu.VMEM((2, n, 256, CHUNK), F32),
                        pltpu.VMEM((2, LANE, 256), F32)],
        compiler_params=_cparams(("arbitrary",)),
        name="mlstm",
    )(p_lat, p_lat, p_lat, p_ctx, p_ctx, p_ctx, cos, sin, conv_w_l)


GLA_BLK = 64
GLA_CLAMP = 80.0


def _gla_kernel(ql_ref, kl_ref, vl_ref, sml_ref, qc_ref, kc_ref, vc_ref, smc_ref, cos_ref, sin_ref, w2_ref, b2_ref,
                ol_ref, oc_ref,
                q_s, k_s, v_s, a_s, b_s, o_s, st_s, *, nc, n):
    low, upp = _tri_masks()
    tri = jnp.where(low, 1.0, 0.0).astype(BF16)
    triu = jnp.where(upp, 1.0, 0.0).astype(BF16)
    lane128 = lax.broadcasted_iota(jnp.int32, (1, LANE), 1)
    lane256 = lax.broadcasted_iota(jnp.int32, (1, 2 * LANE), 1)
    first = (lane128 % 16) < 8
    hm128 = [(lane128 >= 32 * h) & (lane128 < 32 * (h + 1)) for h in range(4)]
    hm256 = [(lane256 >= 64 * h) & (lane256 < 64 * (h + 1)) for h in range(4)]
    nb = CHUNK // GLA_BLK
    r_st = lax.broadcasted_iota(jnp.int32, (nb * GLA_BLK, CHUNK), 0)
    c_st = lax.broadcasted_iota(jnp.int32, (nb * GLA_BLK, CHUNK), 1)
    bd_r = lax.broadcasted_iota(jnp.int32, (LANE, 2 * LANE), 0)
    bd_c = lax.broadcasted_iota(jnp.int32, (LANE, 2 * LANE), 1)
    blockdiag = (bd_r // 32) == (bd_c // 64)

    def prep(q_ref, k_ref, v_ref, sm_ref, ci, dst, use_rope):
        r0 = ci * CHUNK
        q = q_ref[0, r0:r0 + CHUNK, :] * 32 ** -0.5
        k = k_ref[0, r0:r0 + CHUNK, :]
        if use_rope:
            cs = cos_ref[r0:r0 + CHUNK, :]
            sn = sin_ref[r0:r0 + CHUNK, :]
            q = _rope(q, cs, sn, first, 8)
            k = _rope(k, cs, sn, first, 8)
        q_s[dst] = q
        k_s[dst] = k
        v_s[dst] = v_ref[0, r0:r0 + CHUNK, :].astype(BF16)
        lr = sm_ref[0, r0:r0 + CHUNK, :].astype(BF16)
        for d in range(2):
            a = jax.nn.log_sigmoid(_dot(lr, w2_ref[d]) + b2_ref[d]) / GLA_TAU
            a_s[d, dst] = a
            b_s[d, dst] = _dot_exact_l(triu if d else tri, a)

    for ci in range(nc):
        prep(qc_ref, kc_ref, vc_ref, smc_ref, ci, ci, False)
    for ci in range(n - nc):
        prep(ql_ref, kl_ref, vl_ref, sml_ref, ci, nc + ci, True)

    def scan_dir(bwd):
        d = 1 if bwd else 0
        st_s[...] = jnp.zeros_like(st_s)

        def chunk_body(c, carry):
            ch = _bwd_chunk(c, nc, n) if bwd else c
            q = q_s[ch]
            k = k_s[ch]
            vb = v_s[ch]
            a = a_s[d, ch]
            b = b_s[d, ch]
            st_b = st_s[...].astype(BF16)
            o_inter = _dot((q * jnp.exp(b)).astype(BF16), st_b)
            atts = []
            for i in range(nb):
                rows = slice(i * GLA_BLK, (i + 1) * GLA_BLK)
                e = (i + 1) * GLA_BLK - 1 if bwd else i * GLA_BLK
                ref = b[e:e + 1, :] - a[e:e + 1, :]
                qs = q[rows] * jnp.exp(b[rows] - ref)
                ks = (k * jnp.exp(jnp.minimum(ref - b, GLA_CLAMP))).astype(BF16)
                lhs = jnp.concatenate([jnp.where(hm128[h], qs, 0.0) for h in range(4)], 0).astype(BF16)
                att = _dot_nt(lhs, ks)
                t_idx = (r_st % GLA_BLK) + i * GLA_BLK
                ok = (c_st >= t_idx) if bwd else (c_st <= t_idx)
                atts.append(jnp.where(ok, att, 0.0).astype(BF16))
            for i in range(nb):
                rows = slice(i * GLA_BLK, (i + 1) * GLA_BLK)
                oh = _dot(atts[i], vb)
                o_blk = o_inter[rows]
                for h in range(4):
                    o_blk = o_blk + jnp.where(hm256[h], oh[h * GLA_BLK:(h + 1) * GLA_BLK], 0.0)
                if bwd:
                    o_s[ch, rows, :] = o_s[ch, rows, :] + o_blk
                else:
                    o_s[ch, rows, :] = o_blk
            bt = b.T
            tot = bt[:, 0:1] if bwd else bt[:, CHUNK - 1:CHUNK]
            kt = (k.T * jnp.exp(tot - bt)).astype(BF16)
            upd = jnp.where(blockdiag, _dot(kt, vb), 0.0)
            st_s[...] = jnp.exp(tot) * st_s[...] + upd
            return carry

        lax.fori_loop(0, n, chunk_body, 0)

    scan_dir(False)
    scan_dir(True)
    for ci in range(nc):
        oc_ref[0, ci * CHUNK:(ci + 1) * CHUNK, :] = o_s[ci]
    for ci in range(n - nc):
        ol_ref[0, ci * CHUNK:(ci + 1) * CHUNK, :] = o_s[nc + ci]


def _gla_call(p_lat, p_ctx, cos, sin, w2p, b2p):
    B, T, _ = p_lat.shape
    Tc = p_ctx.shape[1]
    nc, n = Tc // CHUNK, (Tc + T) // CHUNK
    cb = lambda base, w: (lambda b: (b, 0, base // w))
    return pl.pallas_call(
        functools.partial(_gla_kernel, nc=nc, n=n),
        grid=(B,),
        in_specs=[pl.BlockSpec((1, T, LANE), cb(C_GQ, LANE)),
                  pl.BlockSpec((1, T, LANE), cb(C_GK, LANE)),
                  pl.BlockSpec((1, T, 256), cb(C_GV, 256)),
                  pl.BlockSpec((1, T, LANE), cb(C_SM, LANE)),
                  pl.BlockSpec((1, Tc, LANE), cb(C_GQ, LANE)),
                  pl.BlockSpec((1, Tc, LANE), cb(C_GK, LANE)),
                  pl.BlockSpec((1, Tc, 256), cb(C_GV, 256)),
                  pl.BlockSpec((1, Tc, LANE), cb(C_SM, LANE)),
                  pl.BlockSpec((T, LANE), lambda b: (0, 0)),
                  pl.BlockSpec((T, LANE), lambda b: (0, 0)),
                  pl.BlockSpec((2, LANE, LANE), lambda b: (0, 0, 0)),
                  pl.BlockSpec((2, 1, LANE), lambda b: (0, 0, 0))],
        out_specs=[pl.BlockSpec((1, T, 256), lambda b: (b, 0, 0)),
                   pl.BlockSpec((1, Tc, 256), lambda b: (b, 0, 0))],
        out_shape=[jax.ShapeDtypeStruct((B, T, 256), F32),
                   jax.ShapeDtypeStruct((B, Tc, 256), F32)],
        scratch_shapes=[pltpu.VMEM((n, CHUNK, LANE), F32),
                        pltpu.VMEM((n, CHUNK, LANE), F32),
                        pltpu.VMEM((n, CHUNK, 256), BF16),
                        pltpu.VMEM((2, n, CHUNK, LANE), F32),
                        pltpu.VMEM((2, n, CHUNK, LANE), F32),
                        pltpu.VMEM((n, CHUNK, 256), F32),
                        pltpu.VMEM((LANE, 256), F32)],
        compiler_params=_cparams(("arbitrary",)),
        name="gla",
    )(p_lat, p_lat, p_lat, p_lat, p_ctx, p_ctx, p_ctx, p_ctx, cos, sin, w2p, b2p)


def _gla_gate_weights(gla_w2, gla_b2):
    w = jnp.zeros((DEPTH, 2, LANE, LANE), F32)
    w = w.at[:, 0, 16:32].set(gla_w2[:, 0]).at[:, 1, 32:48].set(gla_w2[:, 1])
    return w.astype(BF16), gla_b2.astype(F32).reshape(DEPTH, 2, 1, LANE)


TILE_ROWS = D // LANE


def _store_token_tiles(ref, val):
    tm = val.shape[0]
    for s in range(TILE_ROWS):
        ref[0, pl.ds(s, tm, stride=TILE_ROWS), :] = val[:, s * LANE:(s + 1) * LANE]


def _load_token_tiles(ref, lead, t0, tm):
    return [ref[lead + (pl.ds(t0 * TILE_ROWS + s, tm, stride=TILE_ROWS), slice(None))] for s in range(TILE_ROWS)]


def _layer_norm(z, g, b):
    mu = z.mean(-1, keepdims=True)
    zc = z - mu
    var = jnp.mean(jnp.square(zc), -1, keepdims=True)
    return zc * lax.rsqrt(var + LN_EPS) * g + b


def _merge_kernel(hm_ref, hg_ref, hn_ref, mo_ref, gr_ref, x_ref, mod_ref, wout_ref, wr_ref, ng_ref, ln_ref,
                  x1_ref, lg_ref):
    r = lax.broadcasted_iota(jnp.int32, (256, 256), 0)
    c = lax.broadcasted_iota(jnp.int32, (256, 256), 1)
    avg = jnp.where((r // HEAD_DIM) == (c // HEAD_DIM), 1.0 / HEAD_DIM, 0.0).astype(BF16)

    def seg_mean(x):
        hi = x.astype(BF16)
        lo = (x - hi.astype(F32)).astype(BF16)
        return _dot(hi, avg) + _dot(lo, avg)

    def head_norm(h):
        d = h - seg_mean(h)
        return d * lax.rsqrt(seg_mean(d * d) + LN_EPS)

    ym = head_norm(hm_ref[0]) * ng_ref[0:1, :] * jax.nn.sigmoid(mo_ref[0])
    yg = head_norm(hg_ref[0]) * ng_ref[1:2, :] * jax.nn.silu(gr_ref[0])
    y = (_dot(ym.astype(BF16), wout_ref[0:256, :]) + _dot(yg.astype(BF16), wout_ref[256:512, :])
         + _dot(hn_ref[0].astype(BF16), wout_ref[512:1024, :]))
    m = mod_ref[0]
    x1 = _layer_norm(DEEPNORM_ALPHA * x_ref[0] + m[:, 2 * D:3 * D] * y, ln_ref[0:1, :], ln_ref[1:2, :])
    _store_token_tiles(x1_ref, x1)
    u2 = x1 * (1.0 + m[:, 4 * D:5 * D]) + m[:, 3 * D:4 * D]
    lg_ref[0] = _dot_nt(wr_ref[...], u2.astype(BF16))


def _merge_call(hm, hg, hn, p, x, mod3, wout_b, wrt_b, ng, ln, shared_row):
    B, T, _ = x.shape
    tm = min(T, 512)
    mod_map = (lambda b, i: (b, 0, 0)) if shared_row is None else (lambda b, i: (shared_row, 0, 0))
    tok = lambda w: pl.BlockSpec((1, tm, w), lambda b, i: (b, i, 0))
    return pl.pallas_call(
        _merge_kernel,
        grid=(B, T // tm),
        in_specs=[tok(256), tok(256), tok(512),
                  pl.BlockSpec((1, tm, 256), lambda b, i: (b, i, C_MO // 256)),
                  pl.BlockSpec((1, tm, 256), lambda b, i: (b, i, C_GR // 256)),
                  tok(D),
                  pl.BlockSpec((1, 1, 6 * D), mod_map),
                  pl.BlockSpec((D, D), lambda b, i: (0, 0)),
                  pl.BlockSpec((N_EXPERTS, D), lambda b, i: (0, 0)),
                  pl.BlockSpec((2, 256), lambda b, i: (0, 0)),
                  pl.BlockSpec((2, D), lambda b, i: (0, 0))],
        out_specs=[pl.BlockSpec((1, tm * TILE_ROWS, LANE), lambda b, i: (b, i, 0)),
                   pl.BlockSpec((1, N_EXPERTS, tm), lambda b, i: (b, 0, i))],
        out_shape=[jax.ShapeDtypeStruct((B, T * TILE_ROWS, LANE), F32),
                   jax.ShapeDtypeStruct((B, N_EXPERTS, T), F32)],
        compiler_params=_cparams(("arbitrary", "arbitrary")),
        name="merge",
    )(hm, hg, hn, p, p, x, mod3, wout_b, wrt_b, ng, ln)


ROUTER_SAMPLES = 8


def _router_samples(B):
    ns = ROUTER_SAMPLES
    while B % ns:
        ns -= 1
    return ns


def _router_kernel(lg_ref, o_ref, aff_s, sp_s, *, T, cap):
    J = T // LANE
    E = N_EXPERTS
    NS = lg_ref.shape[0]
    keys = [[] for _ in range(NS)]
    for j in range(J):
        for si in range(NS):
            lg = lg_ref[si, :, j * LANE:(j + 1) * LANE]
            ex = jnp.exp(lg - lg.max(0, keepdims=True))
            aff = ex / ex.sum(0, keepdims=True)
            aff_s[si, j * E:(j + 1) * E, :] = aff
            keys[si].append(pltpu.bitcast(aff, jnp.int32))

    def count(ks, pred):
        cnt = None
        for k in ks:
            cj = jnp.where(pred(k), 1.0, 0.0)
            cnt = cj if cnt is None else cnt + cj
        return cnt.sum(-1, keepdims=True)

    thrs = [jnp.zeros((E, 1), jnp.int32) for _ in range(NS)]
    for bit in range(30, -1, -1):
        for si in range(NS):
            cand = thrs[si] | (1 << bit)
            thrs[si] = jnp.where(count(keys[si], lambda k: k >= cand) >= cap, cand, thrs[si])

    r = lax.broadcasted_iota(jnp.int32, (LANE, LANE), 0)
    c = lax.broadcasted_iota(jnp.int32, (LANE, LANE), 1)
    upper = jnp.where(r <= c, 1.0, 0.0).astype(BF16)
    ones = jnp.ones((LANE, LANE), BF16)
    rr = lax.broadcasted_iota(jnp.int32, (J * E, J * E), 0)
    cc = lax.broadcasted_iota(jnp.int32, (J * E, J * E), 1)
    earlier = jnp.where(((rr % E) == (cc % E)) & ((cc // E) < (rr // E)), 1.0, 0.0).astype(BF16)

    def prefix(x01):
        xb = x01.astype(BF16)
        return _dot(xb, upper) + _dot(earlier, _dot(xb, ones).astype(BF16))

    for si in range(NS):
        thr = thrs[si]
        need = cap - count(keys[si], lambda k: k > thr)
        gt = jnp.concatenate([jnp.where(k > thr, 1.0, 0.0) for k in keys[si]], 0)
        eq = jnp.concatenate([jnp.where(k == thr, 1.0, 0.0) for k in keys[si]], 0)
        need_t = jnp.concatenate([need] * J, 0)
        sel = jnp.maximum(gt, jnp.where(prefix(eq) <= need_t, eq, 0.0))
        sp_s[si] = jnp.where(sel > 0.0, prefix(sel) - 1.0, -1.0)

    sb = min(cap, LANE)
    lane = lax.broadcasted_iota(jnp.int32, (1, LANE), 1)
    o_ref[...] = jnp.zeros_like(o_ref)
    for e in range(E):
        for half in range(cap // sb):
            slot = (lax.broadcasted_iota(jnp.int32, (sb, LANE), 0) + half * sb).astype(F32)
            rows = slice(half * sb, (half + 1) * sb)
            for si in range(NS):
                def jbody(j, acc):
                    acc_i, acc_g = acc
                    sp = sp_s[si, pl.ds(j * E + e, 1), :]
                    af = aff_s[si, pl.ds(j * E + e, 1), :]
                    hit = sp == slot
                    tid = (lane + j * LANE).astype(F32)
                    return jnp.where(hit, tid, acc_i), jnp.where(hit, af, acc_g)

                acc_i, acc_g = lax.fori_loop(0, J, jbody, (jnp.zeros((sb, LANE), F32), jnp.zeros((sb, LANE), F32)),
                                             unroll=2)
                icol = acc_i.sum(-1, keepdims=True)
                gcol = acc_g.sum(-1, keepdims=True)
                o_ref[si, rows, :] = jnp.where(lane == e, icol, jnp.where(lane == E + e, gcol, o_ref[si, rows, :]))


def _router_call(logits):
    B, _, T = logits.shape
    cap = CAPACITY_FACTOR * T // N_EXPERTS
    J = T // LANE
    ns = _router_samples(B)
    out = pl.pallas_call(
        functools.partial(_router_kernel, T=T, cap=cap),
        grid=(B // ns,),
        in_specs=[pl.BlockSpec((ns, N_EXPERTS, T), lambda b: (b, 0, 0))],
        out_specs=pl.BlockSpec((ns, cap, LANE), lambda b: (b, 0, 0)),
        out_shape=jax.ShapeDtypeStruct((B, cap, LANE), F32),
        scratch_shapes=[pltpu.VMEM((ns, J * N_EXPERTS, LANE), F32), pltpu.VMEM((ns, J * N_EXPERTS, LANE), F32)],
        compiler_params=_cparams(("arbitrary",)),
        name="router",
    )(logits)
    idx = (out[:, :, :N_EXPERTS].astype(jnp.int32) * TILE_ROWS).transpose(0, 2, 1).reshape(B, 1, N_EXPERTS * cap)
    gate = out[:, :, N_EXPERTS:2 * N_EXPERTS].transpose(0, 2, 1).reshape(B, 1, N_EXPERTS * cap)
    return idx, gate


GATHER_BLOCK_BYTES = 4 * 1024 * 1024
SCATTER_BLOCK_BYTES = 4 * 1024 * 1024


def _experts_per_step(cap, block_bytes):
    eg = 2
    while eg < N_EXPERTS and 2 * eg * cap * D * 4 <= block_bytes:
        eg *= 2
    return eg


def _as_token_tile(v):
    return jnp.concatenate([v[:, s * LANE:(s + 1) * LANE] for s in range(TILE_ROWS)], 0)


def _gather_kernel(idx_ref, x_ref, mod_ref, o_ref, *, cap, eg):
    g = pl.program_id(1)
    m = mod_ref[0]
    scale = _as_token_tile(1.0 + m[:, 4 * D:5 * D])
    shift = _as_token_tile(m[:, 3 * D:4 * D])
    half = cap // 2

    def row_bits(r):
        src = pl.ds(pl.multiple_of(r, TILE_ROWS), TILE_ROWS)
        u = (x_ref[0, src, :] * scale + shift).astype(BF16).astype(F32)
        return pltpu.bitcast(u, jnp.uint32)

    for e in range(eg):
        def body(s, carry):
            base = (g * eg + e) * cap + s
            lo = row_bits(idx_ref[0, 0, base]) >> 16
            hi = row_bits(idx_ref[0, 0, base + half]) & jnp.uint32(0xFFFF0000)
            o_ref[e, 0, pl.ds(pl.multiple_of(s * TILE_ROWS, TILE_ROWS), TILE_ROWS), :] = lo | hi
            return carry
        lax.fori_loop(0, half, body, 0, unroll=4)


def _gather_call(idx, x1t, mod3, shared_row):
    B = x1t.shape[0]
    T = x1t.shape[1] // TILE_ROWS
    cap = CAPACITY_FACTOR * T // N_EXPERTS
    eg = _experts_per_step(cap, GATHER_BLOCK_BYTES)
    mod_map = (lambda b, g: (b, 0, 0)) if shared_row is None else (lambda b, g: (shared_row, 0, 0))
    smem = lambda: pl.BlockSpec((1, 1, N_EXPERTS * cap), lambda b, g: (b, 0, 0), memory_space=pltpu.SMEM)
    return pl.pallas_call(
        functools.partial(_gather_kernel, cap=cap, eg=eg),
        grid=(B, N_EXPERTS // eg),
        in_specs=[smem(), pl.BlockSpec((1, T * TILE_ROWS, LANE), lambda b, g: (b, 0, 0)),
                  pl.BlockSpec((1, 1, 6 * D), mod_map)],
        out_specs=pl.BlockSpec((eg, 1, cap // 2 * TILE_ROWS, LANE), lambda b, g: (g, b, 0, 0)),
        out_shape=jax.ShapeDtypeStruct((N_EXPERTS, B, cap // 2 * TILE_ROWS, LANE), jnp.uint32),
        compiler_params=_cparams(("arbitrary", "arbitrary")),
        name="gather",
    )(idx, x1t, mod3)


FF_CHUNK = 512


def _ffn_kernel(x_ref, wg_ref, wu_ref, wd_ref, o_ref, xs, *, tm, half):
    for s, words in enumerate(_load_token_tiles(x_ref, (0,), 0, tm // 2)):
        lo = pltpu.bitcast(words << 16, F32).astype(BF16)
        hi = pltpu.bitcast(words & jnp.uint32(0xFFFF0000), F32).astype(BF16)
        for blk in range(tm // (2 * half)):
            src = slice(blk * half, (blk + 1) * half)
            xs[2 * blk * half:(2 * blk + 1) * half, s * LANE:(s + 1) * LANE] = lo[src]
            xs[(2 * blk + 1) * half:(2 * blk + 2) * half, s * LANE:(s + 1) * LANE] = hi[src]
    xb = xs[...]
    acc = None
    for c in range(EXPERT_FF // FF_CHUNK):
        cols = slice(c * FF_CHUNK, (c + 1) * FF_CHUNK)
        h = (jax.nn.silu(_dot(xb, wg_ref[0, :, cols])) * _dot(xb, wu_ref[0, :, cols])).astype(BF16)
        t = _dot(h, wd_ref[0, cols, :])
        acc = t if acc is None else acc + t
    _store_token_tiles(o_ref, acc)


def _ffn_call(xe, wg_b, wu_b, wd_b, layer, cap):
    E = xe.shape[0]
    M = 2 * xe.shape[1] // TILE_ROWS
    tm = min(M, 512)
    assert tm % cap == 0
    w_map = lambda e, i: (layer * E + e, 0, 0)
    return pl.pallas_call(
        functools.partial(_ffn_kernel, tm=tm, half=cap // 2),
        grid=(E, M // tm),
        in_specs=[pl.BlockSpec((1, tm // 2 * TILE_ROWS, LANE), lambda e, i: (e, i, 0)),
                  pl.BlockSpec((1, D, EXPERT_FF), w_map),
                  pl.BlockSpec((1, D, EXPERT_FF), w_map),
                  pl.BlockSpec((1, EXPERT_FF, D), w_map)],
        out_specs=pl.BlockSpec((1, tm * TILE_ROWS, LANE), lambda e, i: (e, i, 0)),
        out_shape=jax.ShapeDtypeStruct((E, M * TILE_ROWS, LANE), F32),
        scratch_shapes=[pltpu.VMEM((tm, D), BF16)],
        compiler_params=_cparams(("arbitrary", "arbitrary")),
        name="ffn",
    )(xe, wg_b, wu_b, wd_b)


SCATTER_FIN_ROWS = 256
SCATTER_BATCH = 4


def _scatter_kernel(idx_ref, gate_ref, y_ref, x1_ref, mod_ref, ln_ref, o_ref, acc_a, acc_b, *, cap, eg, n_g, T):
    g = pl.program_id(1)

    @pl.when(g == 0)
    def _():
        acc_a[...] = jnp.zeros_like(acc_a)
        acc_b[...] = jnp.zeros_like(acc_b)

    def add_rows(acc, e_local, s0):
        p0 = (g * eg + e_local) * cap + s0
        dst = [pl.ds(pl.multiple_of(idx_ref[0, 0, p0 + k], TILE_ROWS), TILE_ROWS) for k in range(SCATTER_BATCH)]
        old = [acc[d, :] for d in dst]
        for k in range(SCATTER_BATCH):
            src = pl.ds(pl.multiple_of((s0 + k) * TILE_ROWS, TILE_ROWS), TILE_ROWS)
            acc[dst[k], :] = old[k] + y_ref[e_local, 0, src, :] * gate_ref[0, 0, p0 + k]

    for pair in range(eg // 2):
        def body(i, carry):
            add_rows(acc_a, 2 * pair, i * SCATTER_BATCH)
            add_rows(acc_b, 2 * pair + 1, i * SCATTER_BATCH)
            return carry
        lax.fori_loop(0, cap // SCATTER_BATCH, body, 0, unroll=2)

    @pl.when(g == n_g - 1)
    def _():
        g2 = mod_ref[0][:, 5 * D:6 * D]
        tb = min(T, SCATTER_FIN_ROWS)
        for i in range(T // tb):
            rows = slice(i * tb, (i + 1) * tb)
            f = jnp.concatenate([pa + pb for pa, pb in zip(_load_token_tiles(acc_a, (), i * tb, tb),
                                                             _load_token_tiles(acc_b, (), i * tb, tb))], -1)
            x1 = jnp.concatenate(_load_token_tiles(x1_ref, (0,), i * tb, tb), -1)
            z = DEEPNORM_ALPHA * x1 + g2 * f
            o_ref[0, rows, :] = _layer_norm(z, ln_ref[0:1, :], ln_ref[1:2, :])


def _scatter_call(idx, gate, y4, x1t, mod3, ln, shared_row):
    B = x1t.shape[0]
    T = x1t.shape[1] // TILE_ROWS
    cap = CAPACITY_FACTOR * T // N_EXPERTS
    eg = _experts_per_step(cap, SCATTER_BLOCK_BYTES)
    n_g = N_EXPERTS // eg
    mod_map = (lambda b, g: (b, 0, 0)) if shared_row is None else (lambda b, g: (shared_row, 0, 0))
    smem = lambda: pl.BlockSpec((1, 1, N_EXPERTS * cap), lambda b, g: (b, 0, 0), memory_space=pltpu.SMEM)
    return pl.pallas_call(
        functools.partial(_scatter_kernel, cap=cap, eg=eg, n_g=n_g, T=T),
        grid=(B, n_g),
        scratch_shapes=[pltpu.VMEM((T * TILE_ROWS, LANE), F32), pltpu.VMEM((T * TILE_ROWS, LANE), F32)],
        in_specs=[smem(), smem(),
                  pl.BlockSpec((eg, 1, cap * TILE_ROWS, LANE), lambda b, g: (g, b, 0, 0)),
                  pl.BlockSpec((1, T * TILE_ROWS, LANE), lambda b, g: (b, 0, 0), pipeline_mode=pl.Buffered(1)),
                  pl.BlockSpec((1, 1, 6 * D), mod_map),
                  pl.BlockSpec((2, D), lambda b, g: (0, 0))],
        out_specs=pl.BlockSpec((1, T, D), lambda b, g: (b, 0, 0)),
        out_shape=jax.ShapeDtypeStruct((B, T, D), F32),
        compiler_params=_cparams(("arbitrary", "arbitrary")),
        name="scatter",
    )(idx, gate, y4, x1t, mod3, ln)


def _moe(x1t, logits, mod3, wg_b, wu_b, wd_b, layer, ln2, shared_row):
    B = x1t.shape[0]
    T = x1t.shape[1] // TILE_ROWS
    cap = CAPACITY_FACTOR * T // N_EXPERTS
    idx, gate = _router_call(logits)
    xe = _gather_call(idx, x1t, mod3, shared_row)
    y = _ffn_call(xe.reshape(N_EXPERTS, B * (cap // 2) * TILE_ROWS, LANE), wg_b, wu_b, wd_b, layer, cap)
    return _scatter_call(idx, gate, y.reshape(N_EXPERTS, B, cap * TILE_ROWS, LANE), x1t, mod3, ln2, shared_row)


def kernel(x, c, ctx, c_ctx, w_mod, b_mod, w_in, b_in, conv_w, gla_w2, gla_b2, mlstm_norm_g, gla_norm_g, rpb, w_out, ln1_g, ln1_b, w_router, w_gate, w_up, w_down, ln2_g, ln2_b):
    B, T, _ = x.shape
    n_mod = -(-(B + 1) // 8) * 8
    c_all = jnp.concatenate([c, c_ctx[None], jnp.zeros((n_mod - B - 1, D), F32)], 0)
    mods = _mod_call(c_all, w_mod, b_mod)
    w_p = _repack_columns(w_in).astype(BF16)
    b_p = _repack_columns(b_in)
    bias = _natten_bias_tables(rpb)
    mcos, msin = _rope_tables(T, 4, 16)
    gcos, gsin = _rope_tables(T, 4, 8)
    w2p, b2p = _gla_gate_weights(gla_w2, gla_b2)
    wout_b = w_out.astype(BF16)
    wrt_b = jnp.swapaxes(w_router, 1, 2).astype(BF16)
    wg_b = w_gate.astype(BF16).reshape(DEPTH * N_EXPERTS, D, EXPERT_FF)
    wu_b = w_up.astype(BF16).reshape(DEPTH * N_EXPERTS, D, EXPERT_FF)
    wd_b = w_down.astype(BF16).reshape(DEPTH * N_EXPERTS, EXPERT_FF, D)
    ng = jnp.stack([mlstm_norm_g, gla_norm_g], 1)
    ln1 = jnp.stack([ln1_g, ln1_b], 1)
    ln2 = jnp.stack([ln2_g, ln2_b], 1)
    for l in range(DEPTH):
        mod3 = mods[l].reshape(n_mod, 1, 6 * D)
        p_lat, pn_lat = _inproj_call(x, mod3, w_p[l], b_p[l][None], None)
        p_ctx, pn_ctx = _inproj_call(ctx, mod3, w_p[l], b_p[l][None], B)
        n_lat, n_ctx = _natten_call(pn_lat, pn_ctx, bias[l])
        m_lat, m_ctx = _mlstm_call(p_lat, p_ctx, mcos, msin, conv_w[l])
        g_lat, g_ctx = _gla_call(p_lat, p_ctx, gcos, gsin, w2p[l], b2p[l])
        x1t, lg = _merge_call(m_lat, g_lat, n_lat, p_lat, x, mod3, wout_b[l], wrt_b[l], ng[l], ln1[l], None)
        x = _moe(x1t, lg, mod3, wg_b, wu_b, wd_b, l, ln2[l], None)
        if l < DEPTH - 1:
            c1t, lgc = _merge_call(m_ctx, g_ctx, n_ctx, p_ctx, ctx, mod3, wout_b[l], wrt_b[l], ng[l], ln1[l], B)
            ctx = _moe(c1t, lgc, mod3, wg_b, wu_b, wd_b, l, ln2[l], B)
    return x
```

```python
import functools

import numpy as np
import jax
import jax.numpy as jnp
from jax import lax
from jax.experimental import pallas as pl
from jax.experimental.pallas import tpu as pltpu

F32 = jnp.float32
BF16 = jnp.bfloat16

D = 1024
DEPTH = 4
GRID_W = 64
HEAD_DIM = 64
WIN_ROWS = 8
WIN_COLS = 16
CONV_K = 3
ROPE_BASE = 10000.0
N_EXPERTS = 16
EXPERT_FF = 2 * D
CAPACITY_FACTOR = 2
LN_EPS = 1e-5
GLA_TAU = 16.0
DEEPNORM_ALPHA = (2 * DEPTH) ** 0.25
NEG = -1e30

VMEM_LIMIT = 56 * 1024 * 1024
LANE = 128
CHUNK = 256

C_MQK, C_MV, C_MO, C_GV, C_GR = 0, 512, 768, 1024, 1280
C_GQ, C_GK, C_SM = 1536, 1664, 1792
PM = 1920
C_NQ, C_NK, C_NV = 0, 512, 1024
PN = 1536
PW = PM + PN
_MAIN_SEGMENTS = ((0, 512), (512, 768), (768, 1024), (1296, 1552), (1552, 1808), (1040, 1168), (1168, 1296),
                  (1024, 1040), (1808, 1840))
_NAT_SEGMENTS = ((1840, 2352), (2352, 2864), (2864, 3376))
_NPAD = PM - sum(b - a for a, b in _MAIN_SEGMENTS)


def _repack_columns(w):
    parts = ([w[..., a:b] for a, b in _MAIN_SEGMENTS] + [jnp.zeros(w.shape[:-1] + (_NPAD,), w.dtype)]
             + [w[..., a:b] for a, b in _NAT_SEGMENTS])
    return jnp.concatenate(parts, -1)


def _cparams(sem):
    return pltpu.CompilerParams(dimension_semantics=sem, vmem_limit_bytes=VMEM_LIMIT)


def _dot(a, b):
    return jnp.dot(a, b, preferred_element_type=F32)


def _dot_nt(a, b):
    return lax.dot_general(a, b, (((1,), (1,)), ((), ())), preferred_element_type=F32)


def _split3(x):
    hi = x.astype(BF16)
    r1 = x - hi.astype(F32)
    mid = r1.astype(BF16)
    lo = (r1 - mid.astype(F32)).astype(BF16)
    return hi, mid, lo


def _dot_exact_l(m01, x):
    hi, mid, lo = _split3(x)
    return _dot(m01, hi) + _dot(m01, mid) + _dot(m01, lo)


def _dot_exact_r(x, m01):
    hi, mid, lo = _split3(x)
    return _dot(hi, m01) + _dot(mid, m01) + _dot(lo, m01)


def _mod_kernel(c_ref, w_ref, b_ref, o_ref):
    s = jax.nn.silu(c_ref[...]).astype(BF16)
    o_ref[0] = _dot(s, w_ref[0].astype(BF16)) + b_ref[0]


def _mod_call(c_all, w_mod, b_mod):
    rows = c_all.shape[0]
    tn = 1536
    return pl.pallas_call(
        _mod_kernel,
        grid=(DEPTH, 6 * D // tn),
        in_specs=[pl.BlockSpec((rows, D), lambda l, j: (0, 0)),
                  pl.BlockSpec((1, D, tn), lambda l, j: (l, 0, j)),
                  pl.BlockSpec((1, 1, tn), lambda l, j: (l, 0, j))],
        out_specs=pl.BlockSpec((1, rows, tn), lambda l, j: (l, 0, j)),
        out_shape=jax.ShapeDtypeStruct((DEPTH, rows, 6 * D), F32),
        compiler_params=_cparams(("arbitrary", "arbitrary")),
        name="mod",
    )(c_all, w_mod, b_mod.reshape(DEPTH, 1, 6 * D))


def _inproj_kernel(x_ref, mod_ref, w_ref, b_ref, o_ref, on_ref):
    m = mod_ref[0]
    u = (x_ref[0] * (1.0 + m[:, D:2 * D]) + m[:, 0:D]).astype(BF16)
    o_ref[0] = _dot(u, w_ref[:, :PM]) + b_ref[:, :PM]
    on_ref[0] = (_dot(u, w_ref[:, PM:]) + b_ref[:, PM:]).astype(BF16)


def _inproj_call(x, mod3, w_p, b_p, shared_row):
    B, T, _ = x.shape
    tm = min(T, 512)
    if shared_row is None:
        mod_map = lambda b, i: (b, 0, 0)
    else:
        mod_map = lambda b, i: (shared_row, 0, 0)
    return pl.pallas_call(
        _inproj_kernel,
        grid=(B, T // tm),
        in_specs=[pl.BlockSpec((1, tm, D), lambda b, i: (b, i, 0)),
                  pl.BlockSpec((1, 1, 6 * D), mod_map),
                  pl.BlockSpec((D, PW), lambda b, i: (0, 0)),
                  pl.BlockSpec((1, PW), lambda b, i: (0, 0))],
        out_specs=[pl.BlockSpec((1, tm, PM), lambda b, i: (b, i, 0)),
                   pl.BlockSpec((1, tm, PN), lambda b, i: (b, i, 0))],
        out_shape=[jax.ShapeDtypeStruct((B, T, PM), F32), jax.ShapeDtypeStruct((B, T, PN), BF16)],
        compiler_params=_cparams(("arbitrary", "arbitrary")),
        name="inproj",
    )(x, mod3, w_p, b_p)


def _softmax_av(s_list, v_list):
    m = s_list[0].max(-1, keepdims=True)
    for s in s_list[1:]:
        m = jnp.maximum(m, s.max(-1, keepdims=True))
    acc = None
    l = None
    for s, v in zip(s_list, v_list):
        p = jnp.exp(s - m)
        ls = p.sum(-1, keepdims=True)
        o = _dot(p.astype(BF16), v)
        acc = o if acc is None else acc + o
        l = ls if l is None else l + ls
    return acc / l


NAT_GROUP = 8


def _natten_kernel(q_ref, k_ref, v_ref, qc_ref, kc_ref, vc_ref, bias_ref, o_ref, oc_ref, *, n_rows):
    kcb = kc_ref[0]
    vcb = vc_ref[0]
    lane = lax.broadcasted_iota(jnp.int32, (1, LANE), 1)
    head0 = lane < HEAD_DIM
    scale = jnp.asarray(HEAD_DIM ** -0.5, BF16)
    zero = jnp.zeros((), BF16)

    def stack_heads(q):
        return jnp.concatenate([jnp.where(head0, q, zero), jnp.where(head0, zero, q)], 0)

    def unstack(o, n):
        return jnp.where(head0, o[:n], o[n:])

    def rows_body(g, carry):
        koffs, scores = [], []
        for i in range(NAT_GROUP):
            r = g * NAT_GROUP + i
            rs = jnp.clip(r - WIN_ROWS // 2, 0, n_rows - WIN_ROWS)
            qs = stack_heads(q_ref[0, pl.ds(pl.multiple_of(r * GRID_W, GRID_W), GRID_W), :] * scale)
            koff = pl.multiple_of(rs * GRID_W, GRID_W)
            kl = k_ref[0, pl.ds(koff, WIN_ROWS * GRID_W), :]
            koffs.append(koff)
            scores.append([_dot_nt(qs, kl) + bias_ref[r - rs, 0], _dot_nt(qs, kcb)])
        for i in range(NAT_GROUP):
            r = g * NAT_GROUP + i
            vl = v_ref[0, pl.ds(koffs[i], WIN_ROWS * GRID_W), :]
            o = _softmax_av(scores[i], [vl, vcb])
            o_ref[0, pl.ds(pl.multiple_of(r * GRID_W, GRID_W), GRID_W), :] = unstack(o, GRID_W)
        return carry

    lax.fori_loop(0, n_rows // NAT_GROUP, rows_body, 0)

    tc = qc_ref.shape[1]
    oc = _softmax_av([_dot_nt(stack_heads(qc_ref[0] * scale), kcb)], [vcb])
    oc_ref[0] = unstack(oc, tc)


def _natten_call(pn_lat, pn_ctx, bias_l):
    B, T, _ = pn_lat.shape
    Tc = pn_ctx.shape[1]
    n_rows = T // GRID_W
    assert n_rows >= WIN_ROWS
    cb = lambda base: (lambda b, p: (b, 0, base // LANE + p))
    return pl.pallas_call(
        functools.partial(_natten_kernel, n_rows=n_rows),
        grid=(B, 4),
        in_specs=[pl.BlockSpec((1, T, LANE), cb(C_NQ)),
                  pl.BlockSpec((1, T, LANE), cb(C_NK)),
                  pl.BlockSpec((1, T, LANE), cb(C_NV)),
                  pl.BlockSpec((1, Tc, LANE), cb(C_NQ)),
                  pl.BlockSpec((1, Tc, LANE), cb(C_NK)),
                  pl.BlockSpec((1, Tc, LANE), cb(C_NV)),
                  pl.BlockSpec((WIN_ROWS, 1, 2 * GRID_W, WIN_ROWS * GRID_W), lambda b, p: (0, p, 0, 0))],
        out_specs=[pl.BlockSpec((1, T, LANE), lambda b, p: (b, 0, p)),
                   pl.BlockSpec((1, Tc, LANE), lambda b, p: (b, 0, p))],
        out_shape=[jax.ShapeDtypeStruct((B, T, 4 * LANE), F32),
                   jax.ShapeDtypeStruct((B, Tc, 4 * LANE), F32)],
        compiler_params=_cparams(("arbitrary", "arbitrary")),
        name="natten",
    )(pn_lat, pn_lat, pn_lat, pn_ctx, pn_ctx, pn_ctx, bias_l)


def _natten_bias_tables(rpb):
    col = np.arange(GRID_W)
    cstart = np.clip(col - WIN_COLS // 2, 0, GRID_W - WIN_COLS)
    col_ok = (col[None, :] >= cstart[:, None]) & (col[None, :] < cstart[:, None] + WIN_COLS)
    dc_idx = np.clip(col[None, :] - col[:, None] + WIN_COLS - 1, 0, 2 * WIN_COLS - 2)
    toe = rpb.astype(F32)[..., dc_idx]
    toe = jnp.where(col_ok[None, None, None], toe, NEG)
    t = jnp.stack([toe[:, :, WIN_ROWS - 1 - d:2 * WIN_ROWS - 1 - d] for d in range(WIN_ROWS)], 1)
    t = t.transpose(0, 1, 2, 4, 3, 5)
    return t.reshape(DEPTH, WIN_ROWS, 4, 2 * GRID_W, WIN_ROWS * GRID_W)


def _tri_masks():
    r = lax.broadcasted_iota(jnp.int32, (CHUNK, CHUNK), 0)
    c = lax.broadcasted_iota(jnp.int32, (CHUNK, CHUNK), 1)
    return r >= c, r <= c


def _rope(x, cs, sn, first, dist):
    w = x.shape[-1]
    partner = jnp.where(first, pltpu.roll(x, w - dist, 1), pltpu.roll(x, dist, 1))
    return x * cs + partner * sn


def _rope_tables(T, n_heads, half):
    t = jnp.arange(T)
    rows = (t // GRID_W).astype(F32)
    cols = (t % GRID_W).astype(F32)
    inv = ROPE_BASE ** (-jnp.arange(half, dtype=F32) / half)
    ar = rows[:, None] * inv[None, :]
    ac = cols[:, None] * inv[None, :]
    cos = jnp.concatenate([jnp.cos(ar), jnp.cos(ar), jnp.cos(ac), jnp.cos(ac)], -1)
    sin = jnp.concatenate([-jnp.sin(ar), jnp.sin(ar), -jnp.sin(ac), jnp.sin(ac)], -1)
    return jnp.tile(cos, (1, n_heads)), jnp.tile(sin, (1, n_heads))


def _bwd_chunk(c, nc, n):
    return jnp.where(c < nc, nc - 1 - c, n - 1 - c + nc)


def _mlstm_kernel(qkl_ref, vl_ref, sml_ref, qkc_ref, vc_ref, smc_ref, cos_ref, sin_ref, cw_ref,
                  ol_ref, oc_ref,
                  qt_s, k_s, vat_s, gcol_s, grow_s, brow_s, cm_s, ot_s, ck_s, *, nc, n):
    low, upp = _tri_masks()
    tri = jnp.where(low, 1.0, 0.0).astype(BF16)
    triu = jnp.where(upp, 1.0, 0.0).astype(BF16)
    lane128 = lax.broadcasted_iota(jnp.int32, (1, LANE), 1)
    lane256 = lax.broadcasted_iota(jnp.int32, (1, 2 * LANE), 1)
    rowi = lax.broadcasted_iota(jnp.int32, (CHUNK, 1), 0)
    row16 = lax.broadcasted_iota(jnp.int32, (16, 1), 0)
    first = (lane256 % 32) < 16
    hmask = [(lane256 >= HEAD_DIM * h) & (lane256 < HEAD_DIM * (h + 1)) for h in range(4)]
    ones_blk = jnp.where(lax.broadcasted_iota(jnp.int32, (HEAD_DIM, CHUNK), 0) == 0, 1.0, 0.0)
    fwd_rows = row16 < 8
    cw = cw_ref[...]

    def cummax_lanes(x, suffix):
        sh = 1
        while sh < CHUNK:
            if suffix:
                moved = jnp.where(lane256 < CHUNK - sh, pltpu.roll(x, CHUNK - sh, 1), NEG)
            else:
                moved = jnp.where(lane256 >= sh, pltpu.roll(x, sh, 1), NEG)
            x = jnp.maximum(x, moved)
            sh *= 2
        return x

    def prep(qk_ref, v_ref, sm_ref, ci, n_str, dst, use_rope):
        r0 = ci * CHUNK
        xc = qk_ref[0, r0:r0 + CHUNK, :]
        prev = qk_ref[0, r0 - 1:r0, :] if ci > 0 else jnp.zeros((1, 4 * LANE), F32)
        nxt = qk_ref[0, r0 + CHUNK:r0 + CHUNK + 1, :] if ci < n_str - 1 else jnp.zeros((1, 4 * LANE), F32)
        xp = jnp.where(rowi == 0, prev, pltpu.roll(xc, 1, 0))
        xn = jnp.where(rowi == CHUNK - 1, nxt, pltpu.roll(xc, CHUNK - 1, 0))
        y = jax.nn.silu(xp * cw[0:1] + xc * cw[1:2] + xn * cw[2:3])
        q = y[:, :2 * LANE]
        k = y[:, 2 * LANE:] * HEAD_DIM ** -0.5
        if use_rope:
            cs = cos_ref[r0:r0 + CHUNK, :]
            sn = sin_ref[r0:r0 + CHUNK, :]
            q = _rope(q, cs, sn, first, 16)
            k = _rope(k, cs, sn, first, 16)
        qt_s[dst] = q.T.astype(BF16)
        k_s[dst] = k.astype(BF16)
        vt = v_ref[0, r0:r0 + CHUNK, :].T
        for h in range(4):
            vat_s[h, dst] = jnp.concatenate([vt[HEAD_DIM * h:HEAD_DIM * (h + 1)], ones_blk], 0).astype(BF16)
        g = sm_ref[0, r0:r0 + CHUNK, :]
        lf = pltpu.roll(jax.nn.log_sigmoid(g), LANE - 4, 1)
        gcol_s[dst] = g - jnp.where(lane128 < 8, _dot_exact_l(tri, lf), _dot_exact_l(triu, lf))
        gt = g.T[0:16]
        lft = pltpu.roll(jax.nn.log_sigmoid(gt), 12, 0)
        brow = jnp.where(fwd_rows, _dot_exact_r(lft, triu), _dot_exact_r(lft, tri))
        brow_s[dst] = brow
        grow_s[dst] = gt - brow

    for ci in range(nc):
        prep(qkc_ref, vc_ref, smc_ref, ci, nc, ci, False)
    for ci in range(n - nc):
        prep(qkl_ref, vl_ref, sml_ref, ci, n - nc, nc + ci, True)
    g_all = grow_s[...].reshape(n * 16, CHUNK)
    fwd_all = (lax.broadcasted_iota(jnp.int32, (n * 16, 1), 0) % 16) < 8
    cm_s[...] = jnp.where(fwd_all, cummax_lanes(g_all, False), cummax_lanes(g_all, True)).reshape(n, 16, CHUNK)

    def scan_dir(bwd):
        di = 1 if bwd else 0
        gi = 8 if bwd else 0
        causal = low if bwd else upp

        def matmuls_first(c):
            ch = _bwd_chunk(c, nc, n) if bwd else c
            qt = qt_s[ch]
            kb = k_s[ch]
            ck_b = ck_s[di].astype(BF16)
            scores, inters = [], []
            for h in range(4):
                pr = slice(LANE * (h // 2), LANE * (h // 2 + 1))
                own = (lane128 < HEAD_DIM) if h % 2 == 0 else (lane128 >= HEAD_DIM)
                zero = jnp.zeros((), BF16)
                scores.append(_dot(jnp.where(own, kb[:, pr], zero), qt[pr, :]))
                inters.append(_dot(jnp.where(own, ck_b[:, pr], zero), qt[pr, :]))
            return ch, kb, scores, inters

        def rest(first, ms):
            ch, kb, scores, inters = first
            gcol_all = gcol_s[ch]
            new_ms = []
            for h in range(4):
                m = ms[h]
                g_row = grow_s[ch, gi + h:gi + h + 1, :]
                b_row = brow_s[ch, gi + h:gi + h + 1, :]
                a_row = jnp.maximum(m, cm_s[ch, gi + h:gi + h + 1, :])
                w = jnp.exp(jnp.where(causal, gcol_all[:, gi + h:gi + h + 1] - a_row, NEG))
                pt = (scores[h] * w).astype(BF16)
                vat = vat_s[h, ch]
                nd = _dot(vat, pt) + jnp.exp(m - a_row) * inters[h]
                den = nd[HEAD_DIM:HEAD_DIM + 1, :]
                ht = nd[0:HEAD_DIM] / jnp.maximum(jnp.abs(den), jnp.exp(-(b_row + a_row)))
                ot_s[di, ch, HEAD_DIM * h:HEAD_DIM * (h + 1), :] = ht
                bl = b_row[:, 0:1] if bwd else b_row[:, CHUNK - 1:CHUNK]
                lw_end = bl + g_row
                m_new = jnp.maximum(bl + m, lw_end.max(-1, keepdims=True))
                upd = _dot((vat * jnp.exp(lw_end - m_new)).astype(BF16), kb)
                ck_s[di] = jnp.where(hmask[h], jnp.exp(bl + m - m_new) * ck_s[di] + upd, ck_s[di])
                new_ms.append(m_new)
            return tuple(new_ms)

        return matmuls_first, rest

    ck_s[...] = jnp.zeros_like(ck_s)
    first_f, rest_f = scan_dir(False)
    first_b, rest_b = scan_dir(True)

    def chunk_body(c, carry):
        ff, fb = first_f(c), first_b(c)
        return rest_f(ff, carry[0]), rest_b(fb, carry[1])

    zeros4 = tuple(jnp.zeros((1, 1), F32) for _ in range(4))
    lax.fori_loop(0, n, chunk_body, (zeros4, zeros4))
    for ci in range(nc):
        oc_ref[0, ci * CHUNK:(ci + 1) * CHUNK, :] = (ot_s[0, ci] + ot_s[1, ci]).T
    for ci in range(n - nc):
        ol_ref[0, ci * CHUNK:(ci + 1) * CHUNK, :] = (ot_s[0, nc + ci] + ot_s[1, nc + ci]).T


def _mlstm_call(p_lat, p_ctx, cos, sin, conv_w_l):
    B, T, _ = p_lat.shape
    Tc = p_ctx.shape[1]
    nc, n = Tc // CHUNK, (Tc + T) // CHUNK
    cb = lambda base, w: (lambda b: (b, 0, base // w))
    return pl.pallas_call(
        functools.partial(_mlstm_kernel, nc=nc, n=n),
        grid=(B,),
        in_specs=[pl.BlockSpec((1, T, 512), cb(C_MQK, 512)),
                  pl.BlockSpec((1, T, 256), cb(C_MV, 256)),
                  pl.BlockSpec((1, T, LANE), cb(C_SM, LANE)),
                  pl.BlockSpec((1, Tc, 512), cb(C_MQK, 512)),
                  pl.BlockSpec((1, Tc, 256), cb(C_MV, 256)),
                  pl.BlockSpec((1, Tc, LANE), cb(C_SM, LANE)),
                  pl.BlockSpec((T, 256), lambda b: (0, 0)),
                  pl.BlockSpec((T, 256), lambda b: (0, 0)),
                  pl.BlockSpec((CONV_K, 512), lambda b: (0, 0))],
        out_specs=[pl.BlockSpec((1, T, 256), lambda b: (b, 0, 0)),
                   pl.BlockSpec((1, Tc, 256), lambda b: (b, 0, 0))],
        out_shape=[jax.ShapeDtypeStruct((B, T, 256), F32),
                   jax.ShapeDtypeStruct((B, Tc, 256), F32)],
        scratch_shapes=[pltpu.VMEM((n, 256, CHUNK), BF16),
                        pltpu.VMEM((n, CHUNK, 256), BF16),
                        pltpu.VMEM((4, n, LANE, CHUNK), BF16),
                        pltpu.VMEM((n, CHUNK, LANE), F32),
                        pltpu.VMEM((n, 16, CHUNK), F32),
                        pltpu.VMEM((n, 16, CHUNK), F32),
                        pltpu.VMEM((n, 16, CHUNK), F32),
                        pltpu.VMEM((2, n, 256, CHUNK), F32),
                        pltpu.VMEM((2, LANE, 256), F32)],
        compiler_params=_cparams(("arbitrary",)),
        name="mlstm",
    )(p_lat, p_lat, p_lat, p_ctx, p_ctx, p_ctx, cos, sin, conv_w_l)


GLA_BLK = 64
GLA_CLAMP = 80.0


def _gla_kernel(ql_ref, kl_ref, vl_ref, sml_ref, qc_ref, kc_ref, vc_ref, smc_ref, cos_ref, sin_ref, w2_ref, b2_ref,
                ol_ref, oc_ref,
                q_s, k_s, v_s, a_s, b_s, o_s, st_s, *, nc, n):
    low, upp = _tri_masks()
    tri = jnp.where(low, 1.0, 0.0).astype(BF16)
    triu = jnp.where(upp, 1.0, 0.0).astype(BF16)
    lane128 = lax.broadcasted_iota(jnp.int32, (1, LANE), 1)
    lane256 = lax.broadcasted_iota(jnp.int32, (1, 2 * LANE), 1)
    first = (lane128 % 16) < 8
    hm128 = [(lane128 >= 32 * h) & (lane128 < 32 * (h + 1)) for h in range(4)]
    hm256 = [(lane256 >= 64 * h) & (lane256 < 64 * (h + 1)) for h in range(4)]
    nb = CHUNK // GLA_BLK
    r_st = lax.broadcasted_iota(jnp.int32, (nb * GLA_BLK, CHUNK), 0)
    c_st = lax.broadcasted_iota(jnp.int32, (nb * GLA_BLK, CHUNK), 1)
    bd_r = lax.broadcasted_iota(jnp.int32, (LANE, 2 * LANE), 0)
    bd_c = lax.broadcasted_iota(jnp.int32, (LANE, 2 * LANE), 1)
    blockdiag = (bd_r // 32) == (bd_c // 64)

    def prep(q_ref, k_ref, v_ref, sm_ref, ci, dst, use_rope):
        r0 = ci * CHUNK
        q = q_ref[0, r0:r0 + CHUNK, :] * 32 ** -0.5
        k = k_ref[0, r0:r0 + CHUNK, :]
        if use_rope:
            cs = cos_ref[r0:r0 + CHUNK, :]
            sn = sin_ref[r0:r0 + CHUNK, :]
            q = _rope(q, cs, sn, first, 8)
            k = _rope(k, cs, sn, first, 8)
        q_s[dst] = q
        k_s[dst] = k
        v_s[dst] = v_ref[0, r0:r0 + CHUNK, :].astype(BF16)
        lr = sm_ref[0, r0:r0 + CHUNK, :].astype(BF16)
        for d in range(2):
            a = jax.nn.log_sigmoid(_dot(lr, w2_ref[d]) + b2_ref[d]) / GLA_TAU
            a_s[d, dst] = a
            b_s[d, dst] = _dot_exact_l(triu if d else tri, a)

    for ci in range(nc):
        prep(qc_ref, kc_ref, vc_ref, smc_ref, ci, ci, False)
    for ci in range(n - nc):
        prep(ql_ref, kl_ref, vl_ref, sml_ref, ci, nc + ci, True)

    def scan_dir(bwd):
        d = 1 if bwd else 0
        st_s[...] = jnp.zeros_like(st_s)

        def chunk_body(c, carry):
            ch = _bwd_chunk(c, nc, n) if bwd else c
            q = q_s[ch]
            k = k_s[ch]
            vb = v_s[ch]
            a = a_s[d, ch]
            b = b_s[d, ch]
            st_b = st_s[...].astype(BF16)
            o_inter = _dot((q * jnp.exp(b)).astype(BF16), st_b)
            atts = []
            for i in range(nb):
                rows = slice(i * GLA_BLK, (i + 1) * GLA_BLK)
                e = (i + 1) * GLA_BLK - 1 if bwd else i * GLA_BLK
                ref = b[e:e + 1, :] - a[e:e + 1, :]
                qs = q[rows] * jnp.exp(b[rows] - ref)
                ks = (k * jnp.exp(jnp.minimum(ref - b, GLA_CLAMP))).astype(BF16)
                lhs = jnp.concatenate([jnp.where(hm128[h], qs, 0.0) for h in range(4)], 0).astype(BF16)
                att = _dot_nt(lhs, ks)
                t_idx = (r_st % GLA_BLK) + i * GLA_BLK
                ok = (c_st >= t_idx) if bwd else (c_st <= t_idx)
                atts.append(jnp.where(ok, att, 0.0).astype(BF16))
            for i in range(nb):
                rows = slice(i * GLA_BLK, (i + 1) * GLA_BLK)
                oh = _dot(atts[i], vb)
                o_blk = o_inter[rows]
                for h in range(4):
                    o_blk = o_blk + jnp.where(hm256[h], oh[h * GLA_BLK:(h + 1) * GLA_BLK], 0.0)
                if bwd:
                    o_s[ch, rows, :] = o_s[ch, rows, :] + o_blk
                else:
                    o_s[ch, rows, :] = o_blk
            bt = b.T
            tot = bt[:, 0:1] if bwd else bt[:, CHUNK - 1:CHUNK]
            kt = (k.T * jnp.exp(tot - bt)).astype(BF16)
            upd = jnp.where(blockdiag, _dot(kt, vb), 0.0)
            st_s[...] = jnp.exp(tot) * st_s[...] + upd
            return carry

        lax.fori_loop(0, n, chunk_body, 0)

    scan_dir(False)
    scan_dir(True)
    for ci in range(nc):
        oc_ref[0, ci * CHUNK:(ci + 1) * CHUNK, :] = o_s[ci]
    for ci in range(n - nc):
        ol_ref[0, ci * CHUNK:(ci + 1) * CHUNK, :] = o_s[nc + ci]


def _gla_call(p_lat, p_ctx, cos, sin, w2p, b2p):
    B, T, _ = p_lat.shape
    Tc = p_ctx.shape[1]
    nc, n = Tc // CHUNK, (Tc + T) // CHUNK
    cb = lambda base, w: (lambda b: (b, 0, base // w))
    return pl.pallas_call(
        functools.partial(_gla_kernel, nc=nc, n=n),
        grid=(B,),
        in_specs=[pl.BlockSpec((1, T, LANE), cb(C_GQ, LANE)),
                  pl.BlockSpec((1, T, LANE), cb(C_GK, LANE)),
                  pl.BlockSpec((1, T, 256), cb(C_GV, 256)),
                  pl.BlockSpec((1, T, LANE), cb(C_SM, LANE)),
                  pl.BlockSpec((1, Tc, LANE), cb(C_GQ, LANE)),
                  pl.BlockSpec((1, Tc, LANE), cb(C_GK, LANE)),
                  pl.BlockSpec((1, Tc, 256), cb(C_GV, 256)),
                  pl.BlockSpec((1, Tc, LANE), cb(C_SM, LANE)),
                  pl.BlockSpec((T, LANE), lambda b: (0, 0)),
                  pl.BlockSpec((T, LANE), lambda b: (0, 0)),
                  pl.BlockSpec((2, LANE, LANE), lambda b: (0, 0, 0)),
                  pl.BlockSpec((2, 1, LANE), lambda b: (0, 0, 0))],
        out_specs=[pl.BlockSpec((1, T, 256), lambda b: (b, 0, 0)),
                   pl.BlockSpec((1, Tc, 256), lambda b: (b, 0, 0))],
        out_shape=[jax.ShapeDtypeStruct((B, T, 256), F32),
                   jax.ShapeDtypeStruct((B, Tc, 256), F32)],
        scratch_shapes=[pltpu.VMEM((n, CHUNK, LANE), F32),
                        pltpu.VMEM((n, CHUNK, LANE), F32),
                        pltpu.VMEM((n, CHUNK, 256), BF16),
                        pltpu.VMEM((2, n, CHUNK, LANE), F32),
                        pltpu.VMEM((2, n, CHUNK, LANE), F32),
                        pltpu.VMEM((n, CHUNK, 256), F32),
                        pltpu.VMEM((LANE, 256), F32)],
        compiler_params=_cparams(("arbitrary",)),
        name="gla",
    )(p_lat, p_lat, p_lat, p_lat, p_ctx, p_ctx, p_ctx, p_ctx, cos, sin, w2p, b2p)


def _gla_gate_weights(gla_w2, gla_b2):
    w = jnp.zeros((DEPTH, 2, LANE, LANE), F32)
    w = w.at[:, 0, 16:32].set(gla_w2[:, 0]).at[:, 1, 32:48].set(gla_w2[:, 1])
    return w.astype(BF16), gla_b2.astype(F32).reshape(DEPTH, 2, 1, LANE)


TILE_ROWS = D // LANE


def _store_token_tiles(ref, val):
    tm = val.shape[0]
    for s in range(TILE_ROWS):
        ref[0, pl.ds(s, tm, stride=TILE_ROWS), :] = val[:, s * LANE:(s + 1) * LANE]


def _load_token_tiles(ref, lead, t0, tm):
    return [ref[lead + (pl.ds(t0 * TILE_ROWS + s, tm, stride=TILE_ROWS), slice(None))] for s in range(TILE_ROWS)]


def _layer_norm(z, g, b):
    mu = z.mean(-1, keepdims=True)
    zc = z - mu
    var = jnp.mean(jnp.square(zc), -1, keepdims=True)
    return zc * lax.rsqrt(var + LN_EPS) * g + b


def _merge_kernel(hm_ref, hg_ref, hn_ref, mo_ref, gr_ref, x_ref, mod_ref, wout_ref, wr_ref, ng_ref, ln_ref,
                  x1_ref, lg_ref):
    r = lax.broadcasted_iota(jnp.int32, (256, 256), 0)
    c = lax.broadcasted_iota(jnp.int32, (256, 256), 1)
    avg = jnp.where((r // HEAD_DIM) == (c // HEAD_DIM), 1.0 / HEAD_DIM, 0.0).astype(BF16)

    def seg_mean(x):
        hi = x.astype(BF16)
        lo = (x - hi.astype(F32)).astype(BF16)
        return _dot(hi, avg) + _dot(lo, avg)

    def head_norm(h):
        d = h - seg_mean(h)
        return d * lax.rsqrt(seg_mean(d * d) + LN_EPS)

    ym = head_norm(hm_ref[0]) * ng_ref[0:1, :] * jax.nn.sigmoid(mo_ref[0])
    yg = head_norm(hg_ref[0]) * ng_ref[1:2, :] * jax.nn.silu(gr_ref[0])
    y = (_dot(ym.astype(BF16), wout_ref[0:256, :]) + _dot(yg.astype(BF16), wout_ref[256:512, :])
         + _dot(hn_ref[0].astype(BF16), wout_ref[512:1024, :]))
    m = mod_ref[0]
    x1 = _layer_norm(DEEPNORM_ALPHA * x_ref[0] + m[:, 2 * D:3 * D] * y, ln_ref[0:1, :], ln_ref[1:2, :])
    _store_token_tiles(x1_ref, x1)
    u2 = x1 * (1.0 + m[:, 4 * D:5 * D]) + m[:, 3 * D:4 * D]
    lg_ref[0] = _dot_nt(wr_ref[...], u2.astype(BF16))


def _merge_call(hm, hg, hn, p, x, mod3, wout_b, wrt_b, ng, ln, shared_row):
    B, T, _ = x.shape
    tm = min(T, 512)
    mod_map = (lambda b, i: (b, 0, 0)) if shared_row is None else (lambda b, i: (shared_row, 0, 0))
    tok = lambda w: pl.BlockSpec((1, tm, w), lambda b, i: (b, i, 0))
    return pl.pallas_call(
        _merge_kernel,
        grid=(B, T // tm),
        in_specs=[tok(256), tok(256), tok(512),
                  pl.BlockSpec((1, tm, 256), lambda b, i: (b, i, C_MO // 256)),
                  pl.BlockSpec((1, tm, 256), lambda b, i: (b, i, C_GR // 256)),
                  tok(D),
                  pl.BlockSpec((1, 1, 6 * D), mod_map),
                  pl.BlockSpec((D, D), lambda b, i: (0, 0)),
                  pl.BlockSpec((N_EXPERTS, D), lambda b, i: (0, 0)),
                  pl.BlockSpec((2, 256), lambda b, i: (0, 0)),
                  pl.BlockSpec((2, D), lambda b, i: (0, 0))],
        out_specs=[pl.BlockSpec((1, tm * TILE_ROWS, LANE), lambda b, i: (b, i, 0)),
                   pl.BlockSpec((1, N_EXPERTS, tm), lambda b, i: (b, 0, i))],
        out_shape=[jax.ShapeDtypeStruct((B, T * TILE_ROWS, LANE), F32),
                   jax.ShapeDtypeStruct((B, N_EXPERTS, T), F32)],
        compiler_params=_cparams(("arbitrary", "arbitrary")),
        name="merge",
    )(hm, hg, hn, p, p, x, mod3, wout_b, wrt_b, ng, ln)


ROUTER_SAMPLES = 8


def _router_samples(B):
    ns = ROUTER_SAMPLES
    while B % ns:
        ns -= 1
    return ns


def _router_kernel(lg_ref, o_ref, aff_s, sp_s, *, T, cap):
    J = T // LANE
    E = N_EXPERTS
    NS = lg_ref.shape[0]
    keys = [[] for _ in range(NS)]
    for j in range(J):
        for si in range(NS):
            lg = lg_ref[si, :, j * LANE:(j + 1) * LANE]
            ex = jnp.exp(lg - lg.max(0, keepdims=True))
            aff = ex / ex.sum(0, keepdims=True)
            aff_s[si, j * E:(j + 1) * E, :] = aff
            keys[si].append(pltpu.bitcast(aff, jnp.int32))

    def count(ks, pred):
        cnt = None
        for k in ks:
            cj = jnp.where(pred(k), 1.0, 0.0)
            cnt = cj if cnt is None else cnt + cj
        return cnt.sum(-1, keepdims=True)

    thrs = [jnp.zeros((E, 1), jnp.int32) for _ in range(NS)]
    for bit in range(30, -1, -1):
        for si in range(NS):
            cand = thrs[si] | (1 << bit)
            thrs[si] = jnp.where(count(keys[si], lambda k: k >= cand) >= cap, cand, thrs[si])

    r = lax.broadcasted_iota(jnp.int32, (LANE, LANE), 0)
    c = lax.broadcasted_iota(jnp.int32, (LANE, LANE), 1)
    upper = jnp.where(r <= c, 1.0, 0.0).astype(BF16)
    ones = jnp.ones((LANE, LANE), BF16)
    rr = lax.broadcasted_iota(jnp.int32, (J * E, J * E), 0)
    cc = lax.broadcasted_iota(jnp.int32, (J * E, J * E), 1)
    earlier = jnp.where(((rr % E) == (cc % E)) & ((cc // E) < (rr // E)), 1.0, 0.0).astype(BF16)

    def prefix(x01):
        xb = x01.astype(BF16)
        return _dot(xb, upper) + _dot(earlier, _dot(xb, ones).astype(BF16))

    for si in range(NS):
        thr = thrs[si]
        need = cap - count(keys[si], lambda k: k > thr)
        gt = jnp.concatenate([jnp.where(k > thr, 1.0, 0.0) for k in keys[si]], 0)
        eq = jnp.concatenate([jnp.where(k == thr, 1.0, 0.0) for k in keys[si]], 0)
        need_t = jnp.concatenate([need] * J, 0)
        sel = jnp.maximum(gt, jnp.where(prefix(eq) <= need_t, eq, 0.0))
        sp_s[si] = jnp.where(sel > 0.0, prefix(sel) - 1.0, -1.0)

    sb = min(cap, LANE)
    lane = lax.broadcasted_iota(jnp.int32, (1, LANE), 1)
    o_ref[...] = jnp.zeros_like(o_ref)
    for e in range(E):
        for half in range(cap // sb):
            slot = (lax.broadcasted_iota(jnp.int32, (sb, LANE), 0) + half * sb).astype(F32)
            rows = slice(half * sb, (half + 1) * sb)
            for si in range(NS):
                def jbody(j, acc):
                    acc_i, acc_g = acc
                    sp = sp_s[si, pl.ds(j * E + e, 1), :]
                    af = aff_s[si, pl.ds(j * E + e, 1), :]
                    hit = sp == slot
                    tid = (lane + j * LANE).astype(F32)
                    return jnp.where(hit, tid, acc_i), jnp.where(hit, af, acc_g)

                acc_i, acc_g = lax.fori_loop(0, J, jbody, (jnp.zeros((sb, LANE), F32), jnp.zeros((sb, LANE), F32)),
                                             unroll=2)
                icol = acc_i.sum(-1, keepdims=True)
                gcol = acc_g.sum(-1, keepdims=True)
                o_ref[si, rows, :] = jnp.where(lane == e, icol, jnp.where(lane == E + e, gcol, o_ref[si, rows, :]))


def _router_call(logits):
    B, _, T = logits.shape
    cap = CAPACITY_FACTOR * T // N_EXPERTS
    J = T // LANE
    ns = _router_samples(B)
    out = pl.pallas_call(
        functools.partial(_router_kernel, T=T, cap=cap),
        grid=(B // ns,),
        in_specs=[pl.BlockSpec((ns, N_EXPERTS, T), lambda b: (b, 0, 0))],
        out_specs=pl.BlockSpec((ns, cap, LANE), lambda b: (b, 0, 0)),
        out_shape=jax.ShapeDtypeStruct((B, cap, LANE), F32),
        scratch_shapes=[pltpu.VMEM((ns, J * N_EXPERTS, LANE), F32), pltpu.VMEM((ns, J * N_EXPERTS, LANE), F32)],
        compiler_params=_cparams(("arbitrary",)),
        name="router",
    )(logits)
    idx = (out[:, :, :N_EXPERTS].astype(jnp.int32) * TILE_ROWS).transpose(0, 2, 1).reshape(B, 1, N_EXPERTS * cap)
    gate = out[:, :, N_EXPERTS:2 * N_EXPERTS].transpose(0, 2, 1).reshape(B, 1, N_EXPERTS * cap)
    return idx, gate


GATHER_BLOCK_BYTES = 4 * 1024 * 1024
SCATTER_BLOCK_BYTES = 4 * 1024 * 1024


def _experts_per_step(cap, block_bytes):
    eg = 2
    while eg < N_EXPERTS and 2 * eg * cap * D * 4 <= block_bytes:
        eg *= 2
    return eg


def _as_token_tile(v):
    return jnp.concatenate([v[:, s * LANE:(s + 1) * LANE] for s in range(TILE_ROWS)], 0)


def _gather_kernel(idx_ref, x_ref, mod_ref, o_ref, *, cap, eg):
    g = pl.program_id(1)
    m = mod_ref[0]
    scale = _as_token_tile(1.0 + m[:, 4 * D:5 * D])
    shift = _as_token_tile(m[:, 3 * D:4 * D])
    half = cap // 2

    def row_bits(r):
        src = pl.ds(pl.multiple_of(r, TILE_ROWS), TILE_ROWS)
        u = (x_ref[0, src, :] * scale + shift).astype(BF16).astype(F32)
        return pltpu.bitcast(u, jnp.uint32)

    for e in range(eg):
        def body(s, carry):
            base = (g * eg + e) * cap + s
            lo = row_bits(idx_ref[0, 0, base]) >> 16
            hi = row_bits(idx_ref[0, 0, base + half]) & jnp.uint32(0xFFFF0000)
            o_ref[e, 0, pl.ds(pl.multiple_of(s * TILE_ROWS, TILE_ROWS), TILE_ROWS), :] = lo | hi
            return carry
        lax.fori_loop(0, half, body, 0, unroll=4)


def _gather_call(idx, x1t, mod3, shared_row):
    B = x1t.shape[0]
    T = x1t.shape[1] // TILE_ROWS
    cap = CAPACITY_FACTOR * T // N_EXPERTS
    eg = _experts_per_step(cap, GATHER_BLOCK_BYTES)
    mod_map = (lambda b, g: (b, 0, 0)) if shared_row is None else (lambda b, g: (shared_row, 0, 0))
    smem = lambda: pl.BlockSpec((1, 1, N_EXPERTS * cap), lambda b, g: (b, 0, 0), memory_space=pltpu.SMEM)
    return pl.pallas_call(
        functools.partial(_gather_kernel, cap=cap, eg=eg),
        grid=(B, N_EXPERTS // eg),
        in_specs=[smem(), pl.BlockSpec((1, T * TILE_ROWS, LANE), lambda b, g: (b, 0, 0)),
                  pl.BlockSpec((1, 1, 6 * D), mod_map)],
        out_specs=pl.BlockSpec((eg, 1, cap // 2 * TILE_ROWS, LANE), lambda b, g: (g, b, 0, 0)),
        out_shape=jax.ShapeDtypeStruct((N_EXPERTS, B, cap // 2 * TILE_ROWS, LANE), jnp.uint32),
        compiler_params=_cparams(("arbitrary", "arbitrary")),
        name="gather",
    )(idx, x1t, mod3)


FF_CHUNK = 512


def _ffn_kernel(x_ref, wg_ref, wu_ref, wd_ref, o_ref, xs, *, tm, half):
    for s, words in enumerate(_load_token_tiles(x_ref, (0,), 0, tm // 2)):
        lo = pltpu.bitcast(words << 16, F32).astype(BF16)
        hi = pltpu.bitcast(words & jnp.uint32(0xFFFF0000), F32).astype(BF16)
        for blk in range(tm // (2 * half)):
            src = slice(blk * half, (blk + 1) * half)
            xs[2 * blk * half:(2 * blk + 1) * half, s * LANE:(s + 1) * LANE] = lo[src]
            xs[(2 * blk + 1) * half:(2 * blk + 2) * half, s * LANE:(s + 1) * LANE] = hi[src]
    xb = xs[...]
    acc = None
    for c in range(EXPERT_FF // FF_CHUNK):
        cols = slice(c * FF_CHUNK, (c + 1) * FF_CHUNK)
        h = (jax.nn.silu(_dot(xb, wg_ref[0, :, cols])) * _dot(xb, wu_ref[0, :, cols])).astype(BF16)
        t = _dot(h, wd_ref[0, cols, :])
        acc = t if acc is None else acc + t
    _store_token_tiles(o_ref, acc)


def _ffn_call(xe, wg_b, wu_b, wd_b, layer, cap):
    E = xe.shape[0]
    M = 2 * xe.shape[1] // TILE_ROWS
    tm = min(M, 512)
    assert tm % cap == 0
    w_map = lambda e, i: (layer * E + e, 0, 0)
    return pl.pallas_call(
        functools.partial(_ffn_kernel, tm=tm, half=cap // 2),
        grid=(E, M // tm),
        in_specs=[pl.BlockSpec((1, tm // 2 * TILE_ROWS, LANE), lambda e, i: (e, i, 0)),
                  pl.BlockSpec((1, D, EXPERT_FF), w_map),
                  pl.BlockSpec((1, D, EXPERT_FF), w_map),
                  pl.BlockSpec((1, EXPERT_FF, D), w_map)],
        out_specs=pl.BlockSpec((1, tm * TILE_ROWS, LANE), lambda e, i: (e, i, 0)),
        out_shape=jax.ShapeDtypeStruct((E, M * TILE_ROWS, LANE), F32),
        scratch_shapes=[pltpu.VMEM((tm, D), BF16)],
        compiler_params=_cparams(("arbitrary", "arbitrary")),
        name="ffn",
    )(xe, wg_b, wu_b, wd_b)


SCATTER_FIN_ROWS = 256
SCATTER_BATCH = 8


def _scatter_kernel(idx_ref, gate_ref, y_ref, x1_ref, mod_ref, ln_ref, o_ref, acc, *, cap, eg, n_g, T):
    g = pl.program_id(1)

    @pl.when(g == 0)
    def _():
        acc[...] = jnp.zeros_like(acc)

    for e in range(eg):
        def body(i, carry):
            s0 = i * SCATTER_BATCH
            p0 = (g * eg + e) * cap + s0
            dst = [pl.ds(pl.multiple_of(idx_ref[0, 0, p0 + k], TILE_ROWS), TILE_ROWS) for k in range(SCATTER_BATCH)]
            old = [acc[d, :] for d in dst]
            for k in range(SCATTER_BATCH):
                src = pl.ds(pl.multiple_of((s0 + k) * TILE_ROWS, TILE_ROWS), TILE_ROWS)
                acc[dst[k], :] = old[k] + y_ref[e, 0, src, :] * gate_ref[0, 0, p0 + k]
            return carry
        lax.fori_loop(0, cap // SCATTER_BATCH, body, 0, unroll=2)

    @pl.when(g == n_g - 1)
    def _():
        g2 = mod_ref[0][:, 5 * D:6 * D]
        tb = min(T, SCATTER_FIN_ROWS)
        for i in range(T // tb):
            rows = slice(i * tb, (i + 1) * tb)
            f = jnp.concatenate(_load_token_tiles(acc, (), i * tb, tb), -1)
            x1 = jnp.concatenate(_load_token_tiles(x1_ref, (0,), i * tb, tb), -1)
            z = DEEPNORM_ALPHA * x1 + g2 * f
            o_ref[0, rows, :] = _layer_norm(z, ln_ref[0:1, :], ln_ref[1:2, :])


def _scatter_call(idx, gate, y4, x1t, mod3, ln, shared_row):
    B = x1t.shape[0]
    T = x1t.shape[1] // TILE_ROWS
    cap = CAPACITY_FACTOR * T // N_EXPERTS
    eg = _experts_per_step(cap, SCATTER_BLOCK_BYTES)
    n_g = N_EXPERTS // eg
    mod_map = (lambda b, g: (b, 0, 0)) if shared_row is None else (lambda b, g: (shared_row, 0, 0))
    smem = lambda: pl.BlockSpec((1, 1, N_EXPERTS * cap), lambda b, g: (b, 0, 0), memory_space=pltpu.SMEM)
    return pl.pallas_call(
        functools.partial(_scatter_kernel, cap=cap, eg=eg, n_g=n_g, T=T),
        grid=(B, n_g),
        scratch_shapes=[pltpu.VMEM((T * TILE_ROWS, LANE), F32)],
        in_specs=[smem(), smem(),
                  pl.BlockSpec((eg, 1, cap * TILE_ROWS, LANE), lambda b, g: (g, b, 0, 0)),
                  pl.BlockSpec((1, T * TILE_ROWS, LANE), lambda b, g: (b, 0, 0)),
                  pl.BlockSpec((1, 1, 6 * D), mod_map),
                  pl.BlockSpec((2, D), lambda b, g: (0, 0))],
        out_specs=pl.BlockSpec((1, T, D), lambda b, g: (b, 0, 0)),
        out_shape=jax.ShapeDtypeStruct((B, T, D), F32),
        compiler_params=_cparams(("arbitrary", "arbitrary")),
        name="scatter",
    )(idx, gate, y4, x1t, mod3, ln)


def _moe(x1t, logits, mod3, wg_b, wu_b, wd_b, layer, ln2, shared_row):
    B = x1t.shape[0]
    T = x1t.shape[1] // TILE_ROWS
    cap = CAPACITY_FACTOR * T // N_EXPERTS
    idx, gate = _router_call(logits)
    xe = _gather_call(idx, x1t, mod3, shared_row)
    y = _ffn_call(xe.reshape(N_EXPERTS, B * (cap // 2) * TILE_ROWS, LANE), wg_b, wu_b, wd_b, layer, cap)
    return _scatter_call(idx, gate, y.reshape(N_EXPERTS, B, cap * TILE_ROWS, LANE), x1t, mod3, ln2, shared_row)


def kernel(x, c, ctx, c_ctx, w_mod, b_mod, w_in, b_in, conv_w, gla_w2, gla_b2, mlstm_norm_g, gla_norm_g, rpb, w_out, ln1_g, ln1_b, w_router, w_gate, w_up, w_down, ln2_g, ln2_b):
    B, T, _ = x.shape
    n_mod = -(-(B + 1) // 8) * 8
    c_all = jnp.concatenate([c, c_ctx[None], jnp.zeros((n_mod - B - 1, D), F32)], 0)
    mods = _mod_call(c_all, w_mod, b_mod)
    w_p = _repack_columns(w_in).astype(BF16)
    b_p = _repack_columns(b_in)
    bias = _natten_bias_tables(rpb)
    mcos, msin = _rope_tables(T, 4, 16)
    gcos, gsin = _rope_tables(T, 4, 8)
    w2p, b2p = _gla_gate_weights(gla_w2, gla_b2)
    wout_b = w_out.astype(BF16)
    wrt_b = jnp.swapaxes(w_router, 1, 2).astype(BF16)
    wg_b = w_gate.astype(BF16).reshape(DEPTH * N_EXPERTS, D, EXPERT_FF)
    wu_b = w_up.astype(BF16).reshape(DEPTH * N_EXPERTS, D, EXPERT_FF)
    wd_b = w_down.astype(BF16).reshape(DEPTH * N_EXPERTS, EXPERT_FF, D)
    ng = jnp.stack([mlstm_norm_g, gla_norm_g], 1)
    ln1 = jnp.stack([ln1_g, ln1_b], 1)
    ln2 = jnp.stack([ln2_g, ln2_b], 1)
    for l in range(DEPTH):
        mod3 = mods[l].reshape(n_mod, 1, 6 * D)
        p_lat, pn_lat = _inproj_call(x, mod3, w_p[l], b_p[l][None], None)
        p_ctx, pn_ctx = _inproj_call(ctx, mod3, w_p[l], b_p[l][None], B)
        n_lat, n_ctx = _natten_call(pn_lat, pn_ctx, bias[l])
        m_lat, m_ctx = _mlstm_call(p_lat, p_ctx, mcos, msin, conv_w[l])
        g_lat, g_ctx = _gla_call(p_lat, p_ctx, gcos, gsin, w2p[l], b2p[l])
        x1t, lg = _merge_call(m_lat, g_lat, n_lat, p_lat, x, mod3, wout_b[l], wrt_b[l], ng[l], ln1[l], None)
        x = _moe(x1t, lg, mod3, wg_b, wu_b, wd_b, l, ln2[l], None)
        if l < DEPTH - 1:
            c1t, lgc = _merge_call(m_ctx, g_ctx, n_ctx, p_ctx, ctx, mod3, wout_b[l], wrt_b[l], ng[l], ln1[l], B)
            ctx = _moe(c1t, lgc, mod3, wg_b, wu_b, wd_b, l, ln2[l], B)
    return x
```

```python
import functools

import numpy as np
import jax
import jax.numpy as jnp
from jax import lax
from jax.experimental import pallas as pl
from jax.experimental.pallas import tpu as pltpu

F32 = jnp.float32
BF16 = jnp.bfloat16

D = 1024
DEPTH = 4
GRID_W = 64
HEAD_DIM = 64
WIN_ROWS = 8
WIN_COLS = 16
CONV_K = 3
ROPE_BASE = 10000.0
N_EXPERTS = 16
EXPERT_FF = 2 * D
CAPACITY_FACTOR = 2
LN_EPS = 1e-5
GLA_TAU = 16.0
DEEPNORM_ALPHA = (2 * DEPTH) ** 0.25
NEG = -1e30

VMEM_LIMIT = 56 * 1024 * 1024
LANE = 128
CHUNK = 256

C_MQK, C_MV, C_MO, C_GV, C_GR = 0, 512, 768, 1024, 1280
C_GQ, C_GK, C_SM = 1536, 1664, 1792
PM = 1920
C_NQ, C_NK, C_NV = 0, 512, 1024
PN = 1536
PW = PM + PN
_MAIN_SEGMENTS = ((0, 512), (512, 768), (768, 1024), (1296, 1552), (1552, 1808), (1040, 1168), (1168, 1296),
                  (1024, 1040), (1808, 1840))
_NAT_SEGMENTS = ((1840, 2352), (2352, 2864), (2864, 3376))
_NPAD = PM - sum(b - a for a, b in _MAIN_SEGMENTS)


def _repack_columns(w):
    parts = ([w[..., a:b] for a, b in _MAIN_SEGMENTS] + [jnp.zeros(w.shape[:-1] + (_NPAD,), w.dtype)]
             + [w[..., a:b] for a, b in _NAT_SEGMENTS])
    return jnp.concatenate(parts, -1)


def _cparams(sem):
    return pltpu.CompilerParams(dimension_semantics=sem, vmem_limit_bytes=VMEM_LIMIT)


def _dot(a, b):
    return jnp.dot(a, b, preferred_element_type=F32)


def _dot_nt(a, b):
    return lax.dot_general(a, b, (((1,), (1,)), ((), ())), preferred_element_type=F32)


def _split3(x):
    hi = x.astype(BF16)
    r1 = x - hi.astype(F32)
    mid = r1.astype(BF16)
    lo = (r1 - mid.astype(F32)).astype(BF16)
    return hi, mid, lo


def _dot_exact_l(m01, x):
    hi, mid, lo = _split3(x)
    return _dot(m01, hi) + _dot(m01, mid) + _dot(m01, lo)


def _dot_exact_r(x, m01):
    hi, mid, lo = _split3(x)
    return _dot(hi, m01) + _dot(mid, m01) + _dot(lo, m01)


def _mod_kernel(c_ref, w_ref, b_ref, o_ref):
    s = jax.nn.silu(c_ref[...]).astype(BF16)
    o_ref[0] = _dot(s, w_ref[0].astype(BF16)) + b_ref[0]


def _mod_call(c_all, w_mod, b_mod):
    rows = c_all.shape[0]
    tn = 1536
    return pl.pallas_call(
        _mod_kernel,
        grid=(DEPTH, 6 * D // tn),
        in_specs=[pl.BlockSpec((rows, D), lambda l, j: (0, 0)),
                  pl.BlockSpec((1, D, tn), lambda l, j: (l, 0, j)),
                  pl.BlockSpec((1, 1, tn), lambda l, j: (l, 0, j))],
        out_specs=pl.BlockSpec((1, rows, tn), lambda l, j: (l, 0, j)),
        out_shape=jax.ShapeDtypeStruct((DEPTH, rows, 6 * D), F32),
        compiler_params=_cparams(("arbitrary", "arbitrary")),
        name="mod",
    )(c_all, w_mod, b_mod.reshape(DEPTH, 1, 6 * D))


def _inproj_kernel(x_ref, mod_ref, w_ref, b_ref, o_ref, on_ref):
    m = mod_ref[0]
    u = (x_ref[0] * (1.0 + m[:, D:2 * D]) + m[:, 0:D]).astype(BF16)
    o_ref[0] = _dot(u, w_ref[:, :PM]) + b_ref[:, :PM]
    on_ref[0] = (_dot(u, w_ref[:, PM:]) + b_ref[:, PM:]).astype(BF16)


def _inproj_call(x, mod3, w_p, b_p, shared_row):
    B, T, _ = x.shape
    tm = min(T, 512)
    if shared_row is None:
        mod_map = lambda b, i: (b, 0, 0)
    else:
        mod_map = lambda b, i: (shared_row, 0, 0)
    return pl.pallas_call(
        _inproj_kernel,
        grid=(B, T // tm),
        in_specs=[pl.BlockSpec((1, tm, D), lambda b, i: (b, i, 0)),
                  pl.BlockSpec((1, 1, 6 * D), mod_map),
                  pl.BlockSpec((D, PW), lambda b, i: (0, 0)),
                  pl.BlockSpec((1, PW), lambda b, i: (0, 0))],
        out_specs=[pl.BlockSpec((1, tm, PM), lambda b, i: (b, i, 0)),
                   pl.BlockSpec((1, tm, PN), lambda b, i: (b, i, 0))],
        out_shape=[jax.ShapeDtypeStruct((B, T, PM), F32), jax.ShapeDtypeStruct((B, T, PN), BF16)],
        compiler_params=_cparams(("arbitrary", "arbitrary")),
        name="inproj",
    )(x, mod3, w_p, b_p)


def _softmax_av(s_list, v_list):
    m = s_list[0].max(-1, keepdims=True)
    for s in s_list[1:]:
        m = jnp.maximum(m, s.max(-1, keepdims=True))
    acc = None
    l = None
    for s, v in zip(s_list, v_list):
        p = jnp.exp(s - m)
        ls = p.sum(-1, keepdims=True)
        o = _dot(p.astype(BF16), v)
        acc = o if acc is None else acc + o
        l = ls if l is None else l + ls
    return acc / l


NAT_GROUP = 8


def _natten_kernel(q_ref, k_ref, v_ref, qc_ref, kc_ref, vc_ref, bias_ref, o_ref, oc_ref, *, n_rows):
    kcb = kc_ref[0]
    vcb = vc_ref[0]
    lane = lax.broadcasted_iota(jnp.int32, (1, LANE), 1)
    head0 = lane < HEAD_DIM
    scale = jnp.asarray(HEAD_DIM ** -0.5, BF16)
    zero = jnp.zeros((), BF16)

    def stack_heads(q):
        return jnp.concatenate([jnp.where(head0, q, zero), jnp.where(head0, zero, q)], 0)

    def unstack(o, n):
        return jnp.where(head0, o[:n], o[n:])

    def rows_body(g, carry):
        koffs, scores = [], []
        for i in range(NAT_GROUP):
            r = g * NAT_GROUP + i
            rs = jnp.clip(r - WIN_ROWS // 2, 0, n_rows - WIN_ROWS)
            qs = stack_heads(q_ref[0, pl.ds(pl.multiple_of(r * GRID_W, GRID_W), GRID_W), :] * scale)
            koff = pl.multiple_of(rs * GRID_W, GRID_W)
            kl = k_ref[0, pl.ds(koff, WIN_ROWS * GRID_W), :]
            koffs.append(koff)
            scores.append([_dot_nt(qs, kl) + bias_ref[r - rs, 0], _dot_nt(qs, kcb)])
        for i in range(NAT_GROUP):
            r = g * NAT_GROUP + i
            vl = v_ref[0, pl.ds(koffs[i], WIN_ROWS * GRID_W), :]
            o = _softmax_av(scores[i], [vl, vcb])
            o_ref[0, pl.ds(pl.multiple_of(r * GRID_W, GRID_W), GRID_W), :] = unstack(o, GRID_W)
        return carry

    lax.fori_loop(0, n_rows // NAT_GROUP, rows_body, 0)

    tc = qc_ref.shape[1]
    oc = _softmax_av([_dot_nt(stack_heads(qc_ref[0] * scale), kcb)], [vcb])
    oc_ref[0] = unstack(oc, tc)


def _natten_call(pn_lat, pn_ctx, bias_l):
    B, T, _ = pn_lat.shape
    Tc = pn_ctx.shape[1]
    n_rows = T // GRID_W
    assert n_rows >= WIN_ROWS
    cb = lambda base: (lambda b, p: (b, 0, base // LANE + p))
    return pl.pallas_call(
        functools.partial(_natten_kernel, n_rows=n_rows),
        grid=(B, 4),
        in_specs=[pl.BlockSpec((1, T, LANE), cb(C_NQ)),
                  pl.BlockSpec((1, T, LANE), cb(C_NK)),
                  pl.BlockSpec((1, T, LANE), cb(C_NV)),
                  pl.BlockSpec((1, Tc, LANE), cb(C_NQ)),
                  pl.BlockSpec((1, Tc, LANE), cb(C_NK)),
                  pl.BlockSpec((1, Tc, LANE), cb(C_NV)),
                  pl.BlockSpec((WIN_ROWS, 1, 2 * GRID_W, WIN_ROWS * GRID_W), lambda b, p: (0, p, 0, 0))],
        out_specs=[pl.BlockSpec((1, T, LANE), lambda b, p: (b, 0, p)),
                   pl.BlockSpec((1, Tc, LANE), lambda b, p: (b, 0, p))],
        out_shape=[jax.ShapeDtypeStruct((B, T, 4 * LANE), F32),
                   jax.ShapeDtypeStruct((B, Tc, 4 * LANE), F32)],
        compiler_params=_cparams(("arbitrary", "arbitrary")),
        name="natten",
    )(pn_lat, pn_lat, pn_lat, pn_ctx, pn_ctx, pn_ctx, bias_l)


def _natten_bias_tables(rpb):
    col = np.arange(GRID_W)
    cstart = np.clip(col - WIN_COLS // 2, 0, GRID_W - WIN_COLS)
    col_ok = (col[None, :] >= cstart[:, None]) & (col[None, :] < cstart[:, None] + WIN_COLS)
    dc_idx = np.clip(col[None, :] - col[:, None] + WIN_COLS - 1, 0, 2 * WIN_COLS - 2)
    toe = rpb.astype(F32)[..., dc_idx]
    toe = jnp.where(col_ok[None, None, None], toe, NEG)
    t = jnp.stack([toe[:, :, WIN_ROWS - 1 - d:2 * WIN_ROWS - 1 - d] for d in range(WIN_ROWS)], 1)
    t = t.transpose(0, 1, 2, 4, 3, 5)
    return t.reshape(DEPTH, WIN_ROWS, 4, 2 * GRID_W, WIN_ROWS * GRID_W)


def _tri_masks():
    r = lax.broadcasted_iota(jnp.int32, (CHUNK, CHUNK), 0)
    c = lax.broadcasted_iota(jnp.int32, (CHUNK, CHUNK), 1)
    return r >= c, r <= c


def _rope(x, cs, sn, first, dist):
    w = x.shape[-1]
    partner = jnp.where(first, pltpu.roll(x, w - dist, 1), pltpu.roll(x, dist, 1))
    return x * cs + partner * sn


def _rope_tables(T, n_heads, half):
    t = jnp.arange(T)
    rows = (t // GRID_W).astype(F32)
    cols = (t % GRID_W).astype(F32)
    inv = ROPE_BASE ** (-jnp.arange(half, dtype=F32) / half)
    ar = rows[:, None] * inv[None, :]
    ac = cols[:, None] * inv[None, :]
    cos = jnp.concatenate([jnp.cos(ar), jnp.cos(ar), jnp.cos(ac), jnp.cos(ac)], -1)
    sin = jnp.concatenate([-jnp.sin(ar), jnp.sin(ar), -jnp.sin(ac), jnp.sin(ac)], -1)
    return jnp.tile(cos, (1, n_heads)), jnp.tile(sin, (1, n_heads))


def _bwd_chunk(c, nc, n):
    return jnp.where(c < nc, nc - 1 - c, n - 1 - c + nc)


def _mlstm_kernel(qkl_ref, vl_ref, sml_ref, qkc_ref, vc_ref, smc_ref, cos_ref, sin_ref, cw_ref,
                  ol_ref, oc_ref,
                  qt_s, k_s, vat_s, gcol_s, grow_s, brow_s, cm_s, ot_s, ck_s, *, nc, n):
    low, upp = _tri_masks()
    tri = jnp.where(low, 1.0, 0.0).astype(BF16)
    triu = jnp.where(upp, 1.0, 0.0).astype(BF16)
    lane128 = lax.broadcasted_iota(jnp.int32, (1, LANE), 1)
    lane256 = lax.broadcasted_iota(jnp.int32, (1, 2 * LANE), 1)
    rowi = lax.broadcasted_iota(jnp.int32, (CHUNK, 1), 0)
    row16 = lax.broadcasted_iota(jnp.int32, (16, 1), 0)
    first = (lane256 % 32) < 16
    hmask = [(lane256 >= HEAD_DIM * h) & (lane256 < HEAD_DIM * (h + 1)) for h in range(4)]
    ones_blk = jnp.where(lax.broadcasted_iota(jnp.int32, (HEAD_DIM, CHUNK), 0) == 0, 1.0, 0.0)
    fwd_rows = row16 < 8
    cw = cw_ref[...]

    def cummax_lanes(x, suffix):
        sh = 1
        while sh < CHUNK:
            if suffix:
                moved = jnp.where(lane256 < CHUNK - sh, pltpu.roll(x, CHUNK - sh, 1), NEG)
            else:
                moved = jnp.where(lane256 >= sh, pltpu.roll(x, sh, 1), NEG)
            x = jnp.maximum(x, moved)
            sh *= 2
        return x

    def prep(qk_ref, v_ref, sm_ref, ci, n_str, dst, use_rope):
        r0 = ci * CHUNK
        xc = qk_ref[0, r0:r0 + CHUNK, :]
        prev = qk_ref[0, r0 - 1:r0, :] if ci > 0 else jnp.zeros((1, 4 * LANE), F32)
        nxt = qk_ref[0, r0 + CHUNK:r0 + CHUNK + 1, :] if ci < n_str - 1 else jnp.zeros((1, 4 * LANE), F32)
        xp = jnp.where(rowi == 0, prev, pltpu.roll(xc, 1, 0))
        xn = jnp.where(rowi == CHUNK - 1, nxt, pltpu.roll(xc, CHUNK - 1, 0))
        y = jax.nn.silu(xp * cw[0:1] + xc * cw[1:2] + xn * cw[2:3])
        q = y[:, :2 * LANE]
        k = y[:, 2 * LANE:] * HEAD_DIM ** -0.5
        if use_rope:
            cs = cos_ref[r0:r0 + CHUNK, :]
            sn = sin_ref[r0:r0 + CHUNK, :]
            q = _rope(q, cs, sn, first, 16)
            k = _rope(k, cs, sn, first, 16)
        qt_s[dst] = q.T.astype(BF16)
        k_s[dst] = k.astype(BF16)
        vt = v_ref[0, r0:r0 + CHUNK, :].T
        for h in range(4):
            vat_s[h, dst] = jnp.concatenate([vt[HEAD_DIM * h:HEAD_DIM * (h + 1)], ones_blk], 0).astype(BF16)
        g = sm_ref[0, r0:r0 + CHUNK, :]
        lf = pltpu.roll(jax.nn.log_sigmoid(g), LANE - 4, 1)
        gcol_s[dst] = g - jnp.where(lane128 < 8, _dot_exact_l(tri, lf), _dot_exact_l(triu, lf))
        gt = g.T[0:16]
        lft = pltpu.roll(jax.nn.log_sigmoid(gt), 12, 0)
        brow = jnp.where(fwd_rows, _dot_exact_r(lft, triu), _dot_exact_r(lft, tri))
        brow_s[dst] = brow
        grow_s[dst] = gt - brow

    for ci in range(nc):
        prep(qkc_ref, vc_ref, smc_ref, ci, nc, ci, False)
    for ci in range(n - nc):
        prep(qkl_ref, vl_ref, sml_ref, ci, n - nc, nc + ci, True)
    g_all = grow_s[...].reshape(n * 16, CHUNK)
    fwd_all = (lax.broadcasted_iota(jnp.int32, (n * 16, 1), 0) % 16) < 8
    cm_s[...] = jnp.where(fwd_all, cummax_lanes(g_all, False), cummax_lanes(g_all, True)).reshape(n, 16, CHUNK)

    def scan_dir(bwd):
        di = 1 if bwd else 0
        gi = 8 if bwd else 0
        causal = low if bwd else upp

        def matmuls_first(c):
            ch = _bwd_chunk(c, nc, n) if bwd else c
            qt = qt_s[ch]
            kb = k_s[ch]
            ck_b = ck_s[di].astype(BF16)
            scores, inters = [], []
            for h in range(4):
                pr = slice(LANE * (h // 2), LANE * (h // 2 + 1))
                own = (lane128 < HEAD_DIM) if h % 2 == 0 else (lane128 >= HEAD_DIM)
                zero = jnp.zeros((), BF16)
                scores.append(_dot(jnp.where(own, kb[:, pr], zero), qt[pr, :]))
                inters.append(_dot(jnp.where(own, ck_b[:, pr], zero), qt[pr, :]))
            return ch, kb, scores, inters

        def rest(first, ms):
            ch, kb, scores, inters = first
            gcol_all = gcol_s[ch]
            new_ms = []
            for h in range(4):
                m = ms[h]
                g_row = grow_s[ch, gi + h:gi + h + 1, :]
                b_row = brow_s[ch, gi + h:gi + h + 1, :]
                a_row = jnp.maximum(m, cm_s[ch, gi + h:gi + h + 1, :])
                w = jnp.exp(jnp.where(causal, gcol_all[:, gi + h:gi + h + 1] - a_row, NEG))
                pt = (scores[h] * w).astype(BF16)
                vat = vat_s[h, ch]
                nd = _dot(vat, pt) + jnp.exp(m - a_row) * inters[h]
                den = nd[HEAD_DIM:HEAD_DIM + 1, :]
                ht = nd[0:HEAD_DIM] / jnp.maximum(jnp.abs(den), jnp.exp(-(b_row + a_row)))
                ot_s[di, ch, HEAD_DIM * h:HEAD_DIM * (h + 1), :] = ht
                bl = b_row[:, 0:1] if bwd else b_row[:, CHUNK - 1:CHUNK]
                lw_end = bl + g_row
                m_new = jnp.maximum(bl + m, lw_end.max(-1, keepdims=True))
                upd = _dot((vat * jnp.exp(lw_end - m_new)).astype(BF16), kb)
                ck_s[di] = jnp.where(hmask[h], jnp.exp(bl + m - m_new) * ck_s[di] + upd, ck_s[di])
                new_ms.append(m_new)
            return tuple(new_ms)

        return matmuls_first, rest

    ck_s[...] = jnp.zeros_like(ck_s)
    first_f, rest_f = scan_dir(False)
    first_b, rest_b = scan_dir(True)

    def chunk_body(c, carry):
        ff, fb = first_f(c), first_b(c)
        return rest_f(ff, carry[0]), rest_b(fb, carry[1])

    zeros4 = tuple(jnp.zeros((1, 1), F32) for _ in range(4))
    lax.fori_loop(0, n, chunk_body, (zeros4, zeros4))
    for ci in range(nc):
        oc_ref[0, ci * CHUNK:(ci + 1) * CHUNK, :] = (ot_s[0, ci] + ot_s[1, ci]).T
    for ci in range(n - nc):
        ol_ref[0, ci * CHUNK:(ci + 1) * CHUNK, :] = (ot_s[0, nc + ci] + ot_s[1, nc + ci]).T


def _mlstm_call(p_lat, p_ctx, cos, sin, conv_w_l):
    B, T, _ = p_lat.shape
    Tc = p_ctx.shape[1]
    nc, n = Tc // CHUNK, (Tc + T) // CHUNK
    cb = lambda base, w: (lambda b: (b, 0, base // w))
    return pl.pallas_call(
        functools.partial(_mlstm_kernel, nc=nc, n=n),
        grid=(B,),
        in_specs=[pl.BlockSpec((1, T, 512), cb(C_MQK, 512)),
                  pl.BlockSpec((1, T, 256), cb(C_MV, 256)),
                  pl.BlockSpec((1, T, LANE), cb(C_SM, LANE)),
                  pl.BlockSpec((1, Tc, 512), cb(C_MQK, 512)),
                  pl.BlockSpec((1, Tc, 256), cb(C_MV, 256)),
                  pl.BlockSpec((1, Tc, LANE), cb(C_SM, LANE)),
                  pl.BlockSpec((T, 256), lambda b: (0, 0)),
                  pl.BlockSpec((T, 256), lambda b: (0, 0)),
                  pl.BlockSpec((CONV_K, 512), lambda b: (0, 0))],
        out_specs=[pl.BlockSpec((1, T, 256), lambda b: (b, 0, 0)),
                   pl.BlockSpec((1, Tc, 256), lambda b: (b, 0, 0))],
        out_shape=[jax.ShapeDtypeStruct((B, T, 256), F32),
                   jax.ShapeDtypeStruct((B, Tc, 256), F32)],
        scratch_shapes=[pltpu.VMEM((n, 256, CHUNK), BF16),
                        pltpu.VMEM((n, CHUNK, 256), BF16),
                        pltpu.VMEM((4, n, LANE, CHUNK), BF16),
                        pltpu.VMEM((n, CHUNK, LANE), F32),
                        pltpu.VMEM((n, 16, CHUNK), F32),
                        pltpu.VMEM((n, 16, CHUNK), F32),
                        pltpu.VMEM((n, 16, CHUNK), F32),
                        pltpu.VMEM((2, n, 256, CHUNK), F32),
                        pltpu.VMEM((2, LANE, 256), F32)],
        compiler_params=_cparams(("arbitrary",)),
        name="mlstm",
    )(p_lat, p_lat, p_lat, p_ctx, p_ctx, p_ctx, cos, sin, conv_w_l)


GLA_BLK = 64
GLA_CLAMP = 80.0


def _gla_kernel(ql_ref, kl_ref, vl_ref, sml_ref, qc_ref, kc_ref, vc_ref, smc_ref, cos_ref, sin_ref, w2_ref, b2_ref,
                ol_ref, oc_ref,
                q_s, k_s, v_s, a_s, b_s, o_s, st_s, *, nc, n):
    low, upp = _tri_masks()
    tri = jnp.where(low, 1.0, 0.0).astype(BF16)
    triu = jnp.where(upp, 1.0, 0.0).astype(BF16)
    lane128 = lax.broadcasted_iota(jnp.int32, (1, LANE), 1)
    lane256 = lax.broadcasted_iota(jnp.int32, (1, 2 * LANE), 1)
    first = (lane128 % 16) < 8
    hm128 = [(lane128 >= 32 * h) & (lane128 < 32 * (h + 1)) for h in range(4)]
    hm256 = [(lane256 >= 64 * h) & (lane256 < 64 * (h + 1)) for h in range(4)]
    nb = CHUNK // GLA_BLK
    r_st = lax.broadcasted_iota(jnp.int32, (nb * GLA_BLK, CHUNK), 0)
    c_st = lax.broadcasted_iota(jnp.int32, (nb * GLA_BLK, CHUNK), 1)
    bd_r = lax.broadcasted_iota(jnp.int32, (LANE, 2 * LANE), 0)
    bd_c = lax.broadcasted_iota(jnp.int32, (LANE, 2 * LANE), 1)
    blockdiag = (bd_r // 32) == (bd_c // 64)

    def prep(q_ref, k_ref, v_ref, sm_ref, ci, dst, use_rope):
        r0 = ci * CHUNK
        q = q_ref[0, r0:r0 + CHUNK, :] * 32 ** -0.5
        k = k_ref[0, r0:r0 + CHUNK, :]
        if use_rope:
            cs = cos_ref[r0:r0 + CHUNK, :]
            sn = sin_ref[r0:r0 + CHUNK, :]
            q = _rope(q, cs, sn, first, 8)
            k = _rope(k, cs, sn, first, 8)
        q_s[dst] = q
        k_s[dst] = k
        v_s[dst] = v_ref[0, r0:r0 + CHUNK, :].astype(BF16)
        lr = sm_ref[0, r0:r0 + CHUNK, :].astype(BF16)
        for d in range(2):
            a = jax.nn.log_sigmoid(_dot(lr, w2_ref[d]) + b2_ref[d]) / GLA_TAU
            a_s[d, dst] = a
            b_s[d, dst] = _dot_exact_l(triu if d else tri, a)

    for ci in range(nc):
        prep(qc_ref, kc_ref, vc_ref, smc_ref, ci, ci, False)
    for ci in range(n - nc):
        prep(ql_ref, kl_ref, vl_ref, sml_ref, ci, nc + ci, True)

    def scan_dir(bwd):
        d = 1 if bwd else 0
        st_s[...] = jnp.zeros_like(st_s)

        def chunk_body(c, carry):
            ch = _bwd_chunk(c, nc, n) if bwd else c
            q = q_s[ch]
            k = k_s[ch]
            vb = v_s[ch]
            a = a_s[d, ch]
            b = b_s[d, ch]
            st_b = st_s[...].astype(BF16)
            o_inter = _dot((q * jnp.exp(b)).astype(BF16), st_b)
            atts = []
            for i in range(nb):
                rows = slice(i * GLA_BLK, (i + 1) * GLA_BLK)
                e = (i + 1) * GLA_BLK - 1 if bwd else i * GLA_BLK
                ref = b[e:e + 1, :] - a[e:e + 1, :]
                qs = q[rows] * jnp.exp(b[rows] - ref)
                ks = (k * jnp.exp(jnp.minimum(ref - b, GLA_CLAMP))).astype(BF16)
                lhs = jnp.concatenate([jnp.where(hm128[h], qs, 0.0) for h in range(4)], 0).astype(BF16)
                att = _dot_nt(lhs, ks)
                t_idx = (r_st % GLA_BLK) + i * GLA_BLK
                ok = (c_st >= t_idx) if bwd else (c_st <= t_idx)
                atts.append(jnp.where(ok, att, 0.0).astype(BF16))
            for i in range(nb):
                rows = slice(i * GLA_BLK, (i + 1) * GLA_BLK)
                oh = _dot(atts[i], vb)
                o_blk = o_inter[rows]
                for h in range(4):
                    o_blk = o_blk + jnp.where(hm256[h], oh[h * GLA_BLK:(h + 1) * GLA_BLK], 0.0)
                if bwd:
                    o_s[ch, rows, :] = o_s[ch, rows, :] + o_blk
                else:
                    o_s[ch, rows, :] = o_blk
            bt = b.T
            tot = bt[:, 0:1] if bwd else bt[:, CHUNK - 1:CHUNK]
            kt = (k.T * jnp.exp(tot - bt)).astype(BF16)
            upd = jnp.where(blockdiag, _dot(kt, vb), 0.0)
            st_s[...] = jnp.exp(tot) * st_s[...] + upd
            return carry

        lax.fori_loop(0, n, chunk_body, 0)

    scan_dir(False)
    scan_dir(True)
    for ci in range(nc):
        oc_ref[0, ci * CHUNK:(ci + 1) * CHUNK, :] = o_s[ci]
    for ci in range(n - nc):
        ol_ref[0, ci * CHUNK:(ci + 1) * CHUNK, :] = o_s[nc + ci]


def _gla_call(p_lat, p_ctx, cos, sin, w2p, b2p):
    B, T, _ = p_lat.shape
    Tc = p_ctx.shape[1]
    nc, n = Tc // CHUNK, (Tc + T) // CHUNK
    cb = lambda base, w: (lambda b: (b, 0, base // w))
    return pl.pallas_call(
        functools.partial(_gla_kernel, nc=nc, n=n),
        grid=(B,),
        in_specs=[pl.BlockSpec((1, T, LANE), cb(C_GQ, LANE)),
                  pl.BlockSpec((1, T, LANE), cb(C_GK, LANE)),
                  pl.BlockSpec((1, T, 256), cb(C_GV, 256)),
                  pl.BlockSpec((1, T, LANE), cb(C_SM, LANE)),
                  pl.BlockSpec((1, Tc, LANE), cb(C_GQ, LANE)),
                  pl.BlockSpec((1, Tc, LANE), cb(C_GK, LANE)),
                  pl.BlockSpec((1, Tc, 256), cb(C_GV, 256)),
                  pl.BlockSpec((1, Tc, LANE), cb(C_SM, LANE)),
                  pl.BlockSpec((T, LANE), lambda b: (0, 0)),
                  pl.BlockSpec((T, LANE), lambda b: (0, 0)),
                  pl.BlockSpec((2, LANE, LANE), lambda b: (0, 0, 0)),
                  pl.BlockSpec((2, 1, LANE), lambda b: (0, 0, 0))],
        out_specs=[pl.BlockSpec((1, T, 256), lambda b: (b, 0, 0)),
                   pl.BlockSpec((1, Tc, 256), lambda b: (b, 0, 0))],
        out_shape=[jax.ShapeDtypeStruct((B, T, 256), F32),
                   jax.ShapeDtypeStruct((B, Tc, 256), F32)],
        scratch_shapes=[pltpu.VMEM((n, CHUNK, LANE), F32),
                        pltpu.VMEM((n, CHUNK, LANE), F32),
                        pltpu.VMEM((n, CHUNK, 256), BF16),
                        pltpu.VMEM((2, n, CHUNK, LANE), F32),
                        pltpu.VMEM((2, n, CHUNK, LANE), F32),
                        pltpu.VMEM((n, CHUNK, 256), F32),
                        pltpu.VMEM((LANE, 256), F32)],
        compiler_params=_cparams(("arbitrary",)),
        name="gla",
    )(p_lat, p_lat, p_lat, p_lat, p_ctx, p_ctx, p_ctx, p_ctx, cos, sin, w2p, b2p)


def _gla_gate_weights(gla_w2, gla_b2):
    w = jnp.zeros((DEPTH, 2, LANE, LANE), F32)
    w = w.at[:, 0, 16:32].set(gla_w2[:, 0]).at[:, 1, 32:48].set(gla_w2[:, 1])
    return w.astype(BF16), gla_b2.astype(F32).reshape(DEPTH, 2, 1, LANE)


TILE_ROWS = D // LANE


def _store_token_tiles(ref, val):
    tm = val.shape[0]
    for s in range(TILE_ROWS):
        ref[0, pl.ds(s, tm, stride=TILE_ROWS), :] = val[:, s * LANE:(s + 1) * LANE]


def _load_token_tiles(ref, lead, t0, tm):
    return [ref[lead + (pl.ds(t0 * TILE_ROWS + s, tm, stride=TILE_ROWS), slice(None))] for s in range(TILE_ROWS)]


def _layer_norm(z, g, b):
    mu = z.mean(-1, keepdims=True)
    zc = z - mu
    var = jnp.mean(jnp.square(zc), -1, keepdims=True)
    return zc * lax.rsqrt(var + LN_EPS) * g + b


def _merge_kernel(hm_ref, hg_ref, hn_ref, mo_ref, gr_ref, x_ref, mod_ref, wout_ref, wr_ref, ng_ref, ln_ref,
                  x1_ref, lg_ref):
    r = lax.broadcasted_iota(jnp.int32, (256, 256), 0)
    c = lax.broadcasted_iota(jnp.int32, (256, 256), 1)
    avg = jnp.where((r // HEAD_DIM) == (c // HEAD_DIM), 1.0 / HEAD_DIM, 0.0).astype(BF16)

    def seg_mean(x):
        hi = x.astype(BF16)
        lo = (x - hi.astype(F32)).astype(BF16)
        return _dot(hi, avg) + _dot(lo, avg)

    def head_norm(h):
        d = h - seg_mean(h)
        return d * lax.rsqrt(seg_mean(d * d) + LN_EPS)

    ym = head_norm(hm_ref[0]) * ng_ref[0:1, :] * jax.nn.sigmoid(mo_ref[0])
    yg = head_norm(hg_ref[0]) * ng_ref[1:2, :] * jax.nn.silu(gr_ref[0])
    y = (_dot(ym.astype(BF16), wout_ref[0:256, :]) + _dot(yg.astype(BF16), wout_ref[256:512, :])
         + _dot(hn_ref[0].astype(BF16), wout_ref[512:1024, :]))
    m = mod_ref[0]
    x1 = _layer_norm(DEEPNORM_ALPHA * x_ref[0] + m[:, 2 * D:3 * D] * y, ln_ref[0:1, :], ln_ref[1:2, :])
    _store_token_tiles(x1_ref, x1)
    u2 = x1 * (1.0 + m[:, 4 * D:5 * D]) + m[:, 3 * D:4 * D]
    lg_ref[0] = _dot_nt(wr_ref[...], u2.astype(BF16))


def _merge_call(hm, hg, hn, p, x, mod3, wout_b, wrt_b, ng, ln, shared_row):
    B, T, _ = x.shape
    tm = min(T, 512)
    mod_map = (lambda b, i: (b, 0, 0)) if shared_row is None else (lambda b, i: (shared_row, 0, 0))
    tok = lambda w: pl.BlockSpec((1, tm, w), lambda b, i: (b, i, 0))
    return pl.pallas_call(
        _merge_kernel,
        grid=(B, T // tm),
        in_specs=[tok(256), tok(256), tok(512),
                  pl.BlockSpec((1, tm, 256), lambda b, i: (b, i, C_MO // 256)),
                  pl.BlockSpec((1, tm, 256), lambda b, i: (b, i, C_GR // 256)),
                  tok(D),
                  pl.BlockSpec((1, 1, 6 * D), mod_map),
                  pl.BlockSpec((D, D), lambda b, i: (0, 0)),
                  pl.BlockSpec((N_EXPERTS, D), lambda b, i: (0, 0)),
                  pl.BlockSpec((2, 256), lambda b, i: (0, 0)),
                  pl.BlockSpec((2, D), lambda b, i: (0, 0))],
        out_specs=[pl.BlockSpec((1, tm * TILE_ROWS, LANE), lambda b, i: (b, i, 0)),
                   pl.BlockSpec((1, N_EXPERTS, tm), lambda b, i: (b, 0, i))],
        out_shape=[jax.ShapeDtypeStruct((B, T * TILE_ROWS, LANE), F32),
                   jax.ShapeDtypeStruct((B, N_EXPERTS, T), F32)],
        compiler_params=_cparams(("arbitrary", "arbitrary")),
        name="merge",
    )(hm, hg, hn, p, p, x, mod3, wout_b, wrt_b, ng, ln)


ROUTER_SAMPLES = 8


def _router_samples(B):
    ns = ROUTER_SAMPLES
    while B % ns:
        ns -= 1
    return ns


def _router_kernel(lg_ref, o_ref, aff_s, sp_s, *, T, cap):
    J = T // LANE
    E = N_EXPERTS
    NS = lg_ref.shape[0]
    keys = [[] for _ in range(NS)]
    for j in range(J):
        for si in range(NS):
            lg = lg_ref[si, :, j * LANE:(j + 1) * LANE]
            ex = jnp.exp(lg - lg.max(0, keepdims=True))
            aff = ex / ex.sum(0, keepdims=True)
            aff_s[si, j * E:(j + 1) * E, :] = aff
            keys[si].append(pltpu.bitcast(aff, jnp.int32))

    def count(ks, pred):
        cnt = None
        for k in ks:
            cj = jnp.where(pred(k), 1.0, 0.0)
            cnt = cj if cnt is None else cnt + cj
        return cnt.sum(-1, keepdims=True)

    thrs = [jnp.zeros((E, 1), jnp.int32) for _ in range(NS)]
    for bit in range(30, -1, -1):
        for si in range(NS):
            cand = thrs[si] | (1 << bit)
            thrs[si] = jnp.where(count(keys[si], lambda k: k >= cand) >= cap, cand, thrs[si])

    r = lax.broadcasted_iota(jnp.int32, (LANE, LANE), 0)
    c = lax.broadcasted_iota(jnp.int32, (LANE, LANE), 1)
    upper = jnp.where(r <= c, 1.0, 0.0).astype(BF16)
    ones = jnp.ones((LANE, LANE), BF16)
    rr = lax.broadcasted_iota(jnp.int32, (J * E, J * E), 0)
    cc = lax.broadcasted_iota(jnp.int32, (J * E, J * E), 1)
    earlier = jnp.where(((rr % E) == (cc % E)) & ((cc // E) < (rr // E)), 1.0, 0.0).astype(BF16)

    def prefix(x01):
        xb = x01.astype(BF16)
        return _dot(xb, upper) + _dot(earlier, _dot(xb, ones).astype(BF16))

    for si in range(NS):
        thr = thrs[si]
        need = cap - count(keys[si], lambda k: k > thr)
        gt = jnp.concatenate([jnp.where(k > thr, 1.0, 0.0) for k in keys[si]], 0)
        eq = jnp.concatenate([jnp.where(k == thr, 1.0, 0.0) for k in keys[si]], 0)
        need_t = jnp.concatenate([need] * J, 0)
        sel = jnp.maximum(gt, jnp.where(prefix(eq) <= need_t, eq, 0.0))
        sp_s[si] = jnp.where(sel > 0.0, prefix(sel) - 1.0, -1.0)

    sb = min(cap, LANE)
    lane = lax.broadcasted_iota(jnp.int32, (1, LANE), 1)
    o_ref[...] = jnp.zeros_like(o_ref)
    for e in range(E):
        for half in range(cap // sb):
            slot = (lax.broadcasted_iota(jnp.int32, (sb, LANE), 0) + half * sb).astype(F32)
            rows = slice(half * sb, (half + 1) * sb)
            for si in range(NS):
                def jbody(j, acc):
                    acc_i, acc_g = acc
                    sp = sp_s[si, pl.ds(j * E + e, 1), :]
                    af = aff_s[si, pl.ds(j * E + e, 1), :]
                    hit = sp == slot
                    tid = (lane + j * LANE).astype(F32)
                    return jnp.where(hit, tid, acc_i), jnp.where(hit, af, acc_g)

                acc_i, acc_g = lax.fori_loop(0, J, jbody, (jnp.zeros((sb, LANE), F32), jnp.zeros((sb, LANE), F32)),
                                             unroll=2)
                icol = acc_i.sum(-1, keepdims=True)
                gcol = acc_g.sum(-1, keepdims=True)
                o_ref[si, rows, :] = jnp.where(lane == e, icol, jnp.where(lane == E + e, gcol, o_ref[si, rows, :]))


def _router_call(logits):
    B, _, T = logits.shape
    cap = CAPACITY_FACTOR * T // N_EXPERTS
    J = T // LANE
    ns = _router_samples(B)
    out = pl.pallas_call(
        functools.partial(_router_kernel, T=T, cap=cap),
        grid=(B // ns,),
        in_specs=[pl.BlockSpec((ns, N_EXPERTS, T), lambda b: (b, 0, 0))],
        out_specs=pl.BlockSpec((ns, cap, LANE), lambda b: (b, 0, 0)),
        out_shape=jax.ShapeDtypeStruct((B, cap, LANE), F32),
        scratch_shapes=[pltpu.VMEM((ns, J * N_EXPERTS, LANE), F32), pltpu.VMEM((ns, J * N_EXPERTS, LANE), F32)],
        compiler_params=_cparams(("arbitrary",)),
        name="router",
    )(logits)
    idx = (out[:, :, :N_EXPERTS].astype(jnp.int32) * TILE_ROWS).transpose(0, 2, 1).reshape(B, 1, N_EXPERTS * cap)
    gate = out[:, :, N_EXPERTS:2 * N_EXPERTS].transpose(0, 2, 1).reshape(B, 1, N_EXPERTS * cap)
    return idx, gate


GATHER_BLOCK_BYTES = 4 * 1024 * 1024
SCATTER_BLOCK_BYTES = 4 * 1024 * 1024


def _experts_per_step(cap, block_bytes):
    eg = 2
    while eg < N_EXPERTS and 2 * eg * cap * D * 4 <= block_bytes:
        eg *= 2
    return eg


def _as_token_tile(v):
    return jnp.concatenate([v[:, s * LANE:(s + 1) * LANE] for s in range(TILE_ROWS)], 0)


def _gather_kernel(idx_ref, x_ref, mod_ref, o_ref, *, cap, eg):
    g = pl.program_id(1)
    m = mod_ref[0]
    scale = _as_token_tile(1.0 + m[:, 4 * D:5 * D])
    shift = _as_token_tile(m[:, 3 * D:4 * D])
    half = cap // 2

    def row_bits(r):
        src = pl.ds(pl.multiple_of(r, TILE_ROWS), TILE_ROWS)
        u = (x_ref[0, src, :] * scale + shift).astype(BF16).astype(F32)
        return pltpu.bitcast(u, jnp.uint32)

    for e in range(eg):
        def body(s, carry):
            base = (g * eg + e) * cap + s
            lo = row_bits(idx_ref[0, 0, base]) >> 16
            hi = row_bits(idx_ref[0, 0, base + half]) & jnp.uint32(0xFFFF0000)
            o_ref[e, 0, pl.ds(pl.multiple_of(s * TILE_ROWS, TILE_ROWS), TILE_ROWS), :] = lo | hi
            return carry
        lax.fori_loop(0, half, body, 0, unroll=8)


def _gather_call(idx, x1t, mod3, shared_row):
    B = x1t.shape[0]
    T = x1t.shape[1] // TILE_ROWS
    cap = CAPACITY_FACTOR * T // N_EXPERTS
    eg = _experts_per_step(cap, GATHER_BLOCK_BYTES)
    mod_map = (lambda b, g: (b, 0, 0)) if shared_row is None else (lambda b, g: (shared_row, 0, 0))
    smem = lambda: pl.BlockSpec((1, 1, N_EXPERTS * cap), lambda b, g: (b, 0, 0), memory_space=pltpu.SMEM)
    return pl.pallas_call(
        functools.partial(_gather_kernel, cap=cap, eg=eg),
        grid=(B, N_EXPERTS // eg),
        in_specs=[smem(), pl.BlockSpec((1, T * TILE_ROWS, LANE), lambda b, g: (b, 0, 0)),
                  pl.BlockSpec((1, 1, 6 * D), mod_map)],
        out_specs=pl.BlockSpec((eg, 1, cap // 2 * TILE_ROWS, LANE), lambda b, g: (g, b, 0, 0)),
        out_shape=jax.ShapeDtypeStruct((N_EXPERTS, B, cap // 2 * TILE_ROWS, LANE), jnp.uint32),
        compiler_params=_cparams(("arbitrary", "arbitrary")),
        name="gather",
    )(idx, x1t, mod3)


FF_CHUNK = 512


def _ffn_kernel(x_ref, wg_ref, wu_ref, wd_ref, o_ref, xs, *, tm, half):
    for s, words in enumerate(_load_token_tiles(x_ref, (0,), 0, tm // 2)):
        lo = pltpu.bitcast(words << 16, F32).astype(BF16)
        hi = pltpu.bitcast(words & jnp.uint32(0xFFFF0000), F32).astype(BF16)
        for blk in range(tm // (2 * half)):
            src = slice(blk * half, (blk + 1) * half)
            xs[2 * blk * half:(2 * blk + 1) * half, s * LANE:(s + 1) * LANE] = lo[src]
            xs[(2 * blk + 1) * half:(2 * blk + 2) * half, s * LANE:(s + 1) * LANE] = hi[src]
    xb = xs[...]
    acc = None
    for c in range(EXPERT_FF // FF_CHUNK):
        cols = slice(c * FF_CHUNK, (c + 1) * FF_CHUNK)
        h = (jax.nn.silu(_dot(xb, wg_ref[0, :, cols])) * _dot(xb, wu_ref[0, :, cols])).astype(BF16)
        t = _dot(h, wd_ref[0, cols, :])
        acc = t if acc is None else acc + t
    _store_token_tiles(o_ref, acc)


def _ffn_call(xe, wg_b, wu_b, wd_b, layer, cap):
    E = xe.shape[0]
    M = 2 * xe.shape[1] // TILE_ROWS
    tm = min(M, 512)
    assert tm % cap == 0
    w_map = lambda e, i: (layer * E + e, 0, 0)
    return pl.pallas_call(
        functools.partial(_ffn_kernel, tm=tm, half=cap // 2),
        grid=(E, M // tm),
        in_specs=[pl.BlockSpec((1, tm // 2 * TILE_ROWS, LANE), lambda e, i: (e, i, 0)),
                  pl.BlockSpec((1, D, EXPERT_FF), w_map),
                  pl.BlockSpec((1, D, EXPERT_FF), w_map),
                  pl.BlockSpec((1, EXPERT_FF, D), w_map)],
        out_specs=pl.BlockSpec((1, tm * TILE_ROWS, LANE), lambda e, i: (e, i, 0)),
        out_shape=jax.ShapeDtypeStruct((E, M * TILE_ROWS, LANE), F32),
        scratch_shapes=[pltpu.VMEM((tm, D), BF16)],
        compiler_params=_cparams(("arbitrary", "arbitrary")),
        name="ffn",
    )(xe, wg_b, wu_b, wd_b)


SCATTER_FIN_ROWS = 256
SCATTER_BATCH = 8


def _scatter_kernel(idx_ref, gate_ref, y_ref, x1_ref, mod_ref, ln_ref, o_ref, acc, *, cap, eg, n_g, T):
    g = pl.program_id(1)

    @pl.when(g == 0)
    def _():
        acc[...] = jnp.zeros_like(acc)

    for e in range(eg):
        def body(i, carry):
            s0 = i * SCATTER_BATCH
            p0 = (g * eg + e) * cap + s0
            dst = [pl.ds(pl.multiple_of(idx_ref[0, 0, p0 + k], TILE_ROWS), TILE_ROWS) for k in range(SCATTER_BATCH)]
            old = [acc[d, :] for d in dst]
            for k in range(SCATTER_BATCH):
                src = pl.ds(pl.multiple_of((s0 + k) * TILE_ROWS, TILE_ROWS), TILE_ROWS)
                acc[dst[k], :] = old[k] + y_ref[e, 0, src, :] * gate_ref[0, 0, p0 + k]
            return carry
        lax.fori_loop(0, cap // SCATTER_BATCH, body, 0, unroll=2)

    @pl.when(g == n_g - 1)
    def _():
        g2 = mod_ref[0][:, 5 * D:6 * D]
        tb = min(T, SCATTER_FIN_ROWS)
        for i in range(T // tb):
            rows = slice(i * tb, (i + 1) * tb)
            f = jnp.concatenate(_load_token_tiles(acc, (), i * tb, tb), -1)
            x1 = jnp.concatenate(_load_token_tiles(x1_ref, (0,), i * tb, tb), -1)
            z = DEEPNORM_ALPHA * x1 + g2 * f
            o_ref[0, rows, :] = _layer_norm(z, ln_ref[0:1, :], ln_ref[1:2, :])


def _scatter_call(idx, gate, y4, x1t, mod3, ln, shared_row):
    B = x1t.shape[0]
    T = x1t.shape[1] // TILE_ROWS
    cap = CAPACITY_FACTOR * T // N_EXPERTS
    eg = _experts_per_step(cap, SCATTER_BLOCK_BYTES)
    n_g = N_EXPERTS // eg
    mod_map = (lambda b, g: (b, 0, 0)) if shared_row is None else (lambda b, g: (shared_row, 0, 0))
    smem = lambda: pl.BlockSpec((1, 1, N_EXPERTS * cap), lambda b, g: (b, 0, 0), memory_space=pltpu.SMEM)
    return pl.pallas_call(
        functools.partial(_scatter_kernel, cap=cap, eg=eg, n_g=n_g, T=T),
        grid=(B, n_g),
        scratch_shapes=[pltpu.VMEM((T * TILE_ROWS, LANE), F32)],
        in_specs=[smem(), smem(),
                  pl.BlockSpec((eg, 1, cap * TILE_ROWS, LANE), lambda b, g: (g, b, 0, 0)),
                  pl.BlockSpec((1, T * TILE_ROWS, LANE), lambda b, g: (b, 0, 0)),
                  pl.BlockSpec((1, 1, 6 * D), mod_map),
                  pl.BlockSpec((2, D), lambda b, g: (0, 0))],
        out_specs=pl.BlockSpec((1, T, D), lambda b, g: (b, 0, 0)),
        out_shape=jax.ShapeDtypeStruct((B, T, D), F32),
        compiler_params=_cparams(("arbitrary", "arbitrary")),
        name="scatter",
    )(idx, gate, y4, x1t, mod3, ln)


def _moe(x1t, logits, mod3, wg_b, wu_b, wd_b, layer, ln2, shared_row):
    B = x1t.shape[0]
    T = x1t.shape[1] // TILE_ROWS
    cap = CAPACITY_FACTOR * T // N_EXPERTS
    idx, gate = _router_call(logits)
    xe = _gather_call(idx, x1t, mod3, shared_row)
    y = _ffn_call(xe.reshape(N_EXPERTS, B * (cap // 2) * TILE_ROWS, LANE), wg_b, wu_b, wd_b, layer, cap)
    return _scatter_call(idx, gate, y.reshape(N_EXPERTS, B, cap * TILE_ROWS, LANE), x1t, mod3, ln2, shared_row)


def kernel(x, c, ctx, c_ctx, w_mod, b_mod, w_in, b_in, conv_w, gla_w2, gla_b2, mlstm_norm_g, gla_norm_g, rpb, w_out, ln1_g, ln1_b, w_router, w_gate, w_up, w_down, ln2_g, ln2_b):
    B, T, _ = x.shape
    n_mod = -(-(B + 1) // 8) * 8
    c_all = jnp.concatenate([c, c_ctx[None], jnp.zeros((n_mod - B - 1, D), F32)], 0)
    mods = _mod_call(c_all, w_mod, b_mod)
    w_p = _repack_columns(w_in).astype(BF16)
    b_p = _repack_columns(b_in)
    bias = _natten_bias_tables(rpb)
    mcos, msin = _rope_tables(T, 4, 16)
    gcos, gsin = _rope_tables(T, 4, 8)
    w2p, b2p = _gla_gate_weights(gla_w2, gla_b2)
    wout_b = w_out.astype(BF16)
    wrt_b = jnp.swapaxes(w_router, 1, 2).astype(BF16)
    wg_b = w_gate.astype(BF16).reshape(DEPTH * N_EXPERTS, D, EXPERT_FF)
    wu_b = w_up.astype(BF16).reshape(DEPTH * N_EXPERTS, D, EXPERT_FF)
    wd_b = w_down.astype(BF16).reshape(DEPTH * N_EXPERTS, EXPERT_FF, D)
    ng = jnp.stack([mlstm_norm_g, gla_norm_g], 1)
    ln1 = jnp.stack([ln1_g, ln1_b], 1)
    ln2 = jnp.stack([ln2_g, ln2_b], 1)
    for l in range(DEPTH):
        mod3 = mods[l].reshape(n_mod, 1, 6 * D)
        p_lat, pn_lat = _inproj_call(x, mod3, w_p[l], b_p[l][None], None)
        p_ctx, pn_ctx = _inproj_call(ctx, mod3, w_p[l], b_p[l][None], B)
        n_lat, n_ctx = _natten_call(pn_lat, pn_ctx, bias[l])
        m_lat, m_ctx = _mlstm_call(p_lat, p_ctx, mcos, msin, conv_w[l])
        g_lat, g_ctx = _gla_call(p_lat, p_ctx, gcos, gsin, w2p[l], b2p[l])
        x1t, lg = _merge_call(m_lat, g_lat, n_lat, p_lat, x, mod3, wout_b[l], wrt_b[l], ng[l], ln1[l], None)
        x = _moe(x1t, lg, mod3, wg_b, wu_b, wd_b, l, ln2[l], None)
        if l < DEPTH - 1:
            c1t, lgc = _merge_call(m_ctx, g_ctx, n_ctx, p_ctx, ctx, mod3, wout_b[l], wrt_b[l], ng[l], ln1[l], B)
            ctx = _moe(c1t, lgc, mod3, wg_b, wu_b, wd_b, l, ln2[l], B)
    return x
```
